```python
import math
import jax, jax.numpy as jnp
from jax import lax
import numpy as np

D_MODEL = 2048
BATCH = 4
SEQ = 2048
DEPTH = 2

N_BRANCH = 3
RMS_EPS = 1e-6
NEG_INF = -1e30
FORCE_SCORE = 1e9

GLA_HEADS = 4
GLA_DK = D_MODEL // 2
GLA_DV = D_MODEL
GLA_HK = GLA_DK // GLA_HEADS
GLA_HV = GLA_DV // GLA_HEADS
GLA_LOWRANK = 16
GLA_GATE_NORMALIZER = 16.0
GLA_CHUNK = 64

NSA_HEAD_DIM = 128
NSA_HEADS = D_MODEL // NSA_HEAD_DIM
NSA_KV_GROUPS = 4
NSA_Q_PER_KV = NSA_HEADS // NSA_KV_GROUPS
NSA_WIDTH = NSA_HEADS * NSA_HEAD_DIM
NSA_KV_WIDTH = NSA_KV_GROUPS * NSA_HEAD_DIM
CMP_LEN = 32
CMP_STRIDE = 16
CMP_HIDDEN = 2 * NSA_HEAD_DIM
SEL_BLOCK = 64
SEL_TOPK = 16
SEL_QBLOCK = 16
WINDOW = 512
WIN_QBLOCK = 128
ROPE_THETA = 500000.0
ROPE_DIM = NSA_HEAD_DIM // 4

SSD_D_INNER = 2 * D_MODEL
SSD_HEAD_DIM = 64
SSD_HEADS = SSD_D_INNER // SSD_HEAD_DIM
SSD_GROUPS = 8
SSD_HEADS_PER_GROUP = SSD_HEADS // SSD_GROUPS
SSD_D_STATE = 128
SSD_CONV = 4
SSD_CHUNK = 64
SSD_CONV_DIM = SSD_D_INNER + 2 * SSD_GROUPS * SSD_D_STATE

D_FF = ((8 * D_MODEL + 3 * 256 - 1) // (3 * 256)) * 256

MIX_WIDTH = GLA_DV + NSA_WIDTH + SSD_D_INNER
IN_SIZES = (N_BRANCH * D_MODEL, GLA_DK, GLA_DK, GLA_DV, GLA_LOWRANK, GLA_DV,
            NSA_WIDTH, 6 * NSA_KV_WIDTH, 3 * NSA_HEADS,
            SSD_D_INNER, SSD_CONV_DIM, SSD_HEADS)
IN_COLS = sum(IN_SIZES)

kernel_name = 'hybrid_gla_nsa_ssd_block'


def rmsnorm(x, g):
    xf = x.astype(jnp.float32)
    y = xf * lax.rsqrt(jnp.mean(xf * xf, axis=-1, keepdims=True) + RMS_EPS)
    return (y * g.astype(jnp.float32)).astype(x.dtype)


def partial_rope(t, pos):
    half = ROPE_DIM // 2
    inv_freq = ROPE_THETA ** (-jnp.arange(half, dtype=jnp.float32) / half)
    ang = pos.astype(jnp.float32)[:, None] * inv_freq[None, :]
    cos, sin = jnp.cos(ang), jnp.sin(ang)
    t1, t2, rest = t[..., :half], t[..., half:ROPE_DIM], t[..., ROPE_DIM:]
    return jnp.concatenate([t1 * cos - t2 * sin, t2 * cos + t1 * sin, rest], axis=-1)


def gla_mixer(q, k, v, g_low, r, gate_w2, gate_b, norm_g):
    f32 = jnp.float32
    bsz, L, _ = q.shape
    nc = L // GLA_CHUNK
    log_a = jax.nn.log_sigmoid((g_low @ gate_w2 + gate_b).astype(f32)) / GLA_GATE_NORMALIZER

    def to_chunks(t, d):
        return t.astype(f32).reshape(bsz, nc, GLA_CHUNK, GLA_HEADS, d).transpose(1, 0, 3, 2, 4)

    qc = to_chunks(q, GLA_HK) * (GLA_HK ** -0.5)
    kc = to_chunks(k, GLA_HK)
    vc = to_chunks(v, GLA_HV)
    ac = to_chunks(log_a, GLA_HK)
    causal = jnp.tril(jnp.ones((GLA_CHUNK, GLA_CHUNK), bool))[:, :, None]

    def step(state, inp):
        qi, ki, vi, ai = inp
        b = jnp.cumsum(ai, axis=-2)
        diff = b[:, :, :, None, :] - b[:, :, None, :, :]
        decay = jnp.exp(jnp.where(causal, diff, -jnp.inf))
        scores = jnp.einsum('bhid,bhijd->bhij', qi, ki[:, :, None, :, :] * decay)
        out = (jnp.einsum('bhij,bhjv->bhiv', scores, vi)
               + jnp.einsum('bhid,bhdv->bhiv', qi * jnp.exp(b), state))
        b_last = b[:, :, -1:, :]
        new_state = (jnp.exp(b_last[:, :, 0, :])[..., None] * state
                     + jnp.einsum('bhjd,bhjv->bhdv', ki * jnp.exp(b_last - b), vi))
        return new_state, out

    s0 = jnp.zeros((bsz, GLA_HEADS, GLA_HK, GLA_HV), f32)
    _, o = lax.scan(step, s0, (qc, kc, vc, ac))
    o = o.transpose(1, 0, 3, 2, 4).reshape(bsz, L, GLA_HEADS, GLA_HV)
    o = rmsnorm(o, norm_g).reshape(bsz, L, GLA_DV) * jax.nn.silu(r.astype(f32))
    return o.astype(q.dtype)


def nsa_mixer(q, kv, g, pos_k, pos_v, wk1, wk2, wv1, wv2):
    f32 = jnp.float32
    bsz, L, _ = q.shape
    G, R, hd = NSA_KV_GROUPS, NSA_Q_PER_KV, NSA_HEAD_DIM
    scale = hd ** -0.5
    pos = jnp.arange(L)
    qh = partial_rope(q.astype(f32).reshape(bsz, L, G, R, hd).transpose(0, 2, 3, 1, 4), pos)

    def kv_heads(t):
        return t.astype(f32).reshape(bsz, L, G, hd).transpose(0, 2, 1, 3)

    k_cmp, v_cmp, k_slc, v_slc, k_win, v_win = [kv_heads(t) for t in jnp.split(kv, 6, axis=-1)]
    k_cmp = partial_rope(k_cmp, pos)
    k_slc = partial_rope(k_slc, pos)
    k_win = partial_rope(k_win, pos)

    n_cmp = (L - CMP_LEN) // CMP_STRIDE + 1
    cmp_starts = np.arange(n_cmp) * CMP_STRIDE
    cmp_idx = cmp_starts[:, None] + np.arange(CMP_LEN)[None, :]

    def compress(t, pe, w1, w2):
        blocks = (t[:, :, cmp_idx, :] + pe.astype(f32)).reshape(bsz, G, n_cmp, CMP_LEN * hd)
        return jax.nn.silu(blocks @ w1.astype(f32)) @ w2.astype(f32)

    kc = compress(k_cmp, pos_k, wk1, wk2)
    vc = compress(v_cmp, pos_v, wv1, wv2)
    cmp_ok = (cmp_starts + CMP_LEN - 1)[None, :] <= np.arange(L)[:, None]
    s_cmp = jnp.einsum('bgrld,bgnd->bgrln', qh, kc) * scale
    p_cmp = jax.nn.softmax(jnp.where(cmp_ok, s_cmp, NEG_INF), axis=-1) * cmp_ok
    o_cmp = jnp.einsum('bgrln,bgnd->bgrld', p_cmp, vc)

    n_sel = L // SEL_BLOCK
    k_eff = min(SEL_TOPK, n_sel)
    sel_starts = np.arange(n_sel) * SEL_BLOCK
    overlap = np.clip(np.minimum(cmp_starts[:, None] + CMP_LEN, sel_starts[None, :] + SEL_BLOCK)
                      - np.maximum(cmp_starts[:, None], sel_starts[None, :]), 0, None).astype(np.float32) / CMP_LEN
    p_slc = jnp.einsum('bgrln,nj->bglj', p_cmp, jnp.asarray(overlap))
    blk_t = (np.arange(L) // SEL_BLOCK)[:, None]
    blk_j = np.arange(n_sel)[None, :]
    forced = (blk_j == 0) | (blk_j == blk_t) | (blk_j == blk_t - 1)
    sel_score = jnp.where(forced, FORCE_SCORE, jnp.where(blk_j <= blk_t, p_slc, NEG_INF))
    _, sel_idx = lax.top_k(sel_score, k_eff)

    kb = k_slc.reshape(bsz, G, n_sel, SEL_BLOCK, hd)
    vb = v_slc.reshape(bsz, G, n_sel, SEL_BLOCK, hd)
    nqb = L // SEL_QBLOCK
    q_blk = qh.reshape(bsz, G, R, nqb, SEL_QBLOCK, hd).transpose(3, 0, 1, 2, 4, 5)
    idx_blk = sel_idx.reshape(bsz, G, nqb, SEL_QBLOCK, k_eff).transpose(2, 0, 1, 3, 4)
    pos_blk = pos.reshape(nqb, SEL_QBLOCK)
    gather = jax.vmap(jax.vmap(lambda blocks, ids: blocks[ids]))
    in_block = jnp.arange(SEL_BLOCK)
    n_keys = k_eff * SEL_BLOCK

    def sel_attend(args):
        qb, ib, pb = args
        ks = gather(kb, ib)
        vs = gather(vb, ib)
        kpos = ib[..., None] * SEL_BLOCK + in_block
        ok = (kpos <= pb[:, None, None]).reshape(bsz, G, 1, SEL_QBLOCK, n_keys)
        s = jnp.einsum('bgrqd,bgqksd->bgrqks', qb, ks).reshape(bsz, G, R, SEL_QBLOCK, n_keys) * scale
        p = jax.nn.softmax(jnp.where(ok, s, NEG_INF), axis=-1)
        return jnp.einsum('bgrqt,bgqtd->bgrqd', p, vs.reshape(bsz, G, SEL_QBLOCK, n_keys, hd))

    o_sel = lax.map(sel_attend, (q_blk, idx_blk, pos_blk))
    o_sel = o_sel.transpose(1, 2, 3, 0, 4, 5).reshape(bsz, G, R, L, hd)

    n_prev = WINDOW // WIN_QBLOCK
    nwb = L // WIN_QBLOCK
    kw_len = (n_prev + 1) * WIN_QBLOCK

    def band(t):
        tp = jnp.pad(t, ((0, 0), (0, 0), (WINDOW, 0), (0, 0))).reshape(bsz, G, nwb + n_prev, WIN_QBLOCK, hd)
        return jnp.concatenate([tp[:, :, j:j + nwb] for j in range(n_prev + 1)], axis=3)

    qpos = np.arange(L).reshape(nwb, WIN_QBLOCK)[:, :, None]
    kpos = (np.arange(nwb)[:, None] * WIN_QBLOCK + np.arange(kw_len)[None, :] - WINDOW)[:, None, :]
    win_ok = (kpos <= qpos) & (kpos > qpos - WINDOW) & (kpos >= 0)
    s_win = jnp.einsum('bgrnqd,bgnkd->bgrnqk', qh.reshape(bsz, G, R, nwb, WIN_QBLOCK, hd), band(k_win)) * scale
    p_win = jax.nn.softmax(jnp.where(win_ok, s_win, NEG_INF), axis=-1)
    o_win = jnp.einsum('bgrnqk,bgnkd->bgrnqd', p_win, band(v_win)).reshape(bsz, G, R, L, hd)

    gates = jax.nn.sigmoid(g.astype(f32)).reshape(bsz, L, G, R, 3).transpose(0, 2, 3, 1, 4)
    o = gates[..., 0:1] * o_cmp + gates[..., 1:2] * o_sel + gates[..., 2:3] * o_win
    return o.transpose(0, 3, 1, 2, 4).reshape(bsz, L, NSA_WIDTH).astype(q.dtype)


def ssd_mixer(z, xbc, dt_raw, conv_w, conv_b, dt_bias, a_log, d_skip, norm_g):
    f32 = jnp.float32
    bsz, L, _ = z.shape
    G, R, P, N, C = SSD_GROUPS, SSD_HEADS_PER_GROUP, SSD_HEAD_DIM, SSD_D_STATE, SSD_CHUNK
    nc = L // C
    conv = lax.conv_general_dilated(
        xbc, conv_w[:, None, :].astype(xbc.dtype), window_strides=(1,),
        padding=[(SSD_CONV - 1, 0)], dimension_numbers=('NWC', 'WIO', 'NWC'),
        feature_group_count=SSD_CONV_DIM)
    xbc = jax.nn.silu((conv + conv_b).astype(f32))
    xs, bm, cm = jnp.split(xbc, (SSD_D_INNER, SSD_D_INNER + G * N), axis=-1)
    xs = xs.reshape(bsz, L, G, R, P)
    dt = jax.nn.softplus(dt_raw.astype(f32) + dt_bias.astype(f32)).reshape(bsz, L, G, R)
    a = (-jnp.exp(a_log.astype(f32))).reshape(G, R)
    xdt = (xs * dt[..., None]).reshape(bsz, nc, C, G, R, P).transpose(1, 0, 2, 3, 4, 5)
    adt = (dt * a).reshape(bsz, nc, C, G, R).transpose(1, 0, 3, 4, 2)
    bc = bm.reshape(bsz, nc, C, G, N).transpose(1, 0, 2, 3, 4)
    cc = cm.reshape(bsz, nc, C, G, N).transpose(1, 0, 2, 3, 4)
    causal = jnp.tril(jnp.ones((C, C), bool))

    def step(state, inp):
        xi, ai, bi, ci = inp
        acum = jnp.cumsum(ai, axis=-1)
        seg = jnp.exp(jnp.where(causal, acum[..., :, None] - acum[..., None, :], -jnp.inf))
        cb = jnp.einsum('blgn,bsgn->bgls', ci, bi)
        y_diag = jnp.einsum('bgrls,bsgrp->blgrp', cb[:, :, None] * seg, xi)
        y_off = jnp.einsum('blgn,bgrpn->blgrp', ci, state) * jnp.exp(acum).transpose(0, 3, 1, 2)[..., None]
        decay = jnp.exp(acum[..., -1:] - acum).transpose(0, 3, 1, 2)[..., None]
        new_state = (state * jnp.exp(acum[..., -1])[..., None, None]
                     + jnp.einsum('bsgn,bsgrp->bgrpn', bi, xi * decay))
        return new_state, y_diag + y_off

    s0 = jnp.zeros((bsz, G, R, P, N), f32)
    _, ys = lax.scan(step, s0, (xdt, adt, bc, cc))
    y = ys.transpose(1, 0, 2, 3, 4, 5).reshape(bsz, L, G, R, P) + xs * d_skip.astype(f32).reshape(G, R)[..., None]
    y = y.reshape(bsz, L, G, R * P) * jax.nn.silu(z.astype(f32)).reshape(bsz, L, G, R * P)
    y = y * lax.rsqrt(jnp.mean(y * y, axis=-1, keepdims=True) + RMS_EPS)
    return (y.reshape(bsz, L, SSD_D_INNER) * norm_g.astype(f32)).astype(z.dtype)


def hybrid_layer(x, norm_mix, w_in, gla_gate_w2, gla_gate_b, gla_out_norm,
                 nsa_cmp_pos_k, nsa_cmp_pos_v, nsa_cmp_k_w1, nsa_cmp_k_w2, nsa_cmp_v_w1, nsa_cmp_v_w2,
                 ssd_conv_w, ssd_conv_b, ssd_dt_bias, ssd_a_log, ssd_d, ssd_out_norm,
                 w_branch, w_out, norm_ffn, w_ffn_gate, w_ffn_up, w_ffn_down):
    bsz, L, _ = x.shape
    h = rmsnorm(x, norm_mix)
    split_at = tuple(int(i) for i in np.cumsum(IN_SIZES)[:-1])
    (gate_cols, gla_q, gla_k, gla_v, gla_low, gla_r,
     nsa_q, nsa_kv, nsa_g, ssd_z, ssd_xbc, ssd_dt) = jnp.split(h @ w_in, split_at, axis=-1)
    o_gla = gla_mixer(gla_q, gla_k, gla_v, gla_low, gla_r, gla_gate_w2, gla_gate_b, gla_out_norm).astype(x.dtype)
    o_nsa = nsa_mixer(nsa_q, nsa_kv, nsa_g, nsa_cmp_pos_k, nsa_cmp_pos_v,
                      nsa_cmp_k_w1, nsa_cmp_k_w2, nsa_cmp_v_w1, nsa_cmp_v_w2).astype(x.dtype)
    o_ssd = ssd_mixer(ssd_z, ssd_xbc, ssd_dt, ssd_conv_w, ssd_conv_b,
                      ssd_dt_bias, ssd_a_log, ssd_d, ssd_out_norm).astype(x.dtype)
    gates = jax.nn.sigmoid(gate_cols.astype(jnp.float32)).astype(x.dtype).reshape(bsz, L, N_BRANCH, D_MODEL)
    u_gla = o_gla @ w_branch[:GLA_DV]
    u_nsa = o_nsa @ w_branch[GLA_DV:GLA_DV + NSA_WIDTH]
    u_ssd = o_ssd @ w_branch[GLA_DV + NSA_WIDTH:]
    merged = gates[:, :, 0] * u_gla + gates[:, :, 1] * u_nsa + gates[:, :, 2] * u_ssd
    x = x + merged @ w_out
    h2 = rmsnorm(x, norm_ffn)
    return x + (jax.nn.silu(h2 @ w_ffn_gate) * (h2 @ w_ffn_up)) @ w_ffn_down


def setup_inputs(seed: int = 0) -> dict:
    key = jax.random.key(seed)
    keys = list(jax.random.split(key, 32))
    f32 = jnp.float32

    def nrm(k, shape, scale):
        return jax.random.normal(k, shape, f32) * scale

    def gain(k, shape):
        return 1.0 + nrm(k, shape, 0.01)

    dt0 = jnp.exp(jax.random.uniform(keys[16], (DEPTH, SSD_HEADS), f32, math.log(1e-3), math.log(1e-1)))
    w_branch = jnp.concatenate([
        nrm(keys[20], (DEPTH, GLA_DV, D_MODEL), GLA_DV ** -0.5),
        nrm(keys[21], (DEPTH, NSA_WIDTH, D_MODEL), NSA_WIDTH ** -0.5),
        nrm(keys[22], (DEPTH, SSD_D_INNER, D_MODEL), SSD_D_INNER ** -0.5)], axis=1)
    return {
        'x': nrm(keys[0], (BATCH, SEQ, D_MODEL), 1.0),
        'norm_mix': gain(keys[1], (DEPTH, D_MODEL)),
        'w_in': nrm(keys[2], (DEPTH, D_MODEL, IN_COLS), D_MODEL ** -0.5),
        'gla_gate_w2': nrm(keys[3], (DEPTH, GLA_LOWRANK, GLA_DK), GLA_LOWRANK ** -0.5),
        'gla_gate_b': nrm(keys[4], (DEPTH, GLA_DK), 0.01),
        'gla_out_norm': gain(keys[5], (DEPTH, GLA_HV)),
        'nsa_cmp_pos_k': nrm(keys[6], (DEPTH, CMP_LEN, NSA_HEAD_DIM), 0.02),
        'nsa_cmp_pos_v': nrm(keys[7], (DEPTH, CMP_LEN, NSA_HEAD_DIM), 0.02),
        'nsa_cmp_k_w1': nrm(keys[8], (DEPTH, CMP_LEN * NSA_HEAD_DIM, CMP_HIDDEN), (CMP_LEN * NSA_HEAD_DIM) ** -0.5),
        'nsa_cmp_k_w2': nrm(keys[9], (DEPTH, CMP_HIDDEN, NSA_HEAD_DIM), CMP_HIDDEN ** -0.5),
        'nsa_cmp_v_w1': nrm(keys[10], (DEPTH, CMP_LEN * NSA_HEAD_DIM, CMP_HIDDEN), (CMP_LEN * NSA_HEAD_DIM) ** -0.5),
        'nsa_cmp_v_w2': nrm(keys[11], (DEPTH, CMP_HIDDEN, NSA_HEAD_DIM), CMP_HIDDEN ** -0.5),
        'ssd_conv_w': nrm(keys[12], (DEPTH, SSD_CONV, SSD_CONV_DIM), SSD_CONV ** -0.5),
        'ssd_conv_b': nrm(keys[13], (DEPTH, SSD_CONV_DIM), 0.01),
        'ssd_dt_bias': dt0 + jnp.log(-jnp.expm1(-dt0)),
        'ssd_a_log': jnp.log(jax.random.uniform(keys[14], (DEPTH, SSD_HEADS), f32, 1.0, 16.0)),
        'ssd_d': gain(keys[15], (DEPTH, SSD_HEADS)),
        'ssd_out_norm': gain(keys[17], (DEPTH, SSD_D_INNER)),
        'w_branch': w_branch,
        'w_out': nrm(keys[23], (DEPTH, D_MODEL, D_MODEL), D_MODEL ** -0.5),
        'norm_ffn': gain(keys[24], (DEPTH, D_MODEL)),
        'w_ffn_gate': nrm(keys[25], (DEPTH, D_MODEL, D_FF), D_MODEL ** -0.5),
        'w_ffn_up': nrm(keys[26], (DEPTH, D_MODEL, D_FF), D_MODEL ** -0.5),
        'w_ffn_down': nrm(keys[27], (DEPTH, D_FF, D_MODEL), D_FF ** -0.5),
        'norm_final': gain(keys[28], (D_MODEL,)),
    }


def reference(x, norm_mix, w_in, gla_gate_w2, gla_gate_b, gla_out_norm,
              nsa_cmp_pos_k, nsa_cmp_pos_v, nsa_cmp_k_w1, nsa_cmp_k_w2, nsa_cmp_v_w1, nsa_cmp_v_w2,
              ssd_conv_w, ssd_conv_b, ssd_dt_bias, ssd_a_log, ssd_d, ssd_out_norm,
              w_branch, w_out, norm_ffn, w_ffn_gate, w_ffn_up, w_ffn_down, norm_final):
    for l in range(DEPTH):
        x = hybrid_layer(x, norm_mix[l], w_in[l], gla_gate_w2[l], gla_gate_b[l], gla_out_norm[l],
                         nsa_cmp_pos_k[l], nsa_cmp_pos_v[l], nsa_cmp_k_w1[l], nsa_cmp_k_w2[l],
                         nsa_cmp_v_w1[l], nsa_cmp_v_w2[l],
                         ssd_conv_w[l], ssd_conv_b[l], ssd_dt_bias[l], ssd_a_log[l], ssd_d[l], ssd_out_norm[l],
                         w_branch[l], w_out[l], norm_ffn[l], w_ffn_gate[l], w_ffn_up[l], w_ffn_down[l])
    return rmsnorm(x, norm_final)
```

```python
import functools

import numpy as np
import jax
import jax.numpy as jnp
from jax import lax
from jax.experimental import pallas as pl
from jax.experimental.pallas import tpu as pltpu

F32 = jnp.float32
BF16 = jnp.bfloat16

D_MODEL = 2048
RMS_EPS = 1e-6
NEG_INF = -1e30
FORCE_SCORE = 1e9

GLA_HEADS = 4
GLA_DK = D_MODEL // 2
GLA_DV = D_MODEL
GLA_HK = GLA_DK // GLA_HEADS
GLA_HV = GLA_DV // GLA_HEADS
GLA_LOWRANK = 16
GLA_GATE_NORMALIZER = 16.0

NSA_HEAD_DIM = 128
NSA_HEADS = D_MODEL // NSA_HEAD_DIM
NSA_KV_GROUPS = 4
NSA_Q_PER_KV = NSA_HEADS // NSA_KV_GROUPS
NSA_WIDTH = NSA_HEADS * NSA_HEAD_DIM
NSA_KV_WIDTH = NSA_KV_GROUPS * NSA_HEAD_DIM
CMP_LEN = 32
CMP_STRIDE = 16
CMP_HIDDEN = 2 * NSA_HEAD_DIM
SEL_BLOCK = 64
SEL_SHIFT = 6
SEL_TOPK = 16
WINDOW = 512
ROPE_THETA = 500000.0
ROPE_DIM = NSA_HEAD_DIM // 4

SSD_D_INNER = 2 * D_MODEL
SSD_HEAD_DIM = 64
SSD_HEAD_SHIFT = 6
SSD_HEADS = SSD_D_INNER // SSD_HEAD_DIM
SSD_GROUPS = 8
SSD_HEADS_PER_GROUP = SSD_HEADS // SSD_GROUPS
SSD_D_STATE = 128
SSD_CONV = 4
SSD_GROUP_WIDTH = SSD_HEADS_PER_GROUP * SSD_HEAD_DIM
SSD_CONV_DIM = SSD_D_INNER + 2 * SSD_GROUPS * SSD_D_STATE

D_FF = ((8 * D_MODEL + 3 * 256 - 1) // (3 * 256)) * 256

LANES = 128
V7X_VMEM_LIMIT_BYTES = 56 * 1024 * 1024

_REF_SIZES = (3 * D_MODEL, GLA_DK, GLA_DK, GLA_DV, GLA_LOWRANK, GLA_DV,
              NSA_WIDTH, 6 * NSA_KV_WIDTH, 3 * NSA_HEADS,
              SSD_D_INNER, SSD_CONV_DIM, SSD_HEADS)
_REF_OFF = tuple(int(v) for v in np.concatenate([[0], np.cumsum(_REF_SIZES)]))
(_R_GATE, _R_GQ, _R_GK, _R_GV, _R_GLOW, _R_GR, _R_NQ, _R_NKV, _R_NG,
 _R_SZ, _R_SXBC, _R_SDT) = _REF_OFF[:-1]
P_GATE = 0
P_GQ = P_GATE + 3 * D_MODEL
P_GK = P_GQ + GLA_DK
P_GV = P_GK + GLA_DK
P_GR = P_GV + GLA_DV
P_NQ = P_GR + GLA_DV
P_NKV = P_NQ + NSA_WIDTH
P_SZ = P_NKV + 6 * NSA_KV_WIDTH
P_SXBC = P_SZ + SSD_D_INNER
P_SMALL = P_SXBC + SSD_CONV_DIM
SMALL_GLOW = 0
SMALL_NG = GLA_LOWRANK
SMALL_DT = GLA_LOWRANK + 3 * NSA_HEADS
P_USED = P_SMALL + LANES
IN_PROJ_TILE_N = 1024
P_COLS = -(-P_USED // IN_PROJ_TILE_N) * IN_PROJ_TILE_N


def _params(sem, vmem=V7X_VMEM_LIMIT_BYTES):
    return pltpu.CompilerParams(dimension_semantics=sem, vmem_limit_bytes=vmem)


def _sigmoid(x):
    return 1.0 / (1.0 + jnp.exp(-x))


def _silu(x):
    return x * _sigmoid(x)


def _softplus(x):
    return jnp.maximum(x, 0.0) + jnp.log(1.0 + jnp.exp(-jnp.abs(x)))


def _split3(x):
    hi = x.astype(BF16)
    r1 = x - hi.astype(F32)
    mid = r1.astype(BF16)
    lo = (r1 - mid.astype(F32)).astype(BF16)
    return hi, mid, lo


def _dot(a, b):
    return jnp.dot(a, b, preferred_element_type=F32)


def _dot_nt(a, b):
    return lax.dot_general(a, b, (((1,), (1,)), ((), ())), preferred_element_type=F32)


def _dot_tn(a, b):
    return lax.dot_general(a, b, (((0,), (0,)), ((), ())), preferred_element_type=F32)


def _sel_dot(sel, x):
    hi, mid, lo = _split3(x)
    return _dot(sel, hi) + _dot(sel, mid) + _dot(sel, lo)


def _sel_dot_left(x, sel):
    hi, mid, lo = _split3(x)
    return _dot(hi, sel) + _dot(mid, sel) + _dot(lo, sel)


def _rms_rows(x, gain):
    ms = jnp.mean(x * x, axis=-1, keepdims=True)
    return x * lax.rsqrt(ms + RMS_EPS) * gain


def _in_proj_kernel(x_ref, g_ref, w_ref, o_ref, h_ref):
    @pl.when(pl.program_id(1) == 0)
    def _():
        h_ref[...] = _rms_rows(x_ref[...], g_ref[...]).astype(BF16)

    o_ref[...] = _dot(h_ref[...], w_ref[...]).astype(o_ref.dtype)


def _in_proj(x2d, gain, w, tm, tn):
    t, d = x2d.shape
    n = w.shape[1]
    return pl.pallas_call(
        _in_proj_kernel,
        out_shape=jax.ShapeDtypeStruct((t, n), F32),
        grid=(t // tm, n // tn),
        in_specs=[pl.BlockSpec((tm, d), lambda i, j: (i, 0)),
                  pl.BlockSpec((1, d), lambda i, j: (0, 0)),
                  pl.BlockSpec((d, tn), lambda i, j: (0, j))],
        out_specs=pl.BlockSpec((tm, tn), lambda i, j: (i, j)),
        scratch_shapes=[pltpu.VMEM((tm, d), BF16)],
        compiler_params=_params(("parallel", "arbitrary")),
        name="in_proj",
    )(x2d, gain.reshape(1, d), w)


def _merge_kernel(og_ref, on_ref, os_ref, g0_ref, g1_ref, g2_ref,
                  wb0_ref, wb1_ref, wb2_ref, wo_ref, x_ref, o_ref):
    @pl.when(pl.program_id(1) == 0)
    def _():
        o_ref[...] = x_ref[...]

    m = (_sigmoid(g0_ref[...]) * _dot(og_ref[...], wb0_ref[...])
         + _sigmoid(g1_ref[...]) * _dot(on_ref[...], wb1_ref[...])
         + _sigmoid(g2_ref[...]) * _dot(os_ref[...], wb2_ref[...]))
    o_ref[...] += _dot(m.astype(BF16), wo_ref[...])


def _merge(o_gla, o_nsa, o_ssd, p, wb, wo, x2d, tm, tn):
    t, d = x2d.shape
    nj = d // tn
    gate_blk0 = P_GATE // tn
    return pl.pallas_call(
        _merge_kernel,
        out_shape=jax.ShapeDtypeStruct((t, d), F32),
        grid=(t // tm, nj),
        in_specs=[pl.BlockSpec((tm, GLA_DV), lambda i, j: (i, 0)),
                  pl.BlockSpec((tm, NSA_WIDTH), lambda i, j: (i, 0)),
                  pl.BlockSpec((tm, SSD_D_INNER), lambda i, j: (i, 0)),
                  pl.BlockSpec((tm, tn), lambda i, j: (i, gate_blk0 + j)),
                  pl.BlockSpec((tm, tn), lambda i, j: (i, gate_blk0 + nj + j)),
                  pl.BlockSpec((tm, tn), lambda i, j: (i, gate_blk0 + 2 * nj + j)),
                  pl.BlockSpec((GLA_DV, tn), lambda i, j: (0, j)),
                  pl.BlockSpec((NSA_WIDTH, tn), lambda i, j: (1, j)),
                  pl.BlockSpec((SSD_D_INNER, tn), lambda i, j: (1, j)),
                  pl.BlockSpec((tn, d), lambda i, j: (j, 0)),
                  pl.BlockSpec((tm, d), lambda i, j: (i, 0))],
        out_specs=pl.BlockSpec((tm, d), lambda i, j: (i, 0)),
        compiler_params=_params(("parallel", "arbitrary")),
        name="branch_merge",
    )(o_gla, o_nsa, o_ssd, p, p, p, wb, wb, wb, wo, x2d)


def _ffn_kernel(x_ref, g_ref, gf_ref, wg_ref, wu_ref, wd_ref, o_ref, h_ref, *, final_norm):
    j = pl.program_id(1)

    @pl.when(j == 0)
    def _():
        x = x_ref[...]
        h_ref[...] = _rms_rows(x, g_ref[...]).astype(BF16)
        o_ref[...] = x

    h = h_ref[...]
    a = _silu(_dot(h, wg_ref[...])) * _dot(h, wu_ref[...])
    o_ref[...] += _dot(a.astype(BF16), wd_ref[...])

    if final_norm:
        @pl.when(j == pl.num_programs(1) - 1)
        def _():
            o_ref[...] = _rms_rows(o_ref[...], gf_ref[...])


def _ffn(x2d, gain, gain_final, wg, wu, wd, tm, tf, final_norm):
    t, d = x2d.shape
    f = wg.shape[1]
    return pl.pallas_call(
        functools.partial(_ffn_kernel, final_norm=final_norm),
        out_shape=jax.ShapeDtypeStruct((t, d), F32),
        grid=(t // tm, f // tf),
        in_specs=[pl.BlockSpec((tm, d), lambda i, j: (i, 0)),
                  pl.BlockSpec((1, d), lambda i, j: (0, 0)),
                  pl.BlockSpec((1, d), lambda i, j: (0, 0)),
                  pl.BlockSpec((d, tf), lambda i, j: (0, j)),
                  pl.BlockSpec((d, tf), lambda i, j: (0, j)),
                  pl.BlockSpec((tf, d), lambda i, j: (j, 0))],
        out_specs=pl.BlockSpec((tm, d), lambda i, j: (i, 0)),
        scratch_shapes=[pltpu.VMEM((tm, d), BF16)],
        compiler_params=_params(("parallel", "arbitrary")),
        name="ffn",
    )(x2d, gain.reshape(1, d), gain_final.reshape(1, d), wg, wu, wd)


GLA_CHUNK = 64
GLA_SUB = 16
GLA_EXP_CAP = 88.0


def _gla_kernel(q_ref, k_ref, v_ref, r_ref, sm_ref, w2_ref, gb_ref, ng_ref, o_ref, st_ref,
                *, chunks_per_tile):
    @pl.when(pl.program_id(2) == 0)
    def _():
        st_ref[...] = jnp.zeros_like(st_ref)

    c_rows = GLA_CHUNK
    row = lax.broadcasted_iota(jnp.int32, (c_rows, c_rows), 0)
    col = lax.broadcasted_iota(jnp.int32, (c_rows, c_rows), 1)
    tri = (col <= row).astype(BF16)
    srow = lax.broadcasted_iota(jnp.int32, (GLA_SUB, c_rows), 0)
    scol = lax.broadcasted_iota(jnp.int32, (GLA_SUB, c_rows), 1)
    scale = GLA_HK ** -0.5

    for c in range(chunks_per_tile):
        rows = pl.ds(c * c_rows, c_rows)
        ga = _dot(sm_ref[rows, :].astype(BF16), w2_ref[...]) + gb_ref[...]
        log_a = -_softplus(-ga) * (1.0 / GLA_GATE_NORMALIZER)
        bc = _sel_dot(tri, log_a)
        q = q_ref[rows, :] * scale
        k = k_ref[rows, :]
        vb = v_ref[rows, :].astype(BF16)
        st = st_ref[...]

        inter = _dot_nt((q * jnp.exp(bc)).astype(BF16), st.astype(BF16))
        parts = []
        for i in range(c_rows // GLA_SUB):
            lo = i * GLA_SUB
            ref_pt = bc[lo - 1:lo, :] if i > 0 else jnp.zeros((1, GLA_HK), F32)
            kt = (k * jnp.exp(jnp.minimum(ref_pt - bc, GLA_EXP_CAP))).astype(BF16)
            qi = (q[lo:lo + GLA_SUB, :] * jnp.exp(bc[lo:lo + GLA_SUB, :] - ref_pt)).astype(BF16)
            sc = _dot_nt(qi, kt)
            parts.append(jnp.where(scol <= srow + lo, sc, 0.0))
        att = jnp.concatenate(parts, axis=0)
        out = inter + _dot(att.astype(BF16), vb)

        b_last = bc[c_rows - 1:c_rows, :]
        khat = (k * jnp.exp(b_last - bc)).astype(BF16)
        st_ref[...] = st * jnp.exp(b_last) + _dot_tn(vb, khat)

        y = _rms_rows(out, ng_ref[...]) * _silu(r_ref[rows, :])
        o_ref[rows, :] = y.astype(o_ref.dtype)


def _gla(p, w2pad, gate_b, norm_g, bsz, seq, tile):
    t = bsz * seq
    nt = seq // tile
    qb, kb = P_GQ // GLA_HK, P_GK // GLA_HK
    vb, rb = P_GV // GLA_HV, P_GR // GLA_HV
    smb = P_SMALL // LANES
    return pl.pallas_call(
        functools.partial(_gla_kernel, chunks_per_tile=tile // GLA_CHUNK),
        out_shape=jax.ShapeDtypeStruct((t, GLA_DV), BF16),
        grid=(bsz, GLA_HEADS, nt),
        in_specs=[pl.BlockSpec((tile, GLA_HK), lambda b, h, c: (b * nt + c, qb + h)),
                  pl.BlockSpec((tile, GLA_HK), lambda b, h, c: (b * nt + c, kb + h)),
                  pl.BlockSpec((tile, GLA_HV), lambda b, h, c: (b * nt + c, vb + h)),
                  pl.BlockSpec((tile, GLA_HV), lambda b, h, c: (b * nt + c, rb + h)),
                  pl.BlockSpec((tile, LANES), lambda b, h, c: (b * nt + c, smb)),
                  pl.BlockSpec((LANES, GLA_HK), lambda b, h, c: (0, h)),
                  pl.BlockSpec((1, GLA_HK), lambda b, h, c: (0, h)),
                  pl.BlockSpec((1, GLA_HV), lambda b, h, c: (0, 0))],
        out_specs=pl.BlockSpec((tile, GLA_HV), lambda b, h, c: (b * nt + c, h)),
        scratch_shapes=[pltpu.VMEM((GLA_HV, GLA_HK), F32)],
        compiler_params=_params(("parallel", "parallel", "arbitrary")),
        name="gla_mixer",
    )(p, p, p, p, p, w2pad, gate_b.reshape(1, GLA_DK), norm_g.reshape(1, GLA_HV))


def _causal_conv(cur, prev, w_ref, b_ref):
    row = lax.broadcasted_iota(jnp.int32, cur.shape, 0)
    acc = cur * w_ref[SSD_CONV - 1:SSD_CONV, :] + b_ref[...]
    for s in range(1, SSD_CONV):
        shifted = jnp.where(row < s, pltpu.roll(prev, s, axis=0), pltpu.roll(cur, s, axis=0))
        acc = acc + shifted * w_ref[SSD_CONV - 1 - s:SSD_CONV - s, :]
    return _silu(acc)


def _ssd_kernel(xs_ref, bm_ref, cm_ref, z_ref, dtc_ref, dtr_ref,
                wx_ref, wb_ref, wc_ref, bx_ref, bb_ref, bc_ref,
                dbc_ref, dbr_ref, alc_ref, alr_ref, dsk_ref, ng_ref,
                o_ref, st_ref, px_ref, pb_ref, pc_ref):
    n = xs_ref.shape[0]
    hpg, hd = SSD_HEADS_PER_GROUP, SSD_HEAD_DIM

    @pl.when(pl.program_id(2) == 0)
    def _():
        st_ref[...] = jnp.zeros_like(st_ref)
        px_ref[...] = jnp.zeros_like(px_ref)
        pb_ref[...] = jnp.zeros_like(pb_ref)
        pc_ref[...] = jnp.zeros_like(pc_ref)

    xs_raw, bm_raw, cm_raw = xs_ref[...], bm_ref[...], cm_ref[...]
    xs = _causal_conv(xs_raw, px_ref[...], wx_ref, bx_ref)
    bm = _causal_conv(bm_raw, pb_ref[...], wb_ref, bb_ref)
    cm = _causal_conv(cm_raw, pc_ref[...], wc_ref, bc_ref)
    px_ref[...] = xs_raw
    pb_ref[...] = bm_raw
    pc_ref[...] = cm_raw

    dt_c = _softplus(dtc_ref[0] + dbc_ref[0])
    dt_r = _softplus(dtr_ref[0] + dbr_ref[0])
    adt_c = dt_c * (-jnp.exp(alc_ref[0]))
    adt_r = dt_r * (-jnp.exp(alr_ref[0]))
    row = lax.broadcasted_iota(jnp.int32, (n, n), 0)
    col = lax.broadcasted_iota(jnp.int32, (n, n), 1)
    causal = col <= row
    tri = causal.astype(BF16)
    acum_c = _sel_dot(tri, adt_c)
    acum_r = _sel_dot_left(adt_r, (row <= col).astype(BF16))
    a_last = acum_c[n - 1:n, :]

    hrow = lax.broadcasted_iota(jnp.int32, (hpg, hpg * hd), 0)
    hcol = lax.broadcasted_iota(jnp.int32, (hpg, hpg * hd), 1)
    expand = ((hcol >> SSD_HEAD_SHIFT) == hrow).astype(BF16)
    dt_x = _sel_dot_left(dt_c, expand)
    eac_x = _sel_dot_left(jnp.exp(acum_c), expand)
    dec_x = _sel_dot_left(jnp.exp(a_last - acum_c), expand)
    sdec_x = _sel_dot_left(jnp.broadcast_to(jnp.exp(a_last), (8, hpg)), expand)[0:1, :]

    xdt = xs * dt_x
    cmb = cm.astype(BF16)
    bmb = bm.astype(BF16)
    cb = _dot_nt(cmb, bmb)
    lane = lax.broadcasted_iota(jnp.int32, (n, 2 * hd), 1)
    pair_out = []
    for pr in range(hpg // 2):
        xp = xdt[:, pr * 2 * hd:(pr + 1) * 2 * hd]
        acc = None
        for half in range(2):
            r = 2 * pr + half
            diff = acum_c[:, r:r + 1] - acum_r[r:r + 1, :]
            seg = jnp.where(causal, jnp.exp(jnp.minimum(diff, 0.0)), 0.0)
            rhs = jnp.where((lane >> SSD_HEAD_SHIFT) == half, xp, 0.0).astype(BF16)
            term = _dot((cb * seg).astype(BF16), rhs)
            acc = term if acc is None else acc + term
        pair_out.append(acc)
    y_diag = jnp.concatenate(pair_out, axis=1)

    st = st_ref[...]
    y_off = _dot(cmb, st.astype(BF16)) * eac_x
    st_ref[...] = st * sdec_x + _dot_tn(bmb, (xdt * dec_x).astype(BF16))

    y = (y_diag + y_off + xs * dsk_ref[0]) * _silu(z_ref[...])
    o_ref[...] = _rms_rows(y, ng_ref[...]).astype(o_ref.dtype)


def _ssd(p, dt_c, dt_r, conv_w, conv_b, dt_bias, a_log, d_skip, norm_g, bsz, seq, chunk):
    t = bsz * seq
    nc = seq // chunk
    g, hpg, gw, ns = SSD_GROUPS, SSD_HEADS_PER_GROUP, SSD_GROUP_WIDTH, SSD_D_STATE
    xsb = (P_SXBC) // gw
    bmb = (P_SXBC + SSD_D_INNER) // ns
    cmb = (P_SXBC + SSD_D_INNER + g * ns) // ns
    zb = P_SZ // gw
    cw_b = SSD_D_INNER // ns
    cw_c = (SSD_D_INNER + g * ns) // ns
    conv_b2 = conv_b.reshape(1, SSD_CONV_DIM)
    dbc = dt_bias.reshape(g, 1, hpg)
    dbr = dt_bias.reshape(g, hpg, 1)
    alc = a_log.reshape(g, 1, hpg)
    alr = a_log.reshape(g, hpg, 1)
    dsk = jnp.repeat(d_skip, SSD_HEAD_DIM).reshape(g, 1, gw)
    ng = norm_g.reshape(1, SSD_D_INNER)
    rowmap = lambda b, gi, c: (b * nc + c)
    return pl.pallas_call(
        _ssd_kernel,
        out_shape=jax.ShapeDtypeStruct((t, SSD_D_INNER), BF16),
        grid=(bsz, g, nc),
        in_specs=[pl.BlockSpec((chunk, gw), lambda b, gi, c: (b * nc + c, xsb + gi)),
                  pl.BlockSpec((chunk, ns), lambda b, gi, c: (b * nc + c, bmb + gi)),
                  pl.BlockSpec((chunk, ns), lambda b, gi, c: (b * nc + c, cmb + gi)),
                  pl.BlockSpec((chunk, gw), lambda b, gi, c: (b * nc + c, zb + gi)),
                  pl.BlockSpec((1, chunk, hpg), lambda b, gi, c: (gi, b * nc + c, 0)),
                  pl.BlockSpec((1, hpg, chunk), lambda b, gi, c: (gi, 0, b * nc + c)),
                  pl.BlockSpec((SSD_CONV, gw), lambda b, gi, c: (0, gi)),
                  pl.BlockSpec((SSD_CONV, ns), lambda b, gi, c: (0, cw_b + gi)),
                  pl.BlockSpec((SSD_CONV, ns), lambda b, gi, c: (0, cw_c + gi)),
                  pl.BlockSpec((1, gw), lambda b, gi, c: (0, gi)),
                  pl.BlockSpec((1, ns), lambda b, gi, c: (0, cw_b + gi)),
                  pl.BlockSpec((1, ns), lambda b, gi, c: (0, cw_c + gi)),
                  pl.BlockSpec((1, 1, hpg), lambda b, gi, c: (gi, 0, 0)),
                  pl.BlockSpec((1, hpg, 1), lambda b, gi, c: (gi, 0, 0)),
                  pl.BlockSpec((1, 1, hpg), lambda b, gi, c: (gi, 0, 0)),
                  pl.BlockSpec((1, hpg, 1), lambda b, gi, c: (gi, 0, 0)),
                  pl.BlockSpec((1, 1, gw), lambda b, gi, c: (gi, 0, 0)),
                  pl.BlockSpec((1, gw), lambda b, gi, c: (0, gi))],
        out_specs=pl.BlockSpec((chunk, gw), lambda b, gi, c: (b * nc + c, gi)),
        scratch_shapes=[pltpu.VMEM((ns, gw), F32),
                        pltpu.VMEM((chunk, gw), F32),
                        pltpu.VMEM((chunk, ns), F32),
                        pltpu.VMEM((chunk, ns), F32)],
        compiler_params=_params(("parallel", "parallel", "arbitrary")),
        name="ssd_mixer",
    )(p, p, p, p, dt_c, dt_r, conv_w, conv_w, conv_w, conv_b2, conv_b2, conv_b2,
      dbc, dbr, alc, alr, dsk, ng)


NSA_TQ = 128
NSA_KB = 256
N_CMP_PAD = 128


def _rope(x, cos, sin):
    lane = lax.broadcasted_iota(jnp.int32, x.shape, 1)
    half = ROPE_DIM // 2
    swapped = jnp.where(lane < half, pltpu.roll(x, LANES - half, axis=1), pltpu.roll(x, half, axis=1))
    return x * cos + swapped * sin


def _compress(src_ref, pe_ref, w1_ref, w2_ref, nblk):
    hd = NSA_HEAD_DIM
    a0 = jnp.zeros((nblk, CMP_HIDDEN), F32)
    a1 = jnp.zeros((nblk, CMP_HIDDEN), F32)
    for c in range(CMP_STRIDE):
        xc = src_ref[pl.ds(c, nblk, stride=CMP_STRIDE), :]
        a0 = a0 + _dot((xc + pe_ref[c:c + 1, :]).astype(BF16), w1_ref[c * hd:(c + 1) * hd, :])
        c1 = CMP_STRIDE + c
        a1 = a1 + _dot((xc + pe_ref[c1:c1 + 1, :]).astype(BF16), w1_ref[c1 * hd:(c1 + 1) * hd, :])
    hid = a0 + pltpu.roll(a1, nblk - 1, axis=0)
    return _dot(_silu(hid).astype(BF16), w2_ref[...])


def _nsa_kernel(q_ref, kc_ref, vc_ref, ks_ref, vs_ref, kw_ref, vw_ref, g_ref,
                cos_ref, sin_ref, pek_ref, pev_ref, w1k_ref, w2k_ref, w1v_ref, w2v_ref, ovt_ref,
                o_ref, kcmp_s, vcmp_s, ksel_s, vsel_s, kwin_s, vwin_s, krope_s, *, seq):
    tq, hd, nr = NSA_TQ, NSA_HEAD_DIM, NSA_Q_PER_KV
    i = pl.program_id(2)
    nblk = seq // CMP_STRIDE
    nsel = seq // SEL_BLOCK
    scale = hd ** -0.5

    @pl.when(i == 0)
    def _():
        cos, sin = cos_ref[...], sin_ref[...]
        krope_s[...] = _rope(kc_ref[...], cos, sin)
        ksel_s[...] = _rope(ks_ref[...], cos, sin).astype(BF16)
        kwin_s[...] = _rope(kw_ref[...], cos, sin).astype(BF16)
        vsel_s[...] = vs_ref[...].astype(BF16)
        vwin_s[...] = vw_ref[...].astype(BF16)
        kcmp_s[...] = _compress(krope_s, pek_ref, w1k_ref, w2k_ref, nblk).astype(BF16)
        vcmp_s[...] = _compress(vc_ref, pev_ref, w1v_ref, w2v_ref, nblk).astype(BF16)

    t0 = pl.multiple_of(i * tq, tq)
    cos = cos_ref[pl.ds(t0, tq), :]
    sin = sin_ref[pl.ds(t0, tq), :]
    qs = [_rope(q_ref[:, r * hd:(r + 1) * hd], cos, sin) for r in range(nr)]
    qb = jnp.concatenate(qs, axis=0).astype(BF16)

    tpos3 = t0 + lax.broadcasted_iota(jnp.int32, (1, tq, 1), 1)

    s = (_dot_nt(qb, kcmp_s[...]) * scale).reshape(nr, tq, nblk)
    ncol = lax.broadcasted_iota(jnp.int32, (1, 1, nblk), 2)
    ok = (ncol * CMP_STRIDE + (CMP_LEN - 1)) <= tpos3
    sm = jnp.where(ok, s, NEG_INF)
    e = jnp.exp(sm - jnp.max(sm, axis=-1, keepdims=True))
    p_cmp = jnp.where(ok, e / jnp.sum(e, axis=-1, keepdims=True), 0.0)
    o_cmp = _dot(p_cmp.reshape(nr * tq, nblk).astype(BF16), vcmp_s[...])

    p_sum = p_cmp[0]
    for r in range(1, nr):
        p_sum = p_sum + p_cmp[r]
    ph = p_sum.astype(BF16)
    pm = (p_sum - ph.astype(F32)).astype(BF16)
    ovt = ovt_ref[...]
    p_slc = _dot_nt(ovt, ph) + _dot_nt(ovt, pm)
    jrow = lax.broadcasted_iota(jnp.int32, (nsel, tq), 0)
    blk_t = (t0 + lax.broadcasted_iota(jnp.int32, (nsel, tq), 1)) >> SEL_SHIFT
    forced = (jrow == 0) | (jrow == blk_t) | (jrow == blk_t - 1)
    score = jnp.where(forced, FORCE_SCORE, jnp.where(jrow <= blk_t, p_slc, NEG_INF))
    rank = jnp.zeros((nsel, tq), F32)
    for j in range(nsel):
        sj = score[j:j + 1, :]
        beats = jnp.where(sj > score, 1.0, jnp.where((sj == score) & (jrow > j), 1.0, 0.0))
        rank = rank + beats
    sel_t = jnp.where(rank < float(min(SEL_TOPK, nsel)), 1.0, 0.0)
    sel = jnp.transpose(sel_t).astype(BF16)

    kb = NSA_KB
    n_kb = (t0 + tq + kb - 1) // kb

    def sel_step(c, carry):
        m, l, acc = carry
        k0 = pl.multiple_of(c * kb, kb)
        kt = ksel_s[pl.ds(k0, kb), :]
        vt = vsel_s[pl.ds(k0, kb), :]
        st = (_dot_nt(qb, kt) * scale).reshape(nr, tq, kb)
        erow = lax.broadcasted_iota(jnp.int32, (nsel, kb), 0)
        ecol = lax.broadcasted_iota(jnp.int32, (nsel, kb), 1)
        expand = (((k0 + ecol) >> SEL_SHIFT) == erow).astype(BF16)
        chosen = _dot(sel, expand)
        kpos = k0 + lax.broadcasted_iota(jnp.int32, (1, 1, kb), 2)
        okk = ((chosen > 0.5)[None] & (kpos <= tpos3))
        stm = jnp.where(okk, st, NEG_INF)
        m_new = jnp.maximum(m, jnp.max(stm, axis=-1, keepdims=True))
        alpha = jnp.exp(m - m_new)
        pt = jnp.where(okk, jnp.exp(stm - m_new), 0.0)
        l_new = alpha * l + jnp.sum(pt, axis=-1, keepdims=True)
        pv = _dot(pt.reshape(nr * tq, kb).astype(BF16), vt).reshape(nr, tq, hd)
        return m_new, l_new, alpha * acc + pv

    m0 = jnp.full((nr, tq, 1), NEG_INF, F32)
    l0 = jnp.zeros((nr, tq, 1), F32)
    a0 = jnp.zeros((nr, tq, hd), F32)
    _, l_sel, acc_sel = lax.fori_loop(0, n_kb, sel_step, (m0, l0, a0))
    o_sel = acc_sel / l_sel

    wlen = min(WINDOW + tq, seq)
    w0 = pl.multiple_of(jnp.maximum(t0 + tq - wlen, 0), tq)
    kt = kwin_s[pl.ds(w0, wlen), :]
    vt = vwin_s[pl.ds(w0, wlen), :]
    sw = (_dot_nt(qb, kt) * scale).reshape(nr, tq, wlen)
    kpos = w0 + lax.broadcasted_iota(jnp.int32, (1, 1, wlen), 2)
    okw = (kpos <= tpos3) & (kpos > tpos3 - WINDOW)
    swm = jnp.where(okw, sw, NEG_INF)
    ew = jnp.where(okw, jnp.exp(swm - jnp.max(swm, axis=-1, keepdims=True)), 0.0)
    pw = ew / jnp.sum(ew, axis=-1, keepdims=True)
    o_win = _dot(pw.reshape(nr * tq, wlen).astype(BF16), vt)

    gs = _sigmoid(g_ref[0])
    o_cmp3 = o_cmp.reshape(nr, tq, hd)
    o_win3 = o_win.reshape(nr, tq, hd)
    for r in range(nr):
        o = (gs[:, 3 * r:3 * r + 1] * o_cmp3[r] + gs[:, 3 * r + 1:3 * r + 2] * o_sel[r]
             + gs[:, 3 * r + 2:3 * r + 3] * o_win3[r])
        o_ref[:, r * hd:(r + 1) * hd] = o.astype(o_ref.dtype)


def _nsa_tables(seq):
    half = ROPE_DIM // 2
    inv_freq = ROPE_THETA ** (-jnp.arange(half, dtype=F32) / half)
    ang = jnp.arange(seq).astype(F32)[:, None] * inv_freq[None, :]
    cos, sin = jnp.cos(ang), jnp.sin(ang)
    rest = NSA_HEAD_DIM - ROPE_DIM
    cos_t = jnp.concatenate([cos, cos, jnp.ones((seq, rest), F32)], axis=1)
    sin_t = jnp.concatenate([-sin, sin, jnp.zeros((seq, rest), F32)], axis=1)
    n_sel = seq // SEL_BLOCK
    cmp_starts = np.arange(seq // CMP_STRIDE) * CMP_STRIDE
    sel_starts = np.arange(n_sel) * SEL_BLOCK
    overlap = np.clip(np.minimum(cmp_starts[:, None] + CMP_LEN, sel_starts[None, :] + SEL_BLOCK)
                      - np.maximum(cmp_starts[:, None], sel_starts[None, :]), 0, None).astype(np.float32) / CMP_LEN
    return cos_t, sin_t, jnp.asarray(overlap.T, BF16)


def _nsa(p, gates, pos_k, pos_v, w1k, w2k, w1v, w2v, bsz, seq):
    t = bsz * seq
    tq, hd, g, nr = NSA_TQ, NSA_HEAD_DIM, NSA_KV_GROUPS, NSA_Q_PER_KV
    nq = seq // tq
    nblk = seq // CMP_STRIDE
    nsel = seq // SEL_BLOCK
    cos_t, sin_t, ovt = _nsa_tables(seq)
    qblk = P_NQ // (nr * hd)
    kvb = P_NKV // hd

    def kv_spec(split):
        return pl.BlockSpec((seq, hd), lambda b, gi, i: (b, kvb + split * g + gi))

    const2 = lambda b, gi, i: (0, 0)
    return pl.pallas_call(
        functools.partial(_nsa_kernel, seq=seq),
        out_shape=jax.ShapeDtypeStruct((t, NSA_WIDTH), BF16),
        grid=(bsz, g, nq),
        in_specs=[pl.BlockSpec((tq, nr * hd), lambda b, gi, i: (b * nq + i, qblk + gi))]
                 + [kv_spec(sp) for sp in range(6)]
                 + [pl.BlockSpec((1, tq, nr * 3), lambda b, gi, i: (gi, b * nq + i, 0)),
                    pl.BlockSpec((seq, hd), const2),
                    pl.BlockSpec((seq, hd), const2),
                    pl.BlockSpec((CMP_LEN, hd), const2),
                    pl.BlockSpec((CMP_LEN, hd), const2),
                    pl.BlockSpec((CMP_LEN * hd, CMP_HIDDEN), const2),
                    pl.BlockSpec((CMP_HIDDEN, hd), const2),
                    pl.BlockSpec((CMP_LEN * hd, CMP_HIDDEN), const2),
                    pl.BlockSpec((CMP_HIDDEN, hd), const2),
                    pl.BlockSpec((nsel, nblk), const2)],
        out_specs=pl.BlockSpec((tq, nr * hd), lambda b, gi, i: (b * nq + i, gi)),
        scratch_shapes=[pltpu.VMEM((nblk, hd), BF16), pltpu.VMEM((nblk, hd), BF16),
                        pltpu.VMEM((seq, hd), BF16), pltpu.VMEM((seq, hd), BF16),
                        pltpu.VMEM((seq, hd), BF16), pltpu.VMEM((seq, hd), BF16),
                        pltpu.VMEM((seq, hd), F32)],
        compiler_params=_params(("parallel", "parallel", "arbitrary")),
        name="nsa_mixer",
    )(p, p, p, p, p, p, p, gates, cos_t, sin_t, pos_k, pos_v, w1k, w2k, w1v, w2v, ovt)


def _relayout_w_in(w):
    def seg(off, size):
        return w[:, off:off + size]

    parts = [seg(_R_GATE, 3 * D_MODEL), seg(_R_GQ, GLA_DK), seg(_R_GK, GLA_DK), seg(_R_GV, GLA_DV),
             seg(_R_GR, GLA_DV), seg(_R_NQ, NSA_WIDTH), seg(_R_NKV, 6 * NSA_KV_WIDTH),
             seg(_R_SZ, SSD_D_INNER), seg(_R_SXBC, SSD_CONV_DIM),
             seg(_R_GLOW, GLA_LOWRANK), seg(_R_NG, 3 * NSA_HEADS), seg(_R_SDT, SSD_HEADS)]
    out = jnp.concatenate(parts, axis=1).astype(BF16)
    return jnp.pad(out, ((0, 0), (0, P_COLS - P_USED)))


def _layer(x2d, bsz, seq, norm_mix, w_in, gla_gate_w2, gla_gate_b, gla_out_norm,
           nsa_cmp_pos_k, nsa_cmp_pos_v, nsa_cmp_k_w1, nsa_cmp_k_w2, nsa_cmp_v_w1, nsa_cmp_v_w2,
           ssd_conv_w, ssd_conv_b, ssd_dt_bias, ssd_a_log, ssd_d, ssd_out_norm,
           w_branch, w_out, norm_ffn, w_ffn_gate, w_ffn_up, w_ffn_down, norm_final, final_norm):
    t = bsz * seq
    p = _in_proj(x2d, norm_mix, _relayout_w_in(w_in), min(1024, t), IN_PROJ_TILE_N)

    small = p[:, P_SMALL:P_SMALL + LANES]
    nsa_g = small[:, SMALL_NG:SMALL_NG + 3 * NSA_HEADS]
    nsa_g = nsa_g.reshape(t, NSA_KV_GROUPS, NSA_Q_PER_KV * 3).transpose(1, 0, 2)
    dt = small[:, SMALL_DT:SMALL_DT + SSD_HEADS].reshape(t, SSD_GROUPS, SSD_HEADS_PER_GROUP)
    dt_c = dt.transpose(1, 0, 2)
    dt_r = dt.transpose(1, 2, 0)

    w2pad = jnp.zeros((LANES, GLA_DK), BF16).at[SMALL_GLOW:SMALL_GLOW + GLA_LOWRANK].set(
        gla_gate_w2.astype(BF16))
    o_gla = _gla(p, w2pad, gla_gate_b, gla_out_norm, bsz, seq, min(256, seq))
    o_nsa = _nsa(p, nsa_g, nsa_cmp_pos_k, nsa_cmp_pos_v,
                 nsa_cmp_k_w1.astype(BF16), nsa_cmp_k_w2.astype(BF16),
                 nsa_cmp_v_w1.astype(BF16), nsa_cmp_v_w2.astype(BF16), bsz, seq)
    o_ssd = _ssd(p, dt_c, dt_r, ssd_conv_w, ssd_conv_b, ssd_dt_bias, ssd_a_log, ssd_d,
                 ssd_out_norm, bsz, seq, min(256, seq))

    x2d = _merge(o_gla, o_nsa, o_ssd, p, w_branch.astype(BF16), w_out.astype(BF16), x2d,
                 min(512, t), 256)
    return _ffn(x2d, norm_ffn, norm_final, w_ffn_gate.astype(BF16), w_ffn_up.astype(BF16),
                w_ffn_down.astype(BF16), min(512, t), 512, final_norm)


def kernel(x, norm_mix, w_in, gla_gate_w2, gla_gate_b, gla_out_norm, nsa_cmp_pos_k, nsa_cmp_pos_v,
           nsa_cmp_k_w1, nsa_cmp_k_w2, nsa_cmp_v_w1, nsa_cmp_v_w2, ssd_conv_w, ssd_conv_b,
           ssd_dt_bias, ssd_a_log, ssd_d, ssd_out_norm, w_branch, w_out, norm_ffn, w_ffn_gate,
           w_ffn_up, w_ffn_down, norm_final):
    bsz, seq, d = x.shape
    depth = norm_mix.shape[0]
    x2d = x.reshape(bsz * seq, d)
    for l in range(depth):
        x2d = _layer(x2d, bsz, seq, norm_mix[l], w_in[l], gla_gate_w2[l], gla_gate_b[l],
                     gla_out_norm[l], nsa_cmp_pos_k[l], nsa_cmp_pos_v[l], nsa_cmp_k_w1[l],
                     nsa_cmp_k_w2[l], nsa_cmp_v_w1[l], nsa_cmp_v_w2[l], ssd_conv_w[l],
                     ssd_conv_b[l], ssd_dt_bias[l], ssd_a_log[l], ssd_d[l], ssd_out_norm[l],
                     w_branch[l], w_out[l], norm_ffn[l], w_ffn_gate[l], w_ffn_up[l],
                     w_ffn_down[l], norm_final, l == depth - 1)
    return x2d.reshape(bsz, seq, d)
```

```python
import functools

import numpy as np
import jax
import jax.numpy as jnp
from jax import lax
from jax.experimental import pallas as pl
from jax.experimental.pallas import tpu as pltpu

F32 = jnp.float32
BF16 = jnp.bfloat16

D_MODEL = 2048
RMS_EPS = 1e-6
NEG_INF = -1e30
FORCE_SCORE = 1e9

GLA_HEADS = 4
GLA_DK = D_MODEL // 2
GLA_DV = D_MODEL
GLA_HK = GLA_DK // GLA_HEADS
GLA_HV = GLA_DV // GLA_HEADS
GLA_LOWRANK = 16
GLA_GATE_NORMALIZER = 16.0

NSA_HEAD_DIM = 128
NSA_HEADS = D_MODEL // NSA_HEAD_DIM
NSA_KV_GROUPS = 4
NSA_Q_PER_KV = NSA_HEADS // NSA_KV_GROUPS
NSA_WIDTH = NSA_HEADS * NSA_HEAD_DIM
NSA_KV_WIDTH = NSA_KV_GROUPS * NSA_HEAD_DIM
CMP_LEN = 32
CMP_STRIDE = 16
CMP_HIDDEN = 2 * NSA_HEAD_DIM
SEL_BLOCK = 64
SEL_SHIFT = 6
SEL_TOPK = 16
WINDOW = 512
ROPE_THETA = 500000.0
ROPE_DIM = NSA_HEAD_DIM // 4

SSD_D_INNER = 2 * D_MODEL
SSD_HEAD_DIM = 64
SSD_HEAD_SHIFT = 6
SSD_HEADS = SSD_D_INNER // SSD_HEAD_DIM
SSD_GROUPS = 8
SSD_HEADS_PER_GROUP = SSD_HEADS // SSD_GROUPS
SSD_D_STATE = 128
SSD_CONV = 4
SSD_GROUP_WIDTH = SSD_HEADS_PER_GROUP * SSD_HEAD_DIM
SSD_CONV_DIM = SSD_D_INNER + 2 * SSD_GROUPS * SSD_D_STATE

D_FF = ((8 * D_MODEL + 3 * 256 - 1) // (3 * 256)) * 256

LANES = 128
V7X_VMEM_LIMIT_BYTES = 56 * 1024 * 1024

_REF_SIZES = (3 * D_MODEL, GLA_DK, GLA_DK, GLA_DV, GLA_LOWRANK, GLA_DV,
              NSA_WIDTH, 6 * NSA_KV_WIDTH, 3 * NSA_HEADS,
              SSD_D_INNER, SSD_CONV_DIM, SSD_HEADS)
_REF_OFF = tuple(int(v) for v in np.concatenate([[0], np.cumsum(_REF_SIZES)]))
(_R_GATE, _R_GQ, _R_GK, _R_GV, _R_GLOW, _R_GR, _R_NQ, _R_NKV, _R_NG,
 _R_SZ, _R_SXBC, _R_SDT) = _REF_OFF[:-1]
PA_COLS = _R_GLOW
PB_START, PB_COLS = _R_GR, _R_NG - _R_GR
PC_START, PC_COLS = _R_SZ, _R_SDT - _R_SZ
A_GATE, A_GQ, A_GK, A_GV = _R_GATE, _R_GQ, _R_GK, _R_GV
B_GR, B_NQ, B_NKV = 0, _R_NQ - _R_GR, _R_NKV - _R_GR
C_SZ, C_SXBC = 0, _R_SXBC - _R_SZ
S_BLOCKS = (_R_GLOW // LANES, _R_NG // LANES, _R_SDT // LANES)
S_GLOW = _R_GLOW % LANES
S_NG = LANES + _R_NG % LANES
S_DT = 2 * LANES + _R_SDT % LANES
PROJ_TILE_M = 1024
PROJ_TILE_N = 1024
PROJ_CAST_ROWS = 256


def _params(sem, vmem=V7X_VMEM_LIMIT_BYTES):
    return pltpu.CompilerParams(dimension_semantics=sem, vmem_limit_bytes=vmem)


def _sigmoid(x):
    return 1.0 / (1.0 + jnp.exp(-x))


def _silu(x):
    return x * _sigmoid(x)


def _softplus(x):
    return jnp.maximum(x, 0.0) + jnp.log(1.0 + jnp.exp(-jnp.abs(x)))


def _split3(x):
    hi = x.astype(BF16)
    r1 = x - hi.astype(F32)
    mid = r1.astype(BF16)
    lo = (r1 - mid.astype(F32)).astype(BF16)
    return hi, mid, lo


def _dot(a, b):
    return jnp.dot(a, b, preferred_element_type=F32)


def _dot_nt(a, b):
    return lax.dot_general(a, b, (((1,), (1,)), ((), ())), preferred_element_type=F32)


def _dot_tn(a, b):
    return lax.dot_general(a, b, (((0,), (0,)), ((), ())), preferred_element_type=F32)


def _sel_dot(sel, x):
    hi, mid, lo = _split3(x)
    return _dot(sel, hi) + _dot(sel, mid) + _dot(sel, lo)


def _sel_dot_left(x, sel):
    hi, mid, lo = _split3(x)
    return _dot(hi, sel) + _dot(mid, sel) + _dot(lo, sel)


def _rms_rows(x, gain):
    ms = jnp.mean(x * x, axis=-1, keepdims=True)
    return x * lax.rsqrt(ms + RMS_EPS) * gain


def _rms_cast_kernel(x_ref, g_ref, o_ref):
    o_ref[...] = _rms_rows(x_ref[...], g_ref[...]).astype(o_ref.dtype)


def _rms_cast(x2d, gain, tm):
    t, d = x2d.shape
    return pl.pallas_call(
        _rms_cast_kernel,
        out_shape=jax.ShapeDtypeStruct((t, d), BF16),
        grid=(t // tm,),
        in_specs=[pl.BlockSpec((tm, d), lambda i: (i, 0)),
                  pl.BlockSpec((1, d), lambda i: (0, 0))],
        out_specs=pl.BlockSpec((tm, d), lambda i: (i, 0)),
        compiler_params=_params(("parallel",)),
        name="rms_cast",
    )(x2d, gain.reshape(1, d))


def _proj_kernel(*refs, shift):
    if shift:
        h_ref, wa_ref, wb_ref, o_ref, w_s = refs
    else:
        h_ref, wa_ref, o_ref, w_s = refs
    tn = w_s.shape[1]

    @pl.when(pl.program_id(1) == 0)
    def _():
        for r0 in range(0, w_s.shape[0], PROJ_CAST_ROWS):
            rows = pl.ds(r0, PROJ_CAST_ROWS)
            w = wa_ref[rows, :]
            if shift:
                wcat = jnp.concatenate([w, wb_ref[rows, :]], axis=1)
                w = pltpu.roll(wcat, tn + LANES - shift, axis=1)[:, :tn]
            w_s[rows, :] = w.astype(BF16)

    o_ref[...] = _dot(h_ref[...], w_s[...])


def _proj(h, w_stack, layer, start, ncols, tm, tn):
    t, d = h.shape
    shift = start % LANES
    base = (start - shift) // tn
    assert (start - shift) % tn == 0 and ncols % tn == 0
    in_specs = [pl.BlockSpec((tm, d), lambda j, i: (i, 0)),
                pl.BlockSpec((None, d, tn), lambda j, i: (layer, 0, base + j))]
    args = [h, w_stack]
    if shift:
        nxt = tn // LANES
        in_specs.append(pl.BlockSpec((None, d, LANES), lambda j, i: (layer, 0, (base + j + 1) * nxt)))
        args.append(w_stack)
    return pl.pallas_call(
        functools.partial(_proj_kernel, shift=shift),
        out_shape=jax.ShapeDtypeStruct((t, ncols), F32),
        grid=(ncols // tn, t // tm),
        in_specs=in_specs,
        out_specs=pl.BlockSpec((tm, tn), lambda j, i: (i, j)),
        scratch_shapes=[pltpu.VMEM((d, tn), BF16)],
        compiler_params=_params(("parallel", "arbitrary")),
        name="in_proj",
    )(*args)


def _proj_small(h, w_stack, layer, tm):
    t, d = h.shape
    b0, b1, b2 = S_BLOCKS

    def wmap(j, i):
        return (layer, 0, jnp.where(j == 0, b0, jnp.where(j == 1, b1, b2)))

    return pl.pallas_call(
        functools.partial(_proj_kernel, shift=0),
        out_shape=jax.ShapeDtypeStruct((t, len(S_BLOCKS) * LANES), F32),
        grid=(len(S_BLOCKS), t // tm),
        in_specs=[pl.BlockSpec((tm, d), lambda j, i: (i, 0)),
                  pl.BlockSpec((None, d, LANES), wmap)],
        out_specs=pl.BlockSpec((tm, LANES), lambda j, i: (i, j)),
        scratch_shapes=[pltpu.VMEM((d, LANES), BF16)],
        compiler_params=_params(("parallel", "arbitrary")),
        name="in_proj_small",
    )(h, w_stack)


def _merge_kernel(og_ref, on_ref, os_ref, g0_ref, g1_ref, g2_ref,
                  wb0_ref, wb1_ref, wb2_ref, wo_ref, x_ref, o_ref):
    @pl.when(pl.program_id(1) == 0)
    def _():
        o_ref[...] = x_ref[...]

    m = (_sigmoid(g0_ref[...]) * _dot(og_ref[...], wb0_ref[...])
         + _sigmoid(g1_ref[...]) * _dot(on_ref[...], wb1_ref[...])
         + _sigmoid(g2_ref[...]) * _dot(os_ref[...], wb2_ref[...]))
    o_ref[...] += _dot(m.astype(BF16), wo_ref[...])


def _merge(o_gla, o_nsa, o_ssd, p, wb, wo, x2d, tm, tn):
    t, d = x2d.shape
    nj = d // tn
    gate_blk0 = A_GATE // tn
    return pl.pallas_call(
        _merge_kernel,
        out_shape=jax.ShapeDtypeStruct((t, d), F32),
        grid=(t // tm, nj),
        in_specs=[pl.BlockSpec((tm, GLA_DV), lambda i, j: (i, 0)),
                  pl.BlockSpec((tm, NSA_WIDTH), lambda i, j: (i, 0)),
                  pl.BlockSpec((tm, SSD_D_INNER), lambda i, j: (i, 0)),
                  pl.BlockSpec((tm, tn), lambda i, j: (i, gate_blk0 + j)),
                  pl.BlockSpec((tm, tn), lambda i, j: (i, gate_blk0 + nj + j)),
                  pl.BlockSpec((tm, tn), lambda i, j: (i, gate_blk0 + 2 * nj + j)),
                  pl.BlockSpec((GLA_DV, tn), lambda i, j: (0, j)),
                  pl.BlockSpec((NSA_WIDTH, tn), lambda i, j: (1, j)),
                  pl.BlockSpec((SSD_D_INNER, tn), lambda i, j: (1, j)),
                  pl.BlockSpec((tn, d), lambda i, j: (j, 0)),
                  pl.BlockSpec((tm, d), lambda i, j: (i, 0))],
        out_specs=pl.BlockSpec((tm, d), lambda i, j: (i, 0)),
        compiler_params=_params(("parallel", "arbitrary")),
        name="branch_merge",
    )(o_gla, o_nsa, o_ssd, p, p, p, wb, wb, wb, wo, x2d)


def _ffn_kernel(x_ref, g_ref, gf_ref, wg_ref, wu_ref, wd_ref, o_ref, h_ref, *, final_norm):
    j = pl.program_id(1)

    @pl.when(j == 0)
    def _():
        x = x_ref[...]
        h_ref[...] = _rms_rows(x, g_ref[...]).astype(BF16)
        o_ref[...] = x

    h = h_ref[...]
    a = _silu(_dot(h, wg_ref[...])) * _dot(h, wu_ref[...])
    o_ref[...] += _dot(a.astype(BF16), wd_ref[...])

    if final_norm:
        @pl.when(j == pl.num_programs(1) - 1)
        def _():
            o_ref[...] = _rms_rows(o_ref[...], gf_ref[...])


def _ffn(x2d, gain, gain_final, wg, wu, wd, tm, tf, final_norm):
    t, d = x2d.shape
    f = wg.shape[1]
    return pl.pallas_call(
        functools.partial(_ffn_kernel, final_norm=final_norm),
        out_shape=jax.ShapeDtypeStruct((t, d), F32),
        grid=(t // tm, f // tf),
        in_specs=[pl.BlockSpec((tm, d), lambda i, j: (i, 0)),
                  pl.BlockSpec((1, d), lambda i, j: (0, 0)),
                  pl.BlockSpec((1, d), lambda i, j: (0, 0)),
                  pl.BlockSpec((d, tf), lambda i, j: (0, j)),
                  pl.BlockSpec((d, tf), lambda i, j: (0, j)),
                  pl.BlockSpec((tf, d), lambda i, j: (j, 0))],
        out_specs=pl.BlockSpec((tm, d), lambda i, j: (i, 0)),
        scratch_shapes=[pltpu.VMEM((tm, d), BF16)],
        compiler_params=_params(("parallel", "arbitrary")),
        name="ffn",
    )(x2d, gain.reshape(1, d), gain_final.reshape(1, d), wg, wu, wd)


GLA_CHUNK = 64
GLA_SUB = 16
GLA_EXP_CAP = 88.0


def _gla_kernel(q_ref, k_ref, v_ref, r_ref, sm_ref, w2_ref, gb_ref, ng_ref, o_ref, st_ref,
                *, chunks_per_tile):
    @pl.when(pl.program_id(2) == 0)
    def _():
        st_ref[...] = jnp.zeros_like(st_ref)

    c_rows = GLA_CHUNK
    row = lax.broadcasted_iota(jnp.int32, (c_rows, c_rows), 0)
    col = lax.broadcasted_iota(jnp.int32, (c_rows, c_rows), 1)
    tri = (col <= row).astype(BF16)
    srow = lax.broadcasted_iota(jnp.int32, (GLA_SUB, c_rows), 0)
    scol = lax.broadcasted_iota(jnp.int32, (GLA_SUB, c_rows), 1)
    scale = GLA_HK ** -0.5

    for c in range(chunks_per_tile):
        rows = pl.ds(c * c_rows, c_rows)
        ga = _dot(sm_ref[rows, :].astype(BF16), w2_ref[...]) + gb_ref[...]
        log_a = -_softplus(-ga) * (1.0 / GLA_GATE_NORMALIZER)
        bc = _sel_dot(tri, log_a)
        q = q_ref[rows, :] * scale
        k = k_ref[rows, :]
        vb = v_ref[rows, :].astype(BF16)
        st = st_ref[...]

        inter = _dot_nt((q * jnp.exp(bc)).astype(BF16), st.astype(BF16))
        parts = []
        for i in range(c_rows // GLA_SUB):
            lo = i * GLA_SUB
            ref_pt = bc[lo - 1:lo, :] if i > 0 else jnp.zeros((1, GLA_HK), F32)
            kt = (k * jnp.exp(jnp.minimum(ref_pt - bc, GLA_EXP_CAP))).astype(BF16)
            qi = (q[lo:lo + GLA_SUB, :] * jnp.exp(bc[lo:lo + GLA_SUB, :] - ref_pt)).astype(BF16)
            sc = _dot_nt(qi, kt)
            parts.append(jnp.where(scol <= srow + lo, sc, 0.0))
        att = jnp.concatenate(parts, axis=0)
        out = inter + _dot(att.astype(BF16), vb)

        b_last = bc[c_rows - 1:c_rows, :]
        khat = (k * jnp.exp(b_last - bc)).astype(BF16)
        st_ref[...] = st * jnp.exp(b_last) + _dot_tn(vb, khat)

        y = _rms_rows(out, ng_ref[...]) * _silu(r_ref[rows, :])
        o_ref[rows, :] = y.astype(o_ref.dtype)


def _gla(pa, pb, ps, w2pad, gate_b, norm_g, bsz, seq, tile):
    t = bsz * seq
    nt = seq // tile
    qb, kb = A_GQ // GLA_HK, A_GK // GLA_HK
    vb, rb = A_GV // GLA_HV, B_GR // GLA_HV
    smb = 0
    return pl.pallas_call(
        functools.partial(_gla_kernel, chunks_per_tile=tile // GLA_CHUNK),
        out_shape=jax.ShapeDtypeStruct((t, GLA_DV), BF16),
        grid=(bsz, GLA_HEADS, nt),
        in_specs=[pl.BlockSpec((tile, GLA_HK), lambda b, h, c: (b * nt + c, qb + h)),
                  pl.BlockSpec((tile, GLA_HK), lambda b, h, c: (b * nt + c, kb + h)),
                  pl.BlockSpec((tile, GLA_HV), lambda b, h, c: (b * nt + c, vb + h)),
                  pl.BlockSpec((tile, GLA_HV), lambda b, h, c: (b * nt + c, rb + h)),
                  pl.BlockSpec((tile, LANES), lambda b, h, c: (b * nt + c, smb)),
                  pl.BlockSpec((LANES, GLA_HK), lambda b, h, c: (0, h)),
                  pl.BlockSpec((1, GLA_HK), lambda b, h, c: (0, h)),
                  pl.BlockSpec((1, GLA_HV), lambda b, h, c: (0, 0))],
        out_specs=pl.BlockSpec((tile, GLA_HV), lambda b, h, c: (b * nt + c, h)),
        scratch_shapes=[pltpu.VMEM((GLA_HV, GLA_HK), F32)],
        compiler_params=_params(("parallel", "parallel", "arbitrary")),
        name="gla_mixer",
    )(pa, pa, pa, pb, ps, w2pad, gate_b.reshape(1, GLA_DK), norm_g.reshape(1, GLA_HV))


def _causal_conv(cur, prev, w_ref, b_ref):
    row = lax.broadcasted_iota(jnp.int32, cur.shape, 0)
    acc = cur * w_ref[SSD_CONV - 1:SSD_CONV, :] + b_ref[...]
    for s in range(1, SSD_CONV):
        shifted = jnp.where(row < s, pltpu.roll(prev, s, axis=0), pltpu.roll(cur, s, axis=0))
        acc = acc + shifted * w_ref[SSD_CONV - 1 - s:SSD_CONV - s, :]
    return _silu(acc)


def _ssd_kernel(xs_ref, bm_ref, cm_ref, z_ref, dtc_ref, dtr_ref,
                wx_ref, wb_ref, wc_ref, bx_ref, bb_ref, bc_ref,
                dbc_ref, dbr_ref, alc_ref, alr_ref, dsk_ref, ng_ref,
                o_ref, st_ref, px_ref, pb_ref, pc_ref):
    n = xs_ref.shape[0]
    hpg, hd = SSD_HEADS_PER_GROUP, SSD_HEAD_DIM

    @pl.when(pl.program_id(2) == 0)
    def _():
        st_ref[...] = jnp.zeros_like(st_ref)
        px_ref[...] = jnp.zeros_like(px_ref)
        pb_ref[...] = jnp.zeros_like(pb_ref)
        pc_ref[...] = jnp.zeros_like(pc_ref)

    xs_raw, bm_raw, cm_raw = xs_ref[...], bm_ref[...], cm_ref[...]
    xs = _causal_conv(xs_raw, px_ref[...], wx_ref, bx_ref)
    bm = _causal_conv(bm_raw, pb_ref[...], wb_ref, bb_ref)
    cm = _causal_conv(cm_raw, pc_ref[...], wc_ref, bc_ref)
    px_ref[...] = xs_raw
    pb_ref[...] = bm_raw
    pc_ref[...] = cm_raw

    dt_c = _softplus(dtc_ref[0] + dbc_ref[0])
    dt_r = _softplus(dtr_ref[0] + dbr_ref[0])
    adt_c = dt_c * (-jnp.exp(alc_ref[0]))
    adt_r = dt_r * (-jnp.exp(alr_ref[0]))
    row = lax.broadcasted_iota(jnp.int32, (n, n), 0)
    col = lax.broadcasted_iota(jnp.int32, (n, n), 1)
    causal = col <= row
    tri = causal.astype(BF16)
    acum_c = _sel_dot(tri, adt_c)
    acum_r = _sel_dot_left(adt_r, (row <= col).astype(BF16))
    a_last = acum_c[n - 1:n, :]

    hrow = lax.broadcasted_iota(jnp.int32, (hpg, hpg * hd), 0)
    hcol = lax.broadcasted_iota(jnp.int32, (hpg, hpg * hd), 1)
    expand = ((hcol >> SSD_HEAD_SHIFT) == hrow).astype(BF16)
    dt_x = _sel_dot_left(dt_c, expand)
    eac_x = _sel_dot_left(jnp.exp(acum_c), expand)
    dec_x = _sel_dot_left(jnp.exp(a_last - acum_c), expand)
    sdec_x = _sel_dot_left(jnp.broadcast_to(jnp.exp(a_last), (8, hpg)), expand)[0:1, :]

    xdt = xs * dt_x
    cmb = cm.astype(BF16)
    bmb = bm.astype(BF16)
    cb = _dot_nt(cmb, bmb)
    lane = lax.broadcasted_iota(jnp.int32, (n, 2 * hd), 1)
    pair_out = []
    for pr in range(hpg // 2):
        xp = xdt[:, pr * 2 * hd:(pr + 1) * 2 * hd]
        acc = None
        for half in range(2):
            r = 2 * pr + half
            diff = acum_c[:, r:r + 1] - acum_r[r:r + 1, :]
            seg = jnp.where(causal, jnp.exp(jnp.minimum(diff, 0.0)), 0.0)
            rhs = jnp.where((lane >> SSD_HEAD_SHIFT) == half, xp, 0.0).astype(BF16)
            term = _dot((cb * seg).astype(BF16), rhs)
            acc = term if acc is None else acc + term
        pair_out.append(acc)
    y_diag = jnp.concatenate(pair_out, axis=1)

    st = st_ref[...]
    y_off = _dot(cmb, st.astype(BF16)) * eac_x
    st_ref[...] = st * sdec_x + _dot_tn(bmb, (xdt * dec_x).astype(BF16))

    y = (y_diag + y_off + xs * dsk_ref[0]) * _silu(z_ref[...])
    o_ref[...] = _rms_rows(y, ng_ref[...]).astype(o_ref.dtype)


def _ssd(pc, dt_c, dt_r, conv_w, conv_b, dt_bias, a_log, d_skip, norm_g, bsz, seq, chunk):
    t = bsz * seq
    nc = seq // chunk
    g, hpg, gw, ns = SSD_GROUPS, SSD_HEADS_PER_GROUP, SSD_GROUP_WIDTH, SSD_D_STATE
    xsb = C_SXBC // gw
    bmb = (C_SXBC + SSD_D_INNER) // ns
    cmb = (C_SXBC + SSD_D_INNER + g * ns) // ns
    zb = C_SZ // gw
    cw_b = SSD_D_INNER // ns
    cw_c = (SSD_D_INNER + g * ns) // ns
    conv_b2 = conv_b.reshape(1, SSD_CONV_DIM)
    dbc = dt_bias.reshape(g, 1, hpg)
    dbr = dt_bias.reshape(g, hpg, 1)
    alc = a_log.reshape(g, 1, hpg)
    alr = a_log.reshape(g, hpg, 1)
    dsk = jnp.repeat(d_skip, SSD_HEAD_DIM).reshape(g, 1, gw)
    ng = norm_g.reshape(1, SSD_D_INNER)
    return pl.pallas_call(
        _ssd_kernel,
        out_shape=jax.ShapeDtypeStruct((t, SSD_D_INNER), BF16),
        grid=(bsz, g, nc),
        in_specs=[pl.BlockSpec((chunk, gw), lambda b, gi, c: (b * nc + c, xsb + gi)),
                  pl.BlockSpec((chunk, ns), lambda b, gi, c: (b * nc + c, bmb + gi)),
                  pl.BlockSpec((chunk, ns), lambda b, gi, c: (b * nc + c, cmb + gi)),
                  pl.BlockSpec((chunk, gw), lambda b, gi, c: (b * nc + c, zb + gi)),
                  pl.BlockSpec((1, chunk, hpg), lambda b, gi, c: (gi, b * nc + c, 0)),
                  pl.BlockSpec((1, hpg, chunk), lambda b, gi, c: (gi, 0, b * nc + c)),
                  pl.BlockSpec((SSD_CONV, gw), lambda b, gi, c: (0, gi)),
                  pl.BlockSpec((SSD_CONV, ns), lambda b, gi, c: (0, cw_b + gi)),
                  pl.BlockSpec((SSD_CONV, ns), lambda b, gi, c: (0, cw_c + gi)),
                  pl.BlockSpec((1, gw), lambda b, gi, c: (0, gi)),
                  pl.BlockSpec((1, ns), lambda b, gi, c: (0, cw_b + gi)),
                  pl.BlockSpec((1, ns), lambda b, gi, c: (0, cw_c + gi)),
                  pl.BlockSpec((1, 1, hpg), lambda b, gi, c: (gi, 0, 0)),
                  pl.BlockSpec((1, hpg, 1), lambda b, gi, c: (gi, 0, 0)),
                  pl.BlockSpec((1, 1, hpg), lambda b, gi, c: (gi, 0, 0)),
                  pl.BlockSpec((1, hpg, 1), lambda b, gi, c: (gi, 0, 0)),
                  pl.BlockSpec((1, 1, gw), lambda b, gi, c: (gi, 0, 0)),
                  pl.BlockSpec((1, gw), lambda b, gi, c: (0, gi))],
        out_specs=pl.BlockSpec((chunk, gw), lambda b, gi, c: (b * nc + c, gi)),
        scratch_shapes=[pltpu.VMEM((ns, gw), F32),
                        pltpu.VMEM((chunk, gw), F32),
                        pltpu.VMEM((chunk, ns), F32),
                        pltpu.VMEM((chunk, ns), F32)],
        compiler_params=_params(("parallel", "parallel", "arbitrary")),
        name="ssd_mixer",
    )(pc, pc, pc, pc, dt_c, dt_r, conv_w, conv_w, conv_w, conv_b2, conv_b2, conv_b2,
      dbc, dbr, alc, alr, dsk, ng)


LOG2_E = 1.4426950408889634
NSA_TQ = 128
NSA_KB = 256
N_CMP_PAD = 128


def _rope(x, cos, sin):
    lane = lax.broadcasted_iota(jnp.int32, x.shape, 1)
    half = ROPE_DIM // 2
    swapped = jnp.where(lane < half, pltpu.roll(x, LANES - half, axis=1), pltpu.roll(x, half, axis=1))
    return x * cos + swapped * sin


def _compress(src_ref, pe_ref, w1_ref, w2_ref, nblk):
    hd = NSA_HEAD_DIM
    a0 = jnp.zeros((nblk, CMP_HIDDEN), F32)
    a1 = jnp.zeros((nblk, CMP_HIDDEN), F32)
    for c in range(CMP_STRIDE):
        xc = src_ref[pl.ds(c, nblk, stride=CMP_STRIDE), :]
        a0 = a0 + _dot((xc + pe_ref[c:c + 1, :]).astype(BF16), w1_ref[c * hd:(c + 1) * hd, :])
        c1 = CMP_STRIDE + c
        a1 = a1 + _dot((xc + pe_ref[c1:c1 + 1, :]).astype(BF16), w1_ref[c1 * hd:(c1 + 1) * hd, :])
    hid = a0 + pltpu.roll(a1, nblk - 1, axis=0)
    return _dot(_silu(hid).astype(BF16), w2_ref[...])


def _nsa_kernel(q_ref, kc_ref, vc_ref, ks_ref, vs_ref, kw_ref, vw_ref, g_ref,
                cos_ref, sin_ref, pek_ref, pev_ref, w1k_ref, w2k_ref, w1v_ref, w2v_ref, ovt_ref,
                o_ref, kcmp_s, vcmp_s, ksel_s, vsel_s, kwin_s, vwin_s, krope_s, *, seq):
    tq, hd, nr = NSA_TQ, NSA_HEAD_DIM, NSA_Q_PER_KV
    i = pl.program_id(2)
    nblk = seq // CMP_STRIDE
    nsel = seq // SEL_BLOCK
    scale = hd ** -0.5

    @pl.when(i == 0)
    def _():
        cos, sin = cos_ref[...], sin_ref[...]
        krope_s[...] = _rope(kc_ref[...], cos, sin)
        ksel_s[...] = _rope(ks_ref[...], cos, sin).astype(BF16)
        kwin_s[...] = _rope(kw_ref[...], cos, sin).astype(BF16)
        vsel_s[...] = vs_ref[...].astype(BF16)
        vwin_s[...] = vw_ref[...].astype(BF16)
        kcmp_s[...] = _compress(krope_s, pek_ref, w1k_ref, w2k_ref, nblk).astype(BF16)
        vcmp_s[...] = _compress(vc_ref, pev_ref, w1v_ref, w2v_ref, nblk).astype(BF16)

    t0 = pl.multiple_of(i * tq, tq)
    cos = cos_ref[pl.ds(t0, tq), :]
    sin = sin_ref[pl.ds(t0, tq), :]
    qs = [_rope(q_ref[:, r * hd:(r + 1) * hd], cos, sin) for r in range(nr)]
    qb = jnp.concatenate(qs, axis=0).astype(BF16)
    c2 = scale * LOG2_E
    tpos = t0 + lax.broadcasted_iota(jnp.int32, (tq, 1), 0)

    ncol = lax.broadcasted_iota(jnp.int32, (1, nblk), 1)
    ok = (ncol * CMP_STRIDE + (CMP_LEN - 1)) <= tpos
    bias_c = jnp.where(ok, 0.0, NEG_INF)
    s = _dot_nt(qb, kcmp_s[...]).reshape(nr, tq, nblk) + bias_c[None]
    e = jnp.exp2((s - jnp.max(s, axis=-1, keepdims=True)) * c2)
    p_cmp = jnp.where(ok[None], e * (1.0 / jnp.sum(e, axis=-1, keepdims=True)), 0.0)
    o_cmp = _dot(p_cmp.reshape(nr * tq, nblk).astype(BF16), vcmp_s[...]).reshape(nr, tq, hd)

    p_sum = p_cmp[0]
    for r in range(1, nr):
        p_sum = p_sum + p_cmp[r]
    ph = p_sum.astype(BF16)
    pm = (p_sum - ph.astype(F32)).astype(BF16)
    ovt = ovt_ref[...]
    p_slc = _dot_nt(ovt, ph) + _dot_nt(ovt, pm)
    jrow = lax.broadcasted_iota(jnp.int32, (nsel, tq), 0)
    blk_t = (t0 + lax.broadcasted_iota(jnp.int32, (nsel, tq), 1)) >> SEL_SHIFT
    forced = (jrow == 0) | (jrow == blk_t) | (jrow == blk_t - 1)
    score = jnp.where(forced, FORCE_SCORE, jnp.where(jrow <= blk_t, p_slc, NEG_INF))
    rank = jnp.zeros((nsel, tq), F32)
    for j in range(nsel):
        sj = score[j:j + 1, :]
        beats = jnp.where(sj > score, 1.0, jnp.where((sj == score) & (jrow > j), 1.0, 0.0))
        rank = rank + beats
    sel_t = jnp.where(rank < float(min(SEL_TOPK, nsel)), 1.0, 0.0)
    sel = jnp.transpose(sel_t).astype(BF16)

    kb = NSA_KB
    n_kb = (t0 + tq + kb - 1) // kb

    def sel_step(c, carry):
        k0 = pl.multiple_of(c * kb, kb)
        kt = ksel_s[pl.ds(k0, kb), :]
        vt = vsel_s[pl.ds(k0, kb), :]
        erow = lax.broadcasted_iota(jnp.int32, (nsel, kb), 0)
        ecol = lax.broadcasted_iota(jnp.int32, (nsel, kb), 1)
        expand = (((k0 + ecol) >> SEL_SHIFT) == erow).astype(BF16)
        chosen = _dot(sel, expand)
        kpos = k0 + lax.broadcasted_iota(jnp.int32, (1, kb), 1)
        bias = jnp.where((chosen > 0.5) & (kpos <= tpos), 0.0, NEG_INF)
        m, l, acc = carry
        sb = _dot_nt(qb, kt).reshape(nr, tq, kb) + bias[None]
        m_new = jnp.maximum(m, jnp.max(sb, axis=-1, keepdims=True))
        alpha = jnp.exp2((m - m_new) * c2)
        pt = jnp.exp2((sb - m_new) * c2)
        pv = _dot(pt.reshape(nr * tq, kb).astype(BF16), vt).reshape(nr, tq, hd)
        return m_new, alpha * l + jnp.sum(pt, axis=-1, keepdims=True), alpha * acc + pv

    init = (jnp.full((nr, tq, 1), NEG_INF, F32), jnp.zeros((nr, tq, 1), F32),
            jnp.zeros((nr, tq, hd), F32))
    _, l_sel, acc_sel = lax.fori_loop(0, n_kb, sel_step, init)
    o_sel = acc_sel * (1.0 / l_sel)

    wlen = min(WINDOW + tq, seq)
    w0 = pl.multiple_of(jnp.maximum(t0 + tq - wlen, 0), tq)
    ktw = kwin_s[pl.ds(w0, wlen), :]
    vtw = vwin_s[pl.ds(w0, wlen), :]
    kposw = w0 + lax.broadcasted_iota(jnp.int32, (1, wlen), 1)
    bias_w = jnp.where((kposw <= tpos) & (kposw > tpos - WINDOW), 0.0, NEG_INF)
    sw = _dot_nt(qb, ktw).reshape(nr, tq, wlen) + bias_w[None]
    ew = jnp.exp2((sw - jnp.max(sw, axis=-1, keepdims=True)) * c2)
    o_win = (_dot(ew.reshape(nr * tq, wlen).astype(BF16), vtw).reshape(nr, tq, hd)
             * (1.0 / jnp.sum(ew, axis=-1, keepdims=True)))

    gs = _sigmoid(g_ref[0])
    for r in range(nr):
        o = (gs[:, 3 * r:3 * r + 1] * o_cmp[r] + gs[:, 3 * r + 1:3 * r + 2] * o_sel[r]
             + gs[:, 3 * r + 2:3 * r + 3] * o_win[r])
        o_ref[:, r * hd:(r + 1) * hd] = o.astype(o_ref.dtype)


def _nsa_tables(seq):
    half = ROPE_DIM // 2
    inv_freq = ROPE_THETA ** (-jnp.arange(half, dtype=F32) / half)
    ang = jnp.arange(seq).astype(F32)[:, None] * inv_freq[None, :]
    cos, sin = jnp.cos(ang), jnp.sin(ang)
    rest = NSA_HEAD_DIM - ROPE_DIM
    cos_t = jnp.concatenate([cos, cos, jnp.ones((seq, rest), F32)], axis=1)
    sin_t = jnp.concatenate([-sin, sin, jnp.zeros((seq, rest), F32)], axis=1)
    n_sel = seq // SEL_BLOCK
    cmp_starts = np.arange(seq // CMP_STRIDE) * CMP_STRIDE
    sel_starts = np.arange(n_sel) * SEL_BLOCK
    overlap = np.clip(np.minimum(cmp_starts[:, None] + CMP_LEN, sel_starts[None, :] + SEL_BLOCK)
                      - np.maximum(cmp_starts[:, None], sel_starts[None, :]), 0, None).astype(np.float32) / CMP_LEN
    return cos_t, sin_t, jnp.asarray(overlap.T, BF16)


def _nsa(pb, gates, pos_k, pos_v, w1k, w2k, w1v, w2v, bsz, seq):
    t = bsz * seq
    tq, hd, g, nr = NSA_TQ, NSA_HEAD_DIM, NSA_KV_GROUPS, NSA_Q_PER_KV
    nq = seq // tq
    nblk = seq // CMP_STRIDE
    nsel = seq // SEL_BLOCK
    cos_t, sin_t, ovt = _nsa_tables(seq)
    qblk = B_NQ // (nr * hd)
    kvb = B_NKV // hd

    def kv_spec(split):
        return pl.BlockSpec((seq, hd), lambda b, gi, i: (b, kvb + split * g + gi))

    const2 = lambda b, gi, i: (0, 0)
    return pl.pallas_call(
        functools.partial(_nsa_kernel, seq=seq),
        out_shape=jax.ShapeDtypeStruct((t, NSA_WIDTH), BF16),
        grid=(bsz, g, nq),
        in_specs=[pl.BlockSpec((tq, nr * hd), lambda b, gi, i: (b * nq + i, qblk + gi))]
                 + [kv_spec(sp) for sp in range(6)]
                 + [pl.BlockSpec((1, tq, nr * 3), lambda b, gi, i: (gi, b * nq + i, 0)),
                    pl.BlockSpec((seq, hd), const2),
                    pl.BlockSpec((seq, hd), const2),
                    pl.BlockSpec((CMP_LEN, hd), const2),
                    pl.BlockSpec((CMP_LEN, hd), const2),
                    pl.BlockSpec((CMP_LEN * hd, CMP_HIDDEN), const2),
                    pl.BlockSpec((CMP_HIDDEN, hd), const2),
                    pl.BlockSpec((CMP_LEN * hd, CMP_HIDDEN), const2),
                    pl.BlockSpec((CMP_HIDDEN, hd), const2),
                    pl.BlockSpec((nsel, nblk), const2)],
        out_specs=pl.BlockSpec((tq, nr * hd), lambda b, gi, i: (b * nq + i, gi)),
        scratch_shapes=[pltpu.VMEM((nblk, hd), BF16), pltpu.VMEM((nblk, hd), BF16),
                        pltpu.VMEM((seq, hd), BF16), pltpu.VMEM((seq, hd), BF16),
                        pltpu.VMEM((seq, hd), BF16), pltpu.VMEM((seq, hd), BF16),
                        pltpu.VMEM((seq, hd), F32)],
        compiler_params=_params(("parallel", "parallel", "arbitrary")),
        name="nsa_mixer",
    )(pb, pb, pb, pb, pb, pb, pb, gates, cos_t, sin_t, pos_k, pos_v, w1k, w2k, w1v, w2v, ovt)


def _in_proj(x2d, gain, w_in_stack, layer):
    t = x2d.shape[0]
    tm = min(PROJ_TILE_M, t)
    h = _rms_cast(x2d, gain, tm)
    pa = _proj(h, w_in_stack, layer, 0, PA_COLS, tm, PROJ_TILE_N)
    pb = _proj(h, w_in_stack, layer, PB_START, PB_COLS, tm, PROJ_TILE_N)
    pc = _proj(h, w_in_stack, layer, PC_START, PC_COLS, tm, PROJ_TILE_N)
    ps = _proj_small(h, w_in_stack, layer, tm)
    return pa, pb, pc, ps


def _small_views(ps):
    t = ps.shape[0]
    nsa_g = ps[:, S_NG:S_NG + 3 * NSA_HEADS]
    nsa_g = nsa_g.reshape(t, NSA_KV_GROUPS, NSA_Q_PER_KV * 3).transpose(1, 0, 2)
    dt = ps[:, S_DT:S_DT + SSD_HEADS].reshape(t, SSD_GROUPS, SSD_HEADS_PER_GROUP)
    return nsa_g, dt.transpose(1, 0, 2), dt.transpose(1, 2, 0)


def _layer(x2d, bsz, seq, layer, norm_mix, w_in, gla_gate_w2, gla_gate_b, gla_out_norm,
           nsa_cmp_pos_k, nsa_cmp_pos_v, nsa_cmp_k_w1, nsa_cmp_k_w2, nsa_cmp_v_w1, nsa_cmp_v_w2,
           ssd_conv_w, ssd_conv_b, ssd_dt_bias, ssd_a_log, ssd_d, ssd_out_norm,
           w_branch, w_out, norm_ffn, w_ffn_gate, w_ffn_up, w_ffn_down, norm_final, final_norm):
    t = bsz * seq
    pa, pb, pc, ps = _in_proj(x2d, norm_mix, w_in, layer)
    nsa_g, dt_c, dt_r = _small_views(ps)

    w2pad = jnp.zeros((LANES, GLA_DK), BF16).at[S_GLOW:S_GLOW + GLA_LOWRANK].set(
        gla_gate_w2.astype(BF16))
    o_gla = _gla(pa, pb, ps, w2pad, gla_gate_b, gla_out_norm, bsz, seq, min(256, seq))
    o_nsa = _nsa(pb, nsa_g, nsa_cmp_pos_k, nsa_cmp_pos_v,
                 nsa_cmp_k_w1.astype(BF16), nsa_cmp_k_w2.astype(BF16),
                 nsa_cmp_v_w1.astype(BF16), nsa_cmp_v_w2.astype(BF16), bsz, seq)
    o_ssd = _ssd(pc, dt_c, dt_r, ssd_conv_w, ssd_conv_b, ssd_dt_bias, ssd_a_log, ssd_d,
                 ssd_out_norm, bsz, seq, min(256, seq))

    x2d = _merge(o_gla, o_nsa, o_ssd, pa, w_branch.astype(BF16), w_out.astype(BF16), x2d,
                 min(512, t), 256)
    return _ffn(x2d, norm_ffn, norm_final, w_ffn_gate.astype(BF16), w_ffn_up.astype(BF16),
                w_ffn_down.astype(BF16), min(512, t), 512, final_norm)


def kernel(x, norm_mix, w_in, gla_gate_w2, gla_gate_b, gla_out_norm, nsa_cmp_pos_k, nsa_cmp_pos_v,
           nsa_cmp_k_w1, nsa_cmp_k_w2, nsa_cmp_v_w1, nsa_cmp_v_w2, ssd_conv_w, ssd_conv_b,
           ssd_dt_bias, ssd_a_log, ssd_d, ssd_out_norm, w_branch, w_out, norm_ffn, w_ffn_gate,
           w_ffn_up, w_ffn_down, norm_final):
    bsz, seq, d = x.shape
    depth = norm_mix.shape[0]
    x2d = x.reshape(bsz * seq, d)
    for l in range(depth):
        x2d = _layer(x2d, bsz, seq, l, norm_mix[l], w_in, gla_gate_w2[l], gla_gate_b[l],
                     gla_out_norm[l], nsa_cmp_pos_k[l], nsa_cmp_pos_v[l], nsa_cmp_k_w1[l],
                     nsa_cmp_k_w2[l], nsa_cmp_v_w1[l], nsa_cmp_v_w2[l], ssd_conv_w[l],
                     ssd_conv_b[l], ssd_dt_bias[l], ssd_a_log[l], ssd_d[l], ssd_out_norm[l],
                     w_branch[l], w_out[l], norm_ffn[l], w_ffn_gate[l], w_ffn_up[l],
                     w_ffn_down[l], norm_final, l == depth - 1)
    return x2d.reshape(bsz, seq, d)
```

```python
import functools

import numpy as np
import jax
import jax.numpy as jnp
from jax import lax
from jax.experimental import pallas as pl
from jax.experimental.pallas import tpu as pltpu

F32 = jnp.float32
BF16 = jnp.bfloat16

D_MODEL = 2048
RMS_EPS = 1e-6
NEG_INF = -1e30
FORCE_SCORE = 1e9

GLA_HEADS = 4
GLA_DK = D_MODEL // 2
GLA_DV = D_MODEL
GLA_HK = GLA_DK // GLA_HEADS
GLA_HV = GLA_DV // GLA_HEADS
GLA_LOWRANK = 16
GLA_GATE_NORMALIZER = 16.0

NSA_HEAD_DIM = 128
NSA_HEAD_SHIFT = 7
NSA_HEADS = D_MODEL // NSA_HEAD_DIM
NSA_KV_GROUPS = 4
NSA_Q_PER_KV = NSA_HEADS // NSA_KV_GROUPS
NSA_WIDTH = NSA_HEADS * NSA_HEAD_DIM
NSA_KV_WIDTH = NSA_KV_GROUPS * NSA_HEAD_DIM
CMP_LEN = 32
CMP_STRIDE = 16
CMP_HIDDEN = 2 * NSA_HEAD_DIM
SEL_BLOCK = 64
SEL_SHIFT = 6
SEL_TOPK = 16
WINDOW = 512
ROPE_THETA = 500000.0
ROPE_DIM = NSA_HEAD_DIM // 4

SSD_D_INNER = 2 * D_MODEL
SSD_HEAD_DIM = 64
SSD_HEAD_SHIFT = 6
SSD_HEADS = SSD_D_INNER // SSD_HEAD_DIM
SSD_GROUPS = 8
SSD_HEADS_PER_GROUP = SSD_HEADS // SSD_GROUPS
SSD_D_STATE = 128
SSD_CONV = 4
SSD_GROUP_WIDTH = SSD_HEADS_PER_GROUP * SSD_HEAD_DIM
SSD_CONV_DIM = SSD_D_INNER + 2 * SSD_GROUPS * SSD_D_STATE

D_FF = ((8 * D_MODEL + 3 * 256 - 1) // (3 * 256)) * 256

LANES = 128
V7X_VMEM_LIMIT_BYTES = 56 * 1024 * 1024

_REF_SIZES = (3 * D_MODEL, GLA_DK, GLA_DK, GLA_DV, GLA_LOWRANK, GLA_DV,
              NSA_WIDTH, 6 * NSA_KV_WIDTH, 3 * NSA_HEADS,
              SSD_D_INNER, SSD_CONV_DIM, SSD_HEADS)
_REF_OFF = tuple(int(v) for v in np.concatenate([[0], np.cumsum(_REF_SIZES)]))
(_R_GATE, _R_GQ, _R_GK, _R_GV, _R_GLOW, _R_GR, _R_NQ, _R_NKV, _R_NG,
 _R_SZ, _R_SXBC, _R_SDT) = _REF_OFF[:-1]
PA_COLS = _R_GLOW
PB_START, PB_COLS = _R_GR, _R_NG - _R_GR
PC_START, PC_COLS = _R_SZ, _R_SDT - _R_SZ
A_GATE, A_GQ, A_GK, A_GV = _R_GATE, _R_GQ, _R_GK, _R_GV
B_GR, B_NQ, B_NKV = 0, _R_NQ - _R_GR, _R_NKV - _R_GR
C_SZ, C_SXBC = 0, _R_SXBC - _R_SZ
S_BLOCKS = (_R_GLOW // LANES, _R_NG // LANES, _R_SDT // LANES)
S_GLOW = _R_GLOW % LANES
S_NG = LANES + _R_NG % LANES
S_DT = 2 * LANES + _R_SDT % LANES
PROJ_TILE_M = 1024
PROJ_TILE_N = 1024
PROJ_CAST_ROWS = 256


def _params(sem, vmem=V7X_VMEM_LIMIT_BYTES):
    return pltpu.CompilerParams(dimension_semantics=sem, vmem_limit_bytes=vmem)


def _sigmoid(x):
    return 1.0 / (1.0 + jnp.exp(-x))


def _silu(x):
    return x * _sigmoid(x)


def _softplus(x):
    return jnp.maximum(x, 0.0) + jnp.log(1.0 + jnp.exp(-jnp.abs(x)))


def _split3(x):
    hi = x.astype(BF16)
    r1 = x - hi.astype(F32)
    mid = r1.astype(BF16)
    lo = (r1 - mid.astype(F32)).astype(BF16)
    return hi, mid, lo


def _dot(a, b):
    return jnp.dot(a, b, preferred_element_type=F32)


def _dot_nt(a, b):
    return lax.dot_general(a, b, (((1,), (1,)), ((), ())), preferred_element_type=F32)


def _dot_tn(a, b):
    return lax.dot_general(a, b, (((0,), (0,)), ((), ())), preferred_element_type=F32)


def _sel_dot(sel, x):
    hi, mid, lo = _split3(x)
    return _dot(sel, hi) + _dot(sel, mid) + _dot(sel, lo)


def _sel_dot_left(x, sel):
    hi, mid, lo = _split3(x)
    return _dot(hi, sel) + _dot(mid, sel) + _dot(lo, sel)


def _rms_rows(x, gain):
    ms = jnp.mean(x * x, axis=-1, keepdims=True)
    return x * lax.rsqrt(ms + RMS_EPS) * gain


def _rms_cast_kernel(x_ref, g_ref, o_ref):
    o_ref[...] = _rms_rows(x_ref[...], g_ref[...]).astype(o_ref.dtype)


def _rms_cast(x2d, gain, tm):
    t, d = x2d.shape
    return pl.pallas_call(
        _rms_cast_kernel,
        out_shape=jax.ShapeDtypeStruct((t, d), BF16),
        grid=(t // tm,),
        in_specs=[pl.BlockSpec((tm, d), lambda i: (i, 0)),
                  pl.BlockSpec((1, d), lambda i: (0, 0))],
        out_specs=pl.BlockSpec((tm, d), lambda i: (i, 0)),
        compiler_params=_params(("parallel",)),
        name="rms_cast",
    )(x2d, gain.reshape(1, d))


def _proj_kernel(*refs, shift, residual):
    refs = list(refs)
    h_ref, wa_ref = refs[:2]
    wb_ref = refs[2] if shift else None
    res_ref = refs[-3] if residual else None
    o_ref, w_s = refs[-2:]
    tn = w_s.shape[1]

    @pl.when(pl.program_id(1) == 0)
    def _():
        for r0 in range(0, w_s.shape[0], PROJ_CAST_ROWS):
            rows = pl.ds(r0, PROJ_CAST_ROWS)
            w = wa_ref[rows, :]
            if shift:
                wcat = jnp.concatenate([w, wb_ref[rows, :]], axis=1)
                w = pltpu.roll(wcat, tn + LANES - shift, axis=1)[:, :tn]
            w_s[rows, :] = w.astype(BF16)

    y = _dot(h_ref[...], w_s[...])
    o_ref[...] = y + res_ref[...] if residual else y


def _proj(h, w_stack, layer, start, ncols, tm, tn, residual=None, name="in_proj"):
    t, d = h.shape
    shift = start % LANES
    base = (start - shift) // tn
    assert (start - shift) % tn == 0 and ncols % tn == 0
    in_specs = [pl.BlockSpec((tm, d), lambda j, i: (i, 0)),
                pl.BlockSpec((None, d, tn), lambda j, i: (layer, 0, base + j))]
    args = [h, w_stack]
    if shift:
        nxt = tn // LANES
        in_specs.append(pl.BlockSpec((None, d, LANES), lambda j, i: (layer, 0, (base + j + 1) * nxt)))
        args.append(w_stack)
    if residual is not None:
        in_specs.append(pl.BlockSpec((tm, tn), lambda j, i: (i, j)))
        args.append(residual)
    return pl.pallas_call(
        functools.partial(_proj_kernel, shift=shift, residual=residual is not None),
        out_shape=jax.ShapeDtypeStruct((t, ncols), F32),
        grid=(ncols // tn, t // tm),
        in_specs=in_specs,
        out_specs=pl.BlockSpec((tm, tn), lambda j, i: (i, j)),
        scratch_shapes=[pltpu.VMEM((d, tn), BF16)],
        compiler_params=_params(("parallel", "arbitrary")),
        name=name,
    )(*args)


def _proj_small(h, w_stack, layer, tm):
    t, d = h.shape
    b0, b1, b2 = S_BLOCKS

    def wmap(j, i):
        return (layer, 0, jnp.where(j == 0, b0, jnp.where(j == 1, b1, b2)))

    return pl.pallas_call(
        functools.partial(_proj_kernel, shift=0, residual=False),
        out_shape=jax.ShapeDtypeStruct((t, len(S_BLOCKS) * LANES), F32),
        grid=(len(S_BLOCKS), t // tm),
        in_specs=[pl.BlockSpec((tm, d), lambda j, i: (i, 0)),
                  pl.BlockSpec((None, d, LANES), wmap)],
        out_specs=pl.BlockSpec((tm, LANES), lambda j, i: (i, j)),
        scratch_shapes=[pltpu.VMEM((d, LANES), BF16)],
        compiler_params=_params(("parallel", "arbitrary")),
        name="in_proj_small",
    )(h, w_stack)


def _merge_kernel(og_ref, on_ref, os_ref, g0_ref, g1_ref, g2_ref, wb0_ref, wb1_ref, wb2_ref, o_ref):
    m = (_sigmoid(g0_ref[...]) * _dot(og_ref[...], wb0_ref[...])
         + _sigmoid(g1_ref[...]) * _dot(on_ref[...], wb1_ref[...])
         + _sigmoid(g2_ref[...]) * _dot(os_ref[...], wb2_ref[...]))
    o_ref[...] = m.astype(o_ref.dtype)


def _merge(o_gla, o_nsa, o_ssd, pa, wb, tm, tn):
    t = o_gla.shape[0]
    d = D_MODEL
    nj = d // tn
    gate_blk0 = A_GATE // tn
    return pl.pallas_call(
        _merge_kernel,
        out_shape=jax.ShapeDtypeStruct((t, d), BF16),
        grid=(t // tm, nj),
        in_specs=[pl.BlockSpec((tm, GLA_DV), lambda i, j: (i, 0)),
                  pl.BlockSpec((tm, NSA_WIDTH), lambda i, j: (i, 0)),
                  pl.BlockSpec((tm, SSD_D_INNER), lambda i, j: (i, 0)),
                  pl.BlockSpec((tm, tn), lambda i, j: (i, gate_blk0 + j)),
                  pl.BlockSpec((tm, tn), lambda i, j: (i, gate_blk0 + nj + j)),
                  pl.BlockSpec((tm, tn), lambda i, j: (i, gate_blk0 + 2 * nj + j)),
                  pl.BlockSpec((GLA_DV, tn), lambda i, j: (0, j)),
                  pl.BlockSpec((NSA_WIDTH, tn), lambda i, j: (1, j)),
                  pl.BlockSpec((SSD_D_INNER, tn), lambda i, j: (1, j))],
        out_specs=pl.BlockSpec((tm, tn), lambda i, j: (i, j)),
        compiler_params=_params(("parallel", "arbitrary")),
        name="branch_merge",
    )(o_gla, o_nsa, o_ssd, pa, pa, pa, wb, wb, wb)


def _ffn_kernel(x_ref, g_ref, gf_ref, wg_ref, wu_ref, wd_ref, o_ref, h_ref, *, final_norm):
    j = pl.program_id(1)

    @pl.when(j == 0)
    def _():
        x = x_ref[...]
        h_ref[...] = _rms_rows(x, g_ref[...]).astype(BF16)
        o_ref[...] = x

    h = h_ref[...]
    a = _silu(_dot(h, wg_ref[...])) * _dot(h, wu_ref[...])
    o_ref[...] += _dot(a.astype(BF16), wd_ref[...])

    if final_norm:
        @pl.when(j == pl.num_programs(1) - 1)
        def _():
            o_ref[...] = _rms_rows(o_ref[...], gf_ref[...])


def _ffn(x2d, gain, gain_final, wg, wu, wd, tm, tf, final_norm):
    t, d = x2d.shape
    f = wg.shape[1]
    return pl.pallas_call(
        functools.partial(_ffn_kernel, final_norm=final_norm),
        out_shape=jax.ShapeDtypeStruct((t, d), F32),
        grid=(t // tm, f // tf),
        in_specs=[pl.BlockSpec((tm, d), lambda i, j: (i, 0)),
                  pl.BlockSpec((1, d), lambda i, j: (0, 0)),
                  pl.BlockSpec((1, d), lambda i, j: (0, 0)),
                  pl.BlockSpec((d, tf), lambda i, j: (0, j)),
                  pl.BlockSpec((d, tf), lambda i, j: (0, j)),
                  pl.BlockSpec((tf, d), lambda i, j: (j, 0))],
        out_specs=pl.BlockSpec((tm, d), lambda i, j: (i, 0)),
        scratch_shapes=[pltpu.VMEM((tm, d), BF16)],
        compiler_params=_params(("parallel", "arbitrary")),
        name="ffn",
    )(x2d, gain.reshape(1, d), gain_final.reshape(1, d), wg, wu, wd)


GLA_CHUNK = 64
GLA_SUB = 16
GLA_EXP_CAP = 88.0
GLA_HEADS_PER_STEP = 2


def _gla_kernel(q_ref, k_ref, v_ref, r_ref, sm_ref, w2_ref, gb_ref, ng_ref, o_ref, st_ref,
                *, chunks_per_tile):
    @pl.when(pl.program_id(2) == 0)
    def _():
        st_ref[...] = jnp.zeros_like(st_ref)

    c_rows = GLA_CHUNK
    row = lax.broadcasted_iota(jnp.int32, (c_rows, c_rows), 0)
    col = lax.broadcasted_iota(jnp.int32, (c_rows, c_rows), 1)
    tri = (col <= row).astype(BF16)
    srow = lax.broadcasted_iota(jnp.int32, (GLA_SUB, c_rows), 0)
    scol = lax.broadcasted_iota(jnp.int32, (GLA_SUB, c_rows), 1)
    scale = GLA_HK ** -0.5

    smb = sm_ref[...].astype(BF16)
    work = []
    for h in range(GLA_HEADS_PER_STEP):
        kcols = slice(h * GLA_HK, (h + 1) * GLA_HK)
        vcols = slice(h * GLA_HV, (h + 1) * GLA_HV)
        ga = _dot(smb, w2_ref[:, kcols]) + gb_ref[:, kcols]
        log_a_all = -_softplus(-ga) * (1.0 / GLA_GATE_NORMALIZER)
        for c in range(chunks_per_tile):
            rows = slice(c * c_rows, (c + 1) * c_rows)
            bc = _sel_dot(tri, log_a_all[rows])
            q = q_ref[rows, kcols] * scale
            k = k_ref[rows, kcols]
            vb = v_ref[rows, vcols].astype(BF16)
            parts = []
            for i in range(c_rows // GLA_SUB):
                lo = i * GLA_SUB
                ref_pt = bc[lo - 1:lo, :] if i > 0 else jnp.zeros((1, GLA_HK), F32)
                kt = (k * jnp.exp(jnp.minimum(ref_pt - bc, GLA_EXP_CAP))).astype(BF16)
                qi = (q[lo:lo + GLA_SUB, :] * jnp.exp(bc[lo:lo + GLA_SUB, :] - ref_pt)).astype(BF16)
                sc = _dot_nt(qi, kt)
                parts.append(jnp.where(scol <= srow + lo, sc, 0.0))
            intra = _dot(jnp.concatenate(parts, axis=0).astype(BF16), vb)
            b_last = bc[c_rows - 1:c_rows, :]
            khat = (k * jnp.exp(b_last - bc)).astype(BF16)
            work.append((h, rows, vcols, (q * jnp.exp(bc)).astype(BF16), intra,
                         jnp.exp(b_last), _dot_tn(vb, khat)))

    states = [st_ref[h] for h in range(GLA_HEADS_PER_STEP)]
    for h, rows, vcols, qe, intra, decay, incr in work:
        out = _dot_nt(qe, states[h].astype(BF16)) + intra
        states[h] = states[h] * decay + incr
        y = _rms_rows(out, ng_ref[...]) * _silu(r_ref[rows, vcols])
        o_ref[rows, vcols] = y.astype(o_ref.dtype)
    for h in range(GLA_HEADS_PER_STEP):
        st_ref[h] = states[h]


def _gla(pa, pb, ps, w2pad, gate_b, norm_g, bsz, seq, tile):
    t = bsz * seq
    nt = seq // tile
    hps = GLA_HEADS_PER_STEP
    kw, vw = hps * GLA_HK, hps * GLA_HV
    qb, kb = A_GQ // kw, A_GK // kw
    vb, rb = A_GV // vw, B_GR // vw
    return pl.pallas_call(
        functools.partial(_gla_kernel, chunks_per_tile=tile // GLA_CHUNK),
        out_shape=jax.ShapeDtypeStruct((t, GLA_DV), BF16),
        grid=(bsz, GLA_HEADS // hps, nt),
        in_specs=[pl.BlockSpec((tile, kw), lambda b, h, c: (b * nt + c, qb + h)),
                  pl.BlockSpec((tile, kw), lambda b, h, c: (b * nt + c, kb + h)),
                  pl.BlockSpec((tile, vw), lambda b, h, c: (b * nt + c, vb + h)),
                  pl.BlockSpec((tile, vw), lambda b, h, c: (b * nt + c, rb + h)),
                  pl.BlockSpec((tile, LANES), lambda b, h, c: (b * nt + c, 0)),
                  pl.BlockSpec((LANES, kw), lambda b, h, c: (0, h)),
                  pl.BlockSpec((1, kw), lambda b, h, c: (0, h)),
                  pl.BlockSpec((1, GLA_HV), lambda b, h, c: (0, 0))],
        out_specs=pl.BlockSpec((tile, vw), lambda b, h, c: (b * nt + c, h)),
        scratch_shapes=[pltpu.VMEM((hps, GLA_HV, GLA_HK), F32)],
        compiler_params=_params(("parallel", "parallel", "arbitrary")),
        name="gla_mixer",
    )(pa, pa, pa, pb, ps, w2pad, gate_b.reshape(1, GLA_DK), norm_g.reshape(1, GLA_HV))


SSD_SUB = 128
SSD_HIST = 8


def _causal_conv(e_ref, src_ref, w_ref, b_ref):
    n = src_ref.shape[0]
    e_ref[pl.ds(SSD_HIST, n), :] = src_ref[...]
    acc = src_ref[...] * w_ref[SSD_CONV - 1:SSD_CONV, :] + b_ref[...]
    for s in range(1, SSD_CONV):
        acc = acc + e_ref[pl.ds(SSD_HIST - s, n), :] * w_ref[SSD_CONV - 1 - s:SSD_CONV - s, :]
    e_ref[pl.ds(0, SSD_HIST), :] = src_ref[pl.ds(n - SSD_HIST, SSD_HIST), :]
    return _silu(acc)


def _ssd_kernel(xs_ref, bm_ref, cm_ref, z_ref, dtc_ref, dtr_ref,
                wx_ref, wb_ref, wc_ref, bx_ref, bb_ref, bc_ref,
                dbc_ref, dbr_ref, alc_ref, alr_ref, dsk_ref, ng_ref,
                o_ref, st_ref, ex_ref, eb_ref, ec_ref):
    n = xs_ref.shape[0]
    sub = min(SSD_SUB, n)
    hpg, hd = SSD_HEADS_PER_GROUP, SSD_HEAD_DIM

    @pl.when(pl.program_id(2) == 0)
    def _():
        st_ref[...] = jnp.zeros_like(st_ref)
        ex_ref[pl.ds(0, SSD_HIST), :] = jnp.zeros((SSD_HIST, ex_ref.shape[1]), F32)
        eb_ref[pl.ds(0, SSD_HIST), :] = jnp.zeros((SSD_HIST, eb_ref.shape[1]), F32)
        ec_ref[pl.ds(0, SSD_HIST), :] = jnp.zeros((SSD_HIST, ec_ref.shape[1]), F32)

    xs = _causal_conv(ex_ref, xs_ref, wx_ref, bx_ref)
    bmb = _causal_conv(eb_ref, bm_ref, wb_ref, bb_ref).astype(BF16)
    cmb = _causal_conv(ec_ref, cm_ref, wc_ref, bc_ref).astype(BF16)

    hrow = lax.broadcasted_iota(jnp.int32, (hpg, hpg * hd), 0)
    hcol = lax.broadcasted_iota(jnp.int32, (hpg, hpg * hd), 1)
    expand = ((hcol >> SSD_HEAD_SHIFT) == hrow).astype(BF16)
    dt_c = _softplus(dtc_ref[0] + dbc_ref[0])
    a_c = -jnp.exp(alc_ref[0])
    a_r = -jnp.exp(alr_ref[0])
    xdt = xs * _sel_dot_left(dt_c, expand)

    row = lax.broadcasted_iota(jnp.int32, (sub, sub), 0)
    col = lax.broadcasted_iota(jnp.int32, (sub, sub), 1)
    causal = col <= row
    tri = causal.astype(BF16)
    triu = (row <= col).astype(BF16)
    lane = lax.broadcasted_iota(jnp.int32, (sub, 2 * hd), 1)

    for si in range(n // sub):
        rows = slice(si * sub, (si + 1) * sub)
        acum_c = _sel_dot(tri, dt_c[rows] * a_c)
        adt_r = _softplus(dtr_ref[0, si] + dbr_ref[0]) * a_r
        acum_r = _sel_dot_left(adt_r, triu)
        a_last = acum_c[sub - 1:sub, :]
        eac_x = _sel_dot_left(jnp.exp(acum_c), expand)
        dec_x = _sel_dot_left(jnp.exp(a_last - acum_c), expand)
        sdec_x = _sel_dot_left(jnp.broadcast_to(jnp.exp(a_last), (8, hpg)), expand)[0:1, :]

        xd, cs, bs = xdt[rows], cmb[rows], bmb[rows]
        cb = _dot_nt(cs, bs)
        pair_out = []
        for pr in range(hpg // 2):
            xp = xd[:, pr * 2 * hd:(pr + 1) * 2 * hd]
            acc = None
            for half in range(2):
                r = 2 * pr + half
                diff = acum_c[:, r:r + 1] - acum_r[r:r + 1, :]
                seg = jnp.where(causal, jnp.exp(jnp.minimum(diff, 0.0)), 0.0)
                rhs = jnp.where((lane >> SSD_HEAD_SHIFT) == half, xp, 0.0).astype(BF16)
                term = _dot((cb * seg).astype(BF16), rhs)
                acc = term if acc is None else acc + term
            pair_out.append(acc)
        y_diag = jnp.concatenate(pair_out, axis=1)

        st = st_ref[...]
        y_off = _dot(cs, st.astype(BF16)) * eac_x
        st_ref[...] = st * sdec_x + _dot_tn(bs, (xd * dec_x).astype(BF16))

        y = (y_diag + y_off + xs[rows] * dsk_ref[0]) * _silu(z_ref[rows, :])
        o_ref[rows, :] = _rms_rows(y, ng_ref[...]).astype(o_ref.dtype)


def _ssd(pc, dt_c, dt_r, conv_w, conv_b, dt_bias, a_log, d_skip, norm_g, bsz, seq, chunk):
    t = bsz * seq
    nc = seq // chunk
    sub = min(SSD_SUB, chunk)
    g, hpg, gw, ns = SSD_GROUPS, SSD_HEADS_PER_GROUP, SSD_GROUP_WIDTH, SSD_D_STATE
    xsb = C_SXBC // gw
    bmb = (C_SXBC + SSD_D_INNER) // ns
    cmb = (C_SXBC + SSD_D_INNER + g * ns) // ns
    zb = C_SZ // gw
    cw_b = SSD_D_INNER // ns
    cw_c = (SSD_D_INNER + g * ns) // ns
    conv_b2 = conv_b.reshape(1, SSD_CONV_DIM)
    dbc = dt_bias.reshape(g, 1, hpg)
    dbr = dt_bias.reshape(g, hpg, 1)
    alc = a_log.reshape(g, 1, hpg)
    alr = a_log.reshape(g, hpg, 1)
    dsk = jnp.repeat(d_skip, SSD_HEAD_DIM).reshape(g, 1, gw)
    ng = norm_g.reshape(1, SSD_D_INNER)
    return pl.pallas_call(
        _ssd_kernel,
        out_shape=jax.ShapeDtypeStruct((t, SSD_D_INNER), BF16),
        grid=(bsz, g, nc),
        in_specs=[pl.BlockSpec((chunk, gw), lambda b, gi, c: (b * nc + c, xsb + gi)),
                  pl.BlockSpec((chunk, ns), lambda b, gi, c: (b * nc + c, bmb + gi)),
                  pl.BlockSpec((chunk, ns), lambda b, gi, c: (b * nc + c, cmb + gi)),
                  pl.BlockSpec((chunk, gw), lambda b, gi, c: (b * nc + c, zb + gi)),
                  pl.BlockSpec((1, chunk, hpg), lambda b, gi, c: (gi, b * nc + c, 0)),
                  pl.BlockSpec((1, chunk // sub, hpg, sub), lambda b, gi, c: (gi, b * nc + c, 0, 0)),
                  pl.BlockSpec((SSD_CONV, gw), lambda b, gi, c: (0, gi)),
                  pl.BlockSpec((SSD_CONV, ns), lambda b, gi, c: (0, cw_b + gi)),
                  pl.BlockSpec((SSD_CONV, ns), lambda b, gi, c: (0, cw_c + gi)),
                  pl.BlockSpec((1, gw), lambda b, gi, c: (0, gi)),
                  pl.BlockSpec((1, ns), lambda b, gi, c: (0, cw_b + gi)),
                  pl.BlockSpec((1, ns), lambda b, gi, c: (0, cw_c + gi)),
                  pl.BlockSpec((1, 1, hpg), lambda b, gi, c: (gi, 0, 0)),
                  pl.BlockSpec((1, hpg, 1), lambda b, gi, c: (gi, 0, 0)),
                  pl.BlockSpec((1, 1, hpg), lambda b, gi, c: (gi, 0, 0)),
                  pl.BlockSpec((1, hpg, 1), lambda b, gi, c: (gi, 0, 0)),
                  pl.BlockSpec((1, 1, gw), lambda b, gi, c: (gi, 0, 0)),
                  pl.BlockSpec((1, gw), lambda b, gi, c: (0, gi))],
        out_specs=pl.BlockSpec((chunk, gw), lambda b, gi, c: (b * nc + c, gi)),
        scratch_shapes=[pltpu.VMEM((ns, gw), F32),
                        pltpu.VMEM((SSD_HIST + chunk, gw), F32),
                        pltpu.VMEM((SSD_HIST + chunk, ns), F32),
                        pltpu.VMEM((SSD_HIST + chunk, ns), F32)],
        compiler_params=_params(("parallel", "parallel", "arbitrary")),
        name="ssd_mixer",
    )(pc, pc, pc, pc, dt_c, dt_r, conv_w, conv_w, conv_w, conv_b2, conv_b2, conv_b2,
      dbc, dbr, alc, alr, dsk, ng)


LOG2_E = 1.4426950408889634
NSA_TQ = 128
NSA_KB = 512


def _rope(x, cos, sin):
    lane = lax.broadcasted_iota(jnp.int32, x.shape, 1)
    half = ROPE_DIM // 2
    swapped = jnp.where(lane < half, pltpu.roll(x, LANES - half, axis=1), pltpu.roll(x, half, axis=1))
    return x * cos + swapped * sin


def _compress(src_ref, pe_ref, w1_ref, w2_ref, nblk):
    hd = NSA_HEAD_DIM
    a0 = jnp.zeros((nblk, CMP_HIDDEN), F32)
    a1 = jnp.zeros((nblk, CMP_HIDDEN), F32)
    for c in range(CMP_STRIDE):
        xc = src_ref[pl.ds(c, nblk, stride=CMP_STRIDE), :]
        a0 = a0 + _dot((xc + pe_ref[c:c + 1, :]).astype(BF16), w1_ref[c * hd:(c + 1) * hd, :])
        c1 = CMP_STRIDE + c
        a1 = a1 + _dot((xc + pe_ref[c1:c1 + 1, :]).astype(BF16), w1_ref[c1 * hd:(c1 + 1) * hd, :])
    hid = a0 + pltpu.roll(a1, nblk - 1, axis=0)
    return _dot(_silu(hid).astype(BF16), w2_ref[...])


def _nsa_kernel(q_ref, kc_ref, vc_ref, ks_ref, vs_ref, kw_ref, vw_ref, g_ref,
                cos_ref, sin_ref, pek_ref, pev_ref, w1k_ref, w2k_ref, w1v_ref, w2v_ref, ovt_ref,
                o_ref, kcmp_s, vcmp_s, ksel_s, vsel_s, kwin_s, vwin_s, krope_s,
                sbuf_s, mx_s, l_s, acc_s, *, seq):
    tq, hd, nr = NSA_TQ, NSA_HEAD_DIM, NSA_Q_PER_KV
    i = pl.program_id(2)
    nblk = seq // CMP_STRIDE
    nsel = seq // SEL_BLOCK
    scale = hd ** -0.5

    @pl.when(i == 0)
    def _():
        cos, sin = cos_ref[...], sin_ref[...]
        krope_s[...] = _rope(kc_ref[...], cos, sin)
        ksel_s[...] = _rope(ks_ref[...], cos, sin).astype(BF16)
        kwin_s[...] = _rope(kw_ref[...], cos, sin).astype(BF16)
        vsel_s[...] = vs_ref[...].astype(BF16)
        vwin_s[...] = vw_ref[...].astype(BF16)
        kcmp_s[...] = _compress(krope_s, pek_ref, w1k_ref, w2k_ref, nblk).astype(BF16)
        vcmp_s[...] = _compress(vc_ref, pev_ref, w1v_ref, w2v_ref, nblk).astype(BF16)

    t0 = pl.multiple_of(i * tq, tq)
    cos = cos_ref[pl.ds(t0, tq), :]
    sin = sin_ref[pl.ds(t0, tq), :]
    qs = [_rope(q_ref[:, r * hd:(r + 1) * hd], cos, sin) for r in range(nr)]
    qb = jnp.concatenate(qs, axis=0).astype(BF16)
    c2 = scale * LOG2_E
    tpos = t0 + lax.broadcasted_iota(jnp.int32, (tq, 1), 0)

    ncol = lax.broadcasted_iota(jnp.int32, (1, nblk), 1)
    ok = (ncol * CMP_STRIDE + (CMP_LEN - 1)) <= tpos
    bias_c = jnp.where(ok, 0.0, NEG_INF)
    s = _dot_nt(qb, kcmp_s[...]).reshape(nr, tq, nblk) + bias_c[None]
    e = jnp.exp2((s - jnp.max(s, axis=-1, keepdims=True)) * c2)
    p_cmp = jnp.where(ok[None], e * (1.0 / jnp.sum(e, axis=-1, keepdims=True)), 0.0)
    o_cmp = _dot(p_cmp.reshape(nr * tq, nblk).astype(BF16), vcmp_s[...]).reshape(nr, tq, hd)

    p_sum = p_cmp[0]
    for r in range(1, nr):
        p_sum = p_sum + p_cmp[r]
    ph = p_sum.astype(BF16)
    pm = (p_sum - ph.astype(F32)).astype(BF16)
    ovt = ovt_ref[...]
    p_slc = _dot_nt(ovt, ph) + _dot_nt(ovt, pm)
    jrow = lax.broadcasted_iota(jnp.int32, (nsel, tq), 0)
    blk_t = (t0 + lax.broadcasted_iota(jnp.int32, (nsel, tq), 1)) >> SEL_SHIFT
    forced = (jrow == 0) | (jrow == blk_t) | (jrow == blk_t - 1)
    score = jnp.where(forced, FORCE_SCORE, jnp.where(jrow <= blk_t, p_slc, NEG_INF))
    rank = jnp.zeros((nsel, tq), F32)
    for j in range(nsel):
        sj = score[j:j + 1, :]
        beats = jnp.where(sj > score, 1.0, jnp.where((sj == score) & (jrow > j), 1.0, 0.0))
        rank = rank + beats
    sel_t = jnp.where(rank < float(min(SEL_TOPK, nsel)), 1.0, 0.0)
    sel = jnp.transpose(sel_t).astype(BF16)

    wlen = min(WINDOW + tq, seq)
    w0 = pl.multiple_of(jnp.maximum(t0 + tq - wlen, 0), tq)
    ktw = kwin_s[pl.ds(w0, wlen), :]
    vtw = vwin_s[pl.ds(w0, wlen), :]
    kposw = w0 + lax.broadcasted_iota(jnp.int32, (1, wlen), 1)
    bias_w = jnp.where((kposw <= tpos) & (kposw > tpos - WINDOW), 0.0, NEG_INF)
    sw = _dot_nt(qb, ktw).reshape(nr, tq, wlen) + bias_w[None]
    ew = jnp.exp2((sw - jnp.max(sw, axis=-1, keepdims=True)) * c2)
    o_win = (_dot(ew.reshape(nr * tq, wlen).astype(BF16), vtw).reshape(nr, tq, hd)
             * (1.0 / jnp.sum(ew, axis=-1, keepdims=True)))

    gs = _sigmoid(g_ref[0])
    gates = [jnp.broadcast_to(gs[:, j:j + 1], (tq, hd)) for j in range(nr * 3)]
    o_cw = [gates[3 * r] * o_cmp[r] + gates[3 * r + 2] * o_win[r] for r in range(nr)]

    kb = min(NSA_KB, seq)
    n_kb = (t0 + tq + kb - 1) // kb
    nlb = kb // LANES
    mx_s[...] = jnp.full(mx_s.shape, NEG_INF, F32)
    l_s[...] = jnp.zeros(l_s.shape, F32)
    acc_s[...] = jnp.zeros(acc_s.shape, F32)

    def lane_blocks(x):
        return [x[:, b * LANES:(b + 1) * LANES] for b in range(nlb)]

    def score_step(c, carry):
        k0 = pl.multiple_of(c * kb, kb)
        erow = lax.broadcasted_iota(jnp.int32, (nsel, kb), 0)
        ecol = lax.broadcasted_iota(jnp.int32, (nsel, kb), 1)
        expand = (((k0 + ecol) >> SEL_SHIFT) == erow).astype(BF16)
        chosen = _dot(sel, expand)
        kpos = k0 + lax.broadcasted_iota(jnp.int32, (1, kb), 1)
        bias = jnp.where((chosen > 0.5) & (kpos <= tpos), 0.0, NEG_INF)
        sb = _dot_nt(qb, ksel_s[pl.ds(k0, kb), :]).reshape(nr, tq, kb) + bias[None]
        sb = sb.reshape(nr * tq, kb)
        sbuf_s[c] = sb
        mx = mx_s[...]
        for blk in lane_blocks(sb):
            mx = jnp.maximum(mx, blk)
        mx_s[...] = mx
        return carry

    lax.fori_loop(0, n_kb, score_step, 0)
    m_sel = jnp.max(mx_s[...], axis=-1, keepdims=True)

    def prob_step(c, carry):
        k0 = pl.multiple_of(c * kb, kb)
        pt = jnp.exp2((sbuf_s[c] - m_sel) * c2)
        lsum = l_s[...]
        for blk in lane_blocks(pt):
            lsum = lsum + blk
        l_s[...] = lsum
        acc_s[...] += _dot(pt.astype(BF16), vsel_s[pl.ds(k0, kb), :])
        return carry

    lax.fori_loop(0, n_kb, prob_step, 0)
    o_sel = (acc_s[...] * (1.0 / jnp.sum(l_s[...], axis=-1, keepdims=True))).reshape(nr, tq, hd)

    for r in range(nr):
        o = o_cw[r] + gates[3 * r + 1] * o_sel[r]
        o_ref[:, r * hd:(r + 1) * hd] = o.astype(o_ref.dtype)


def _nsa_tables(seq):
    half = ROPE_DIM // 2
    inv_freq = ROPE_THETA ** (-jnp.arange(half, dtype=F32) / half)
    ang = jnp.arange(seq).astype(F32)[:, None] * inv_freq[None, :]
    cos, sin = jnp.cos(ang), jnp.sin(ang)
    rest = NSA_HEAD_DIM - ROPE_DIM
    cos_t = jnp.concatenate([cos, cos, jnp.ones((seq, rest), F32)], axis=1)
    sin_t = jnp.concatenate([-sin, sin, jnp.zeros((seq, rest), F32)], axis=1)
    n_sel = seq // SEL_BLOCK
    cmp_starts = np.arange(seq // CMP_STRIDE) * CMP_STRIDE
    sel_starts = np.arange(n_sel) * SEL_BLOCK
    overlap = np.clip(np.minimum(cmp_starts[:, None] + CMP_LEN, sel_starts[None, :] + SEL_BLOCK)
                      - np.maximum(cmp_starts[:, None], sel_starts[None, :]), 0, None).astype(np.float32) / CMP_LEN
    return cos_t, sin_t, jnp.asarray(overlap.T, BF16)


def _nsa(pb, gates, pos_k, pos_v, w1k, w2k, w1v, w2v, bsz, seq):
    t = bsz * seq
    tq, hd, g, nr = NSA_TQ, NSA_HEAD_DIM, NSA_KV_GROUPS, NSA_Q_PER_KV
    nq = seq // tq
    nblk = seq // CMP_STRIDE
    nsel = seq // SEL_BLOCK
    cos_t, sin_t, ovt = _nsa_tables(seq)
    kb = min(NSA_KB, seq)
    qblk = B_NQ // (nr * hd)
    kvb = B_NKV // hd

    def kv_spec(split):
        return pl.BlockSpec((seq, hd), lambda b, gi, i: (b, kvb + split * g + gi))

    const2 = lambda b, gi, i: (0, 0)
    return pl.pallas_call(
        functools.partial(_nsa_kernel, seq=seq),
        out_shape=jax.ShapeDtypeStruct((t, NSA_WIDTH), BF16),
        grid=(bsz, g, nq),
        in_specs=[pl.BlockSpec((tq, nr * hd), lambda b, gi, i: (b * nq + i, qblk + gi))]
                 + [kv_spec(sp) for sp in range(6)]
                 + [pl.BlockSpec((1, tq, nr * 3), lambda b, gi, i: (gi, b * nq + i, 0)),
                    pl.BlockSpec((seq, hd), const2),
                    pl.BlockSpec((seq, hd), const2),
                    pl.BlockSpec((CMP_LEN, hd), const2),
                    pl.BlockSpec((CMP_LEN, hd), const2),
                    pl.BlockSpec((CMP_LEN * hd, CMP_HIDDEN), const2),
                    pl.BlockSpec((CMP_HIDDEN, hd), const2),
                    pl.BlockSpec((CMP_LEN * hd, CMP_HIDDEN), const2),
                    pl.BlockSpec((CMP_HIDDEN, hd), const2),
                    pl.BlockSpec((nsel, nblk), const2)],
        out_specs=pl.BlockSpec((tq, nr * hd), lambda b, gi, i: (b * nq + i, gi)),
        scratch_shapes=[pltpu.VMEM((nblk, hd), BF16), pltpu.VMEM((nblk, hd), BF16),
                        pltpu.VMEM((seq, hd), BF16), pltpu.VMEM((seq, hd), BF16),
                        pltpu.VMEM((seq, hd), BF16), pltpu.VMEM((seq, hd), BF16),
                        pltpu.VMEM((seq, hd), F32),
                        pltpu.VMEM((seq // kb, nr * tq, kb), F32),
                        pltpu.VMEM((nr * tq, LANES), F32),
                        pltpu.VMEM((nr * tq, LANES), F32),
                        pltpu.VMEM((nr * tq, hd), F32)],
        compiler_params=_params(("parallel", "parallel", "arbitrary")),
        name="nsa_mixer",
    )(pb, pb, pb, pb, pb, pb, pb, gates, cos_t, sin_t, pos_k, pos_v, w1k, w2k, w1v, w2v, ovt)


def _in_proj(x2d, gain, w_in_stack, layer):
    t = x2d.shape[0]
    tm = min(PROJ_TILE_M, t)
    h = _rms_cast(x2d, gain, tm)
    pa = _proj(h, w_in_stack, layer, 0, PA_COLS, tm, PROJ_TILE_N)
    pb = _proj(h, w_in_stack, layer, PB_START, PB_COLS, tm, PROJ_TILE_N)
    pc = _proj(h, w_in_stack, layer, PC_START, PC_COLS, tm, PROJ_TILE_N)
    ps = _proj_small(h, w_in_stack, layer, tm)
    return pa, pb, pc, ps


def _small_views(ps):
    t = ps.shape[0]
    nsa_g = ps[:, S_NG:S_NG + 3 * NSA_HEADS]
    nsa_g = nsa_g.reshape(t, NSA_KV_GROUPS, NSA_Q_PER_KV * 3).transpose(1, 0, 2)
    dt = ps[:, S_DT:S_DT + SSD_HEADS].reshape(t, SSD_GROUPS, SSD_HEADS_PER_GROUP)
    sub = min(SSD_SUB, t)
    dt_r = dt.reshape(t // sub, sub, SSD_GROUPS, SSD_HEADS_PER_GROUP).transpose(2, 0, 3, 1)
    return nsa_g, dt.transpose(1, 0, 2), dt_r


def _layer(x2d, bsz, seq, layer, norm_mix, w_in, gla_gate_w2, gla_gate_b, gla_out_norm,
           nsa_cmp_pos_k, nsa_cmp_pos_v, nsa_cmp_k_w1, nsa_cmp_k_w2, nsa_cmp_v_w1, nsa_cmp_v_w2,
           ssd_conv_w, ssd_conv_b, ssd_dt_bias, ssd_a_log, ssd_d, ssd_out_norm,
           w_branch, w_out, norm_ffn, w_ffn_gate, w_ffn_up, w_ffn_down, norm_final, final_norm):
    t = bsz * seq
    pa, pb, pc, ps = _in_proj(x2d, norm_mix, w_in, layer)
    nsa_g, dt_c, dt_r = _small_views(ps)

    w2pad = jnp.zeros((LANES, GLA_DK), BF16).at[S_GLOW:S_GLOW + GLA_LOWRANK].set(
        gla_gate_w2.astype(BF16))
    o_gla = _gla(pa, pb, ps, w2pad, gla_gate_b, gla_out_norm, bsz, seq, min(256, seq))
    o_nsa = _nsa(pb, nsa_g, nsa_cmp_pos_k, nsa_cmp_pos_v,
                 nsa_cmp_k_w1.astype(BF16), nsa_cmp_k_w2.astype(BF16),
                 nsa_cmp_v_w1.astype(BF16), nsa_cmp_v_w2.astype(BF16), bsz, seq)
    o_ssd = _ssd(pc, dt_c, dt_r, ssd_conv_w, ssd_conv_b, ssd_dt_bias, ssd_a_log, ssd_d,
                 ssd_out_norm, bsz, seq, min(512, seq))

    merged = _merge(o_gla, o_nsa, o_ssd, pa, w_branch.astype(BF16), min(512, t), 512)
    tm = min(PROJ_TILE_M, t)
    x2d = _proj(merged, w_out, layer, 0, D_MODEL, tm, PROJ_TILE_N, residual=x2d, name="out_proj")
    return _ffn(x2d, norm_ffn, norm_final, w_ffn_gate.astype(BF16), w_ffn_up.astype(BF16),
                w_ffn_down.astype(BF16), min(512, t), 512, final_norm)


def kernel(x, norm_mix, w_in, gla_gate_w2, gla_gate_b, gla_out_norm, nsa_cmp_pos_k, nsa_cmp_pos_v,
           nsa_cmp_k_w1, nsa_cmp_k_w2, nsa_cmp_v_w1, nsa_cmp_v_w2, ssd_conv_w, ssd_conv_b,
           ssd_dt_bias, ssd_a_log, ssd_d, ssd_out_norm, w_branch, w_out, norm_ffn, w_ffn_gate,
           w_ffn_up, w_ffn_down, norm_final):
    bsz, seq, d = x.shape
    depth = norm_mix.shape[0]
    x2d = x.reshape(bsz * seq, d)
    for l in range(depth):
        x2d = _layer(x2d, bsz, seq, l, norm_mix[l], w_in, gla_gate_w2[l], gla_gate_b[l],
                     gla_out_norm[l], nsa_cmp_pos_k[l], nsa_cmp_pos_v[l], nsa_cmp_k_w1[l],
                     nsa_cmp_k_w2[l], nsa_cmp_v_w1[l], nsa_cmp_v_w2[l], ssd_conv_w[l],
                     ssd_conv_b[l], ssd_dt_bias[l], ssd_a_log[l], ssd_d[l], ssd_out_norm[l],
                     w_branch[l], w_out, norm_ffn[l], w_ffn_gate[l], w_ffn_up[l],
                     w_ffn_down[l], norm_final, l == depth - 1)
    return x2d.reshape(bsz, seq, d)
```

```python
import functools

import numpy as np
import jax
import jax.numpy as jnp
from jax import lax
from jax.experimental import pallas as pl
from jax.experimental.pallas import tpu as pltpu

F32 = jnp.float32
BF16 = jnp.bfloat16

D_MODEL = 2048
RMS_EPS = 1e-6
LOG2_E = 1.4426950408889634
NEG_INF = -1e30
FORCE_SCORE = 1e9

GLA_HEADS = 4
GLA_DK = D_MODEL // 2
GLA_DV = D_MODEL
GLA_HK = GLA_DK // GLA_HEADS
GLA_HV = GLA_DV // GLA_HEADS
GLA_LOWRANK = 16
GLA_GATE_NORMALIZER = 16.0

NSA_HEAD_DIM = 128
NSA_HEAD_SHIFT = 7
NSA_HEADS = D_MODEL // NSA_HEAD_DIM
NSA_KV_GROUPS = 4
NSA_Q_PER_KV = NSA_HEADS // NSA_KV_GROUPS
NSA_WIDTH = NSA_HEADS * NSA_HEAD_DIM
NSA_KV_WIDTH = NSA_KV_GROUPS * NSA_HEAD_DIM
CMP_LEN = 32
CMP_STRIDE = 16
CMP_HIDDEN = 2 * NSA_HEAD_DIM
SEL_BLOCK = 64
SEL_SHIFT = 6
SEL_TOPK = 16
WINDOW = 512
ROPE_THETA = 500000.0
ROPE_DIM = NSA_HEAD_DIM // 4

SSD_D_INNER = 2 * D_MODEL
SSD_HEAD_DIM = 64
SSD_HEAD_SHIFT = 6
SSD_HEADS = SSD_D_INNER // SSD_HEAD_DIM
SSD_GROUPS = 8
SSD_HEADS_PER_GROUP = SSD_HEADS // SSD_GROUPS
SSD_D_STATE = 128
SSD_CONV = 4
SSD_GROUP_WIDTH = SSD_HEADS_PER_GROUP * SSD_HEAD_DIM
SSD_CONV_DIM = SSD_D_INNER + 2 * SSD_GROUPS * SSD_D_STATE

D_FF = ((8 * D_MODEL + 3 * 256 - 1) // (3 * 256)) * 256

LANES = 128
V7X_VMEM_LIMIT_BYTES = 56 * 1024 * 1024

_REF_SIZES = (3 * D_MODEL, GLA_DK, GLA_DK, GLA_DV, GLA_LOWRANK, GLA_DV,
              NSA_WIDTH, 6 * NSA_KV_WIDTH, 3 * NSA_HEADS,
              SSD_D_INNER, SSD_CONV_DIM, SSD_HEADS)
_REF_OFF = tuple(int(v) for v in np.concatenate([[0], np.cumsum(_REF_SIZES)]))
(_R_GATE, _R_GQ, _R_GK, _R_GV, _R_GLOW, _R_GR, _R_NQ, _R_NKV, _R_NG,
 _R_SZ, _R_SXBC, _R_SDT) = _REF_OFF[:-1]
PA_COLS = _R_GLOW
PB_START, PB_COLS = _R_GR, _R_NG - _R_GR
PC_START, PC_COLS = _R_SZ, _R_SDT - _R_SZ
A_GATE, A_GQ, A_GK, A_GV = _R_GATE, _R_GQ, _R_GK, _R_GV
B_GR, B_NQ, B_NKV = 0, _R_NQ - _R_GR, _R_NKV - _R_GR
C_SZ, C_SXBC = 0, _R_SXBC - _R_SZ
S_BLOCKS = (_R_GLOW // LANES, _R_NG // LANES, _R_SDT // LANES)
S_GLOW = _R_GLOW % LANES
S_NG = LANES + _R_NG % LANES
S_DT = 2 * LANES + _R_SDT % LANES
PROJ_TILE_M = 1024
PROJ_TILE_N = 1024
PROJ_CAST_ROWS = 256


def _params(sem, vmem=V7X_VMEM_LIMIT_BYTES):
    return pltpu.CompilerParams(dimension_semantics=sem, vmem_limit_bytes=vmem)


def _sigmoid(x):
    return 1.0 / (1.0 + jnp.exp(-x))


def _silu(x):
    return x * _sigmoid(x)


def _softplus(x):
    return jnp.maximum(x, 0.0) + jnp.log(1.0 + jnp.exp(-jnp.abs(x)))


def _split3(x):
    hi = x.astype(BF16)
    r1 = x - hi.astype(F32)
    mid = r1.astype(BF16)
    lo = (r1 - mid.astype(F32)).astype(BF16)
    return hi, mid, lo


def _dot(a, b):
    return jnp.dot(a, b, preferred_element_type=F32)


def _dot_nt(a, b):
    return lax.dot_general(a, b, (((1,), (1,)), ((), ())), preferred_element_type=F32)


def _dot_tn(a, b):
    return lax.dot_general(a, b, (((0,), (0,)), ((), ())), preferred_element_type=F32)


def _sel_dot(sel, x):
    hi, mid, lo = _split3(x)
    return _dot(sel, hi) + _dot(sel, mid) + _dot(sel, lo)


def _sel_dot_left(x, sel):
    hi, mid, lo = _split3(x)
    return _dot(hi, sel) + _dot(mid, sel) + _dot(lo, sel)


def _rms_rows(x, gain):
    ms = jnp.mean(x * x, axis=-1, keepdims=True)
    return x * lax.rsqrt(ms + RMS_EPS) * gain


def _rms_cast_kernel(x_ref, g_ref, o_ref):
    o_ref[...] = _rms_rows(x_ref[...], g_ref[...]).astype(o_ref.dtype)


def _rms_cast(x2d, gain, tm):
    t, d = x2d.shape
    return pl.pallas_call(
        _rms_cast_kernel,
        out_shape=jax.ShapeDtypeStruct((t, d), BF16),
        grid=(t // tm,),
        in_specs=[pl.BlockSpec((tm, d), lambda i: (i, 0)),
                  pl.BlockSpec((1, d), lambda i: (0, 0))],
        out_specs=pl.BlockSpec((tm, d), lambda i: (i, 0)),
        compiler_params=_params(("parallel",)),
        name="rms_cast",
    )(x2d, gain.reshape(1, d))


def _proj_kernel(*refs, shift, residual):
    refs = list(refs)
    h_ref, wa_ref = refs[:2]
    wb_ref = refs[2] if shift else None
    res_ref = refs[-3] if residual else None
    o_ref, w_s = refs[-2:]
    tn = w_s.shape[1]

    @pl.when(pl.program_id(1) == 0)
    def _():
        for r0 in range(0, w_s.shape[0], PROJ_CAST_ROWS):
            rows = pl.ds(r0, PROJ_CAST_ROWS)
            w = wa_ref[rows, :]
            if shift:
                wcat = jnp.concatenate([w, wb_ref[rows, :]], axis=1)
                w = pltpu.roll(wcat, tn + LANES - shift, axis=1)[:, :tn]
            w_s[rows, :] = w.astype(BF16)

    y = _dot(h_ref[...], w_s[...])
    o_ref[...] = y + res_ref[...] if residual else y


def _proj(h, w_stack, layer, start, ncols, tm, tn, residual=None, name="in_proj"):
    t, d = h.shape
    shift = start % LANES
    base = (start - shift) // tn
    assert (start - shift) % tn == 0 and ncols % tn == 0
    in_specs = [pl.BlockSpec((tm, d), lambda j, i: (i, 0)),
                pl.BlockSpec((None, d, tn), lambda j, i: (layer, 0, base + j))]
    args = [h, w_stack]
    if shift:
        nxt = tn // LANES
        in_specs.append(pl.BlockSpec((None, d, LANES), lambda j, i: (layer, 0, (base + j + 1) * nxt)))
        args.append(w_stack)
    if residual is not None:
        in_specs.append(pl.BlockSpec((tm, tn), lambda j, i: (i, j)))
        args.append(residual)
    return pl.pallas_call(
        functools.partial(_proj_kernel, shift=shift, residual=residual is not None),
        out_shape=jax.ShapeDtypeStruct((t, ncols), F32),
        grid=(ncols // tn, t // tm),
        in_specs=in_specs,
        out_specs=pl.BlockSpec((tm, tn), lambda j, i: (i, j)),
        scratch_shapes=[pltpu.VMEM((d, tn), BF16)],
        compiler_params=_params(("parallel", "arbitrary")),
        name=name,
    )(*args)


def _proj_small(h, w_stack, layer, tm):
    t, d = h.shape
    b0, b1, b2 = S_BLOCKS

    def wmap(j, i):
        return (layer, 0, jnp.where(j == 0, b0, jnp.where(j == 1, b1, b2)))

    return pl.pallas_call(
        functools.partial(_proj_kernel, shift=0, residual=False),
        out_shape=jax.ShapeDtypeStruct((t, len(S_BLOCKS) * LANES), F32),
        grid=(len(S_BLOCKS), t // tm),
        in_specs=[pl.BlockSpec((tm, d), lambda j, i: (i, 0)),
                  pl.BlockSpec((None, d, LANES), wmap)],
        out_specs=pl.BlockSpec((tm, LANES), lambda j, i: (i, j)),
        scratch_shapes=[pltpu.VMEM((d, LANES), BF16)],
        compiler_params=_params(("parallel", "arbitrary")),
        name="in_proj_small",
    )(h, w_stack)


def _merge_kernel(og_ref, on_ref, os_ref, g0_ref, g1_ref, g2_ref, wb0_ref, wb1_ref, wb2_ref, o_ref):
    m = (_sigmoid(g0_ref[...]) * _dot(og_ref[...], wb0_ref[...])
         + _sigmoid(g1_ref[...]) * _dot(on_ref[...], wb1_ref[...])
         + _sigmoid(g2_ref[...]) * _dot(os_ref[...], wb2_ref[...]))
    o_ref[...] = m.astype(o_ref.dtype)


def _merge(o_gla, o_nsa, o_ssd, pa, wb, tm, tn):
    t = o_gla.shape[0]
    d = D_MODEL
    nj = d // tn
    gate_blk0 = A_GATE // tn
    return pl.pallas_call(
        _merge_kernel,
        out_shape=jax.ShapeDtypeStruct((t, d), BF16),
        grid=(t // tm, nj),
        in_specs=[pl.BlockSpec((tm, GLA_DV), lambda i, j: (i, 0)),
                  pl.BlockSpec((tm, NSA_WIDTH), lambda i, j: (i, 0)),
                  pl.BlockSpec((tm, SSD_D_INNER), lambda i, j: (i, 0)),
                  pl.BlockSpec((tm, tn), lambda i, j: (i, gate_blk0 + j)),
                  pl.BlockSpec((tm, tn), lambda i, j: (i, gate_blk0 + nj + j)),
                  pl.BlockSpec((tm, tn), lambda i, j: (i, gate_blk0 + 2 * nj + j)),
                  pl.BlockSpec((GLA_DV, tn), lambda i, j: (0, j)),
                  pl.BlockSpec((NSA_WIDTH, tn), lambda i, j: (1, j)),
                  pl.BlockSpec((SSD_D_INNER, tn), lambda i, j: (1, j))],
        out_specs=pl.BlockSpec((tm, tn), lambda i, j: (i, j)),
        compiler_params=_params(("parallel", "arbitrary")),
        name="branch_merge",
    )(o_gla, o_nsa, o_ssd, pa, pa, pa, wb, wb, wb)


def _ffn_kernel(x_ref, g_ref, gf_ref, wg_ref, wu_ref, wd_ref, o_ref, h_ref, a_ref, *, final_norm):
    j = pl.program_id(1)
    last = pl.num_programs(1) - 1

    def hidden():
        h = h_ref[...]
        return (_silu(_dot(h, wg_ref[...])) * _dot(h, wu_ref[...])).astype(BF16)

    @pl.when(j == 0)
    def _():
        x = x_ref[...]
        h_ref[...] = _rms_rows(x, g_ref[...]).astype(BF16)
        o_ref[...] = x
        a_ref[...] = hidden()

    @pl.when((j > 0) & (j < last))
    def _():
        a_new = hidden()
        o_ref[...] += _dot(a_ref[...], wd_ref[...])
        a_ref[...] = a_new

    @pl.when(j == last)
    def _():
        y = o_ref[...] + _dot(a_ref[...], wd_ref[...])
        o_ref[...] = _rms_rows(y, gf_ref[...]) if final_norm else y


def _ffn(x2d, gain, gain_final, wg, wu, wd, tm, tf, final_norm):
    t, d = x2d.shape
    nf = wg.shape[1] // tf
    return pl.pallas_call(
        functools.partial(_ffn_kernel, final_norm=final_norm),
        out_shape=jax.ShapeDtypeStruct((t, d), F32),
        grid=(t // tm, nf + 1),
        in_specs=[pl.BlockSpec((tm, d), lambda i, j: (i, 0)),
                  pl.BlockSpec((1, d), lambda i, j: (0, 0)),
                  pl.BlockSpec((1, d), lambda i, j: (0, 0)),
                  pl.BlockSpec((d, tf), lambda i, j: (0, jnp.minimum(j, nf - 1))),
                  pl.BlockSpec((d, tf), lambda i, j: (0, jnp.minimum(j, nf - 1))),
                  pl.BlockSpec((tf, d), lambda i, j: (jnp.maximum(j - 1, 0), 0))],
        out_specs=pl.BlockSpec((tm, d), lambda i, j: (i, 0)),
        scratch_shapes=[pltpu.VMEM((tm, d), BF16), pltpu.VMEM((tm, tf), BF16)],
        compiler_params=_params(("parallel", "arbitrary")),
        name="ffn",
    )(x2d, gain.reshape(1, d), gain_final.reshape(1, d), wg, wu, wd)


GLA_CHUNK = 64
GLA_SUB = 16
GLA_EXP_CAP = 88.0
GLA_HEADS_PER_STEP = 2


def _gla_kernel(q_ref, k_ref, v_ref, r_ref, sm_ref, w2_ref, gb_ref, ng_ref, o_ref, st_ref,
                *, chunks_per_tile):
    @pl.when(pl.program_id(2) == 0)
    def _():
        st_ref[...] = jnp.zeros_like(st_ref)

    c_rows = GLA_CHUNK
    row = lax.broadcasted_iota(jnp.int32, (c_rows, c_rows), 0)
    col = lax.broadcasted_iota(jnp.int32, (c_rows, c_rows), 1)
    tri = (col <= row).astype(BF16)
    srow = lax.broadcasted_iota(jnp.int32, (GLA_SUB, c_rows), 0)
    scol = lax.broadcasted_iota(jnp.int32, (GLA_SUB, c_rows), 1)
    scale = GLA_HK ** -0.5

    smb = sm_ref[...].astype(BF16)
    work = []
    for h in range(GLA_HEADS_PER_STEP):
        kcols = slice(h * GLA_HK, (h + 1) * GLA_HK)
        vcols = slice(h * GLA_HV, (h + 1) * GLA_HV)
        ga = _dot(smb, w2_ref[:, kcols]) + gb_ref[:, kcols]
        log_a_all = -_softplus(-ga) * (1.0 / GLA_GATE_NORMALIZER)
        for c in range(chunks_per_tile):
            rows = slice(c * c_rows, (c + 1) * c_rows)
            bc = _sel_dot(tri, log_a_all[rows])
            q = q_ref[rows, kcols] * scale
            k = k_ref[rows, kcols]
            vb = v_ref[rows, vcols].astype(BF16)
            parts = []
            for i in range(c_rows // GLA_SUB):
                lo = i * GLA_SUB
                ref_pt = bc[lo - 1:lo, :] if i > 0 else jnp.zeros((1, GLA_HK), F32)
                kt = (k * jnp.exp(jnp.minimum(ref_pt - bc, GLA_EXP_CAP))).astype(BF16)
                qi = (q[lo:lo + GLA_SUB, :] * jnp.exp(bc[lo:lo + GLA_SUB, :] - ref_pt)).astype(BF16)
                sc = _dot_nt(qi, kt)
                parts.append(jnp.where(scol <= srow + lo, sc, 0.0))
            intra = _dot(jnp.concatenate(parts, axis=0).astype(BF16), vb)
            b_last = bc[c_rows - 1:c_rows, :]
            khat = (k * jnp.exp(b_last - bc)).astype(BF16)
            work.append((h, rows, vcols, (q * jnp.exp(bc)).astype(BF16), intra,
                         jnp.exp(b_last), _dot_tn(vb, khat)))

    states = [st_ref[h] for h in range(GLA_HEADS_PER_STEP)]
    for h, rows, vcols, qe, intra, decay, incr in work:
        out = _dot_nt(qe, states[h].astype(BF16)) + intra
        states[h] = states[h] * decay + incr
        y = _rms_rows(out, ng_ref[...]) * _silu(r_ref[rows, vcols])
        o_ref[rows, vcols] = y.astype(o_ref.dtype)
    for h in range(GLA_HEADS_PER_STEP):
        st_ref[h] = states[h]


def _gla(pa, pb, ps, w2pad, gate_b, norm_g, bsz, seq, tile):
    t = bsz * seq
    nt = seq // tile
    hps = GLA_HEADS_PER_STEP
    kw, vw = hps * GLA_HK, hps * GLA_HV
    qb, kb = A_GQ // kw, A_GK // kw
    vb, rb = A_GV // vw, B_GR // vw
    return pl.pallas_call(
        functools.partial(_gla_kernel, chunks_per_tile=tile // GLA_CHUNK),
        out_shape=jax.ShapeDtypeStruct((t, GLA_DV), BF16),
        grid=(bsz, GLA_HEADS // hps, nt),
        in_specs=[pl.BlockSpec((tile, kw), lambda b, h, c: (b * nt + c, qb + h)),
                  pl.BlockSpec((tile, kw), lambda b, h, c: (b * nt + c, kb + h)),
                  pl.BlockSpec((tile, vw), lambda b, h, c: (b * nt + c, vb + h)),
                  pl.BlockSpec((tile, vw), lambda b, h, c: (b * nt + c, rb + h)),
                  pl.BlockSpec((tile, LANES), lambda b, h, c: (b * nt + c, 0)),
                  pl.BlockSpec((LANES, kw), lambda b, h, c: (0, h)),
                  pl.BlockSpec((1, kw), lambda b, h, c: (0, h)),
                  pl.BlockSpec((1, GLA_HV), lambda b, h, c: (0, 0))],
        out_specs=pl.BlockSpec((tile, vw), lambda b, h, c: (b * nt + c, h)),
        scratch_shapes=[pltpu.VMEM((hps, GLA_HV, GLA_HK), F32)],
        compiler_params=_params(("parallel", "parallel", "arbitrary")),
        name="gla_mixer",
    )(pa, pa, pa, pb, ps, w2pad, gate_b.reshape(1, GLA_DK), norm_g.reshape(1, GLA_HV))


SSD_SUB = 128
SSD_HIST = 8


def _causal_conv(e_ref, src_ref, w_ref, b_ref):
    n = src_ref.shape[0]
    e_ref[pl.ds(SSD_HIST, n), :] = src_ref[...]
    acc = src_ref[...] * w_ref[SSD_CONV - 1:SSD_CONV, :] + b_ref[...]
    for s in range(1, SSD_CONV):
        acc = acc + e_ref[pl.ds(SSD_HIST - s, n), :] * w_ref[SSD_CONV - 1 - s:SSD_CONV - s, :]
    e_ref[pl.ds(0, SSD_HIST), :] = src_ref[pl.ds(n - SSD_HIST, SSD_HIST), :]
    return _silu(acc)


def _ssd_kernel(xs_ref, bm_ref, cm_ref, z_ref, dtc_ref, dtr_ref,
                wx_ref, wb_ref, wc_ref, bx_ref, bb_ref, bc_ref,
                dbc_ref, dbr_ref, alc_ref, alr_ref, dsk_ref, ng_ref,
                o_ref, st_ref, ex_ref, eb_ref, ec_ref):
    n = xs_ref.shape[0]
    sub = min(SSD_SUB, n)
    hpg, hd = SSD_HEADS_PER_GROUP, SSD_HEAD_DIM

    @pl.when(pl.program_id(2) == 0)
    def _():
        st_ref[...] = jnp.zeros_like(st_ref)
        ex_ref[pl.ds(0, SSD_HIST), :] = jnp.zeros((SSD_HIST, ex_ref.shape[1]), F32)
        eb_ref[pl.ds(0, SSD_HIST), :] = jnp.zeros((SSD_HIST, eb_ref.shape[1]), F32)
        ec_ref[pl.ds(0, SSD_HIST), :] = jnp.zeros((SSD_HIST, ec_ref.shape[1]), F32)

    xs = _causal_conv(ex_ref, xs_ref, wx_ref, bx_ref)
    bmb = _causal_conv(eb_ref, bm_ref, wb_ref, bb_ref).astype(BF16)
    cmb = _causal_conv(ec_ref, cm_ref, wc_ref, bc_ref).astype(BF16)

    w3 = 3 * hpg
    krow = lax.broadcasted_iota(jnp.int32, (w3, hpg * hd), 0)
    kcol = lax.broadcasted_iota(jnp.int32, (w3, hpg * hd), 1) >> SSD_HEAD_SHIFT
    expand3 = ((krow == kcol) | (krow == kcol + hpg) | (krow == kcol + 2 * hpg)).astype(BF16)

    def expand_heads(x):
        hi, mid, lo = _split3(x)
        klane = lax.broadcasted_iota(jnp.int32, x.shape, 1)
        return _dot(jnp.where(klane < hpg, hi, jnp.where(klane < 2 * hpg, mid, lo)), expand3)

    dt_c = _softplus(dtc_ref[0] + dbc_ref[0])
    a_c = -jnp.exp(alc_ref[0])
    a_r = -jnp.exp(alr_ref[0])
    xdt = xs * expand_heads(dt_c)

    row = lax.broadcasted_iota(jnp.int32, (sub, sub), 0)
    col = lax.broadcasted_iota(jnp.int32, (sub, sub), 1)
    tri = (col <= row).astype(BF16)
    triu = (row <= col).astype(BF16)
    causal_bias = jnp.where(col <= row, 0.0, NEG_INF)
    lane = lax.broadcasted_iota(jnp.int32, (sub, 2 * hd), 1)

    for si in range(n // sub):
        rows = slice(si * sub, (si + 1) * sub)
        acum_c = _sel_dot(tri, dt_c[rows] * a_c)
        adt_r = _softplus(dtr_ref[0, si] + dbr_ref[0]) * a_r
        acum_r = _sel_dot_left(adt_r, triu)
        a_last = acum_c[sub - 1:sub, :]
        eac_x = expand_heads(jnp.exp(acum_c))
        dec_x = expand_heads(jnp.exp(a_last - acum_c))
        sdec_x = expand_heads(jnp.broadcast_to(jnp.exp(a_last), (8, w3)))[0:1, :]
        ac2 = acum_c * LOG2_E
        ar2 = acum_r * LOG2_E

        xd, cs, bs = xdt[rows], cmb[rows], bmb[rows]
        cb = _dot_nt(cs, bs)
        pair_out = []
        for pr in range(hpg // 2):
            xp = xd[:, pr * 2 * hd:(pr + 1) * 2 * hd]
            acc = None
            for half in range(2):
                r = 2 * pr + half
                seg = jnp.exp2(ac2[:, r:r + 1] - ar2[r:r + 1, :] + causal_bias)
                rhs = jnp.where((lane >> SSD_HEAD_SHIFT) == half, xp, 0.0).astype(BF16)
                term = _dot((cb * seg).astype(BF16), rhs)
                acc = term if acc is None else acc + term
            pair_out.append(acc)
        y_diag = jnp.concatenate(pair_out, axis=1)

        st = st_ref[...]
        y_off = _dot(cs, st.astype(BF16)) * eac_x
        st_ref[...] = st * sdec_x + _dot_tn(bs, (xd * dec_x).astype(BF16))

        y = (y_diag + y_off + xs[rows] * dsk_ref[0]) * _silu(z_ref[rows, :])
        o_ref[rows, :] = _rms_rows(y, ng_ref[...]).astype(o_ref.dtype)


def _ssd(pc, dt_c, dt_r, conv_w, conv_b, dt_bias, a_log, d_skip, norm_g, bsz, seq, chunk):
    t = bsz * seq
    nc = seq // chunk
    sub = min(SSD_SUB, chunk)
    g, hpg, gw, ns = SSD_GROUPS, SSD_HEADS_PER_GROUP, SSD_GROUP_WIDTH, SSD_D_STATE
    xsb = C_SXBC // gw
    bmb = (C_SXBC + SSD_D_INNER) // ns
    cmb = (C_SXBC + SSD_D_INNER + g * ns) // ns
    zb = C_SZ // gw
    cw_b = SSD_D_INNER // ns
    cw_c = (SSD_D_INNER + g * ns) // ns
    conv_b2 = conv_b.reshape(1, SSD_CONV_DIM)
    dbc = jnp.tile(dt_bias.reshape(g, 1, hpg), (1, 1, 3))
    dbr = dt_bias.reshape(g, hpg, 1)
    alc = jnp.tile(a_log.reshape(g, 1, hpg), (1, 1, 3))
    alr = a_log.reshape(g, hpg, 1)
    dsk = jnp.repeat(d_skip, SSD_HEAD_DIM).reshape(g, 1, gw)
    ng = norm_g.reshape(1, SSD_D_INNER)
    return pl.pallas_call(
        _ssd_kernel,
        out_shape=jax.ShapeDtypeStruct((t, SSD_D_INNER), BF16),
        grid=(bsz, g, nc),
        in_specs=[pl.BlockSpec((chunk, gw), lambda b, gi, c: (b * nc + c, xsb + gi)),
                  pl.BlockSpec((chunk, ns), lambda b, gi, c: (b * nc + c, bmb + gi)),
                  pl.BlockSpec((chunk, ns), lambda b, gi, c: (b * nc + c, cmb + gi)),
                  pl.BlockSpec((chunk, gw), lambda b, gi, c: (b * nc + c, zb + gi)),
                  pl.BlockSpec((1, chunk, 3 * hpg), lambda b, gi, c: (gi, b * nc + c, 0)),
                  pl.BlockSpec((1, chunk // sub, hpg, sub), lambda b, gi, c: (gi, b * nc + c, 0, 0)),
                  pl.BlockSpec((SSD_CONV, gw), lambda b, gi, c: (0, gi)),
                  pl.BlockSpec((SSD_CONV, ns), lambda b, gi, c: (0, cw_b + gi)),
                  pl.BlockSpec((SSD_CONV, ns), lambda b, gi, c: (0, cw_c + gi)),
                  pl.BlockSpec((1, gw), lambda b, gi, c: (0, gi)),
                  pl.BlockSpec((1, ns), lambda b, gi, c: (0, cw_b + gi)),
                  pl.BlockSpec((1, ns), lambda b, gi, c: (0, cw_c + gi)),
                  pl.BlockSpec((1, 1, 3 * hpg), lambda b, gi, c: (gi, 0, 0)),
                  pl.BlockSpec((1, hpg, 1), lambda b, gi, c: (gi, 0, 0)),
                  pl.BlockSpec((1, 1, 3 * hpg), lambda b, gi, c: (gi, 0, 0)),
                  pl.BlockSpec((1, hpg, 1), lambda b, gi, c: (gi, 0, 0)),
                  pl.BlockSpec((1, 1, gw), lambda b, gi, c: (gi, 0, 0)),
                  pl.BlockSpec((1, gw), lambda b, gi, c: (0, gi))],
        out_specs=pl.BlockSpec((chunk, gw), lambda b, gi, c: (b * nc + c, gi)),
        scratch_shapes=[pltpu.VMEM((ns, gw), F32),
                        pltpu.VMEM((SSD_HIST + chunk, gw), F32),
                        pltpu.VMEM((SSD_HIST + chunk, ns), F32),
                        pltpu.VMEM((SSD_HIST + chunk, ns), F32)],
        compiler_params=_params(("parallel", "parallel", "arbitrary")),
        name="ssd_mixer",
    )(pc, pc, pc, pc, dt_c, dt_r, conv_w, conv_w, conv_w, conv_b2, conv_b2, conv_b2,
      dbc, dbr, alc, alr, dsk, ng)


NSA_TQ = 256
NSA_KB = 512


def _rope(x, cos, sin):
    lane = lax.broadcasted_iota(jnp.int32, x.shape, 1)
    half = ROPE_DIM // 2
    swapped = jnp.where(lane < half, pltpu.roll(x, LANES - half, axis=1), pltpu.roll(x, half, axis=1))
    return x * cos + swapped * sin


def _compress(src_ref, pe_ref, w1_ref, w2_ref, nblk):
    hd = NSA_HEAD_DIM
    a0 = jnp.zeros((nblk, CMP_HIDDEN), F32)
    a1 = jnp.zeros((nblk, CMP_HIDDEN), F32)
    for c in range(CMP_STRIDE):
        xc = src_ref[pl.ds(c, nblk, stride=CMP_STRIDE), :]
        a0 = a0 + _dot((xc + pe_ref[c:c + 1, :]).astype(BF16), w1_ref[c * hd:(c + 1) * hd, :])
        c1 = CMP_STRIDE + c
        a1 = a1 + _dot((xc + pe_ref[c1:c1 + 1, :]).astype(BF16), w1_ref[c1 * hd:(c1 + 1) * hd, :])
    hid = a0 + pltpu.roll(a1, nblk - 1, axis=0)
    return _dot(_silu(hid).astype(BF16), w2_ref[...])


def _nsa_kernel(q_ref, kc_ref, vc_ref, ks_ref, vs_ref, kw_ref, vw_ref, g_ref,
                cos_ref, sin_ref, pek_ref, pev_ref, w1k_ref, w2k_ref, w1v_ref, w2v_ref, ovt_ref,
                o_ref, kcmp_s, vcmp_s, ksel_s, vsel_s, kwin_s, vwin_s, krope_s,
                sbuf_s, mx_s, l_s, acc_s, *, seq):
    tq, hd, nr = NSA_TQ, NSA_HEAD_DIM, NSA_Q_PER_KV
    i = pl.program_id(2)
    nblk = seq // CMP_STRIDE
    nsel = seq // SEL_BLOCK
    scale = hd ** -0.5

    @pl.when(i == 0)
    def _():
        cos, sin = cos_ref[...], sin_ref[...]
        krope_s[...] = _rope(kc_ref[...], cos, sin)
        ksel_s[...] = _rope(ks_ref[...], cos, sin).astype(BF16)
        kwin_s[...] = _rope(kw_ref[...], cos, sin).astype(BF16)
        vsel_s[...] = vs_ref[...].astype(BF16)
        vwin_s[...] = vw_ref[...].astype(BF16)
        kcmp_s[...] = _compress(krope_s, pek_ref, w1k_ref, w2k_ref, nblk).astype(BF16)
        vcmp_s[...] = _compress(vc_ref, pev_ref, w1v_ref, w2v_ref, nblk).astype(BF16)

    t0 = pl.multiple_of(i * tq, tq)
    cos = cos_ref[pl.ds(t0, tq), :]
    sin = sin_ref[pl.ds(t0, tq), :]
    qs = [_rope(q_ref[:, r * hd:(r + 1) * hd], cos, sin) for r in range(nr)]
    qb = jnp.concatenate(qs, axis=0).astype(BF16)
    c2 = scale * LOG2_E
    tpos = t0 + lax.broadcasted_iota(jnp.int32, (tq, 1), 0)

    ncol = lax.broadcasted_iota(jnp.int32, (1, nblk), 1)
    ok = (ncol * CMP_STRIDE + (CMP_LEN - 1)) <= tpos
    bias_c = jnp.where(ok, 0.0, NEG_INF)
    s = _dot_nt(qb, kcmp_s[...]).reshape(nr, tq, nblk) + bias_c[None]
    e = jnp.exp2((s - jnp.max(s, axis=-1, keepdims=True)) * c2)
    p_cmp = jnp.where(ok[None], e * (1.0 / jnp.sum(e, axis=-1, keepdims=True)), 0.0)
    o_cmp = _dot(p_cmp.reshape(nr * tq, nblk).astype(BF16), vcmp_s[...]).reshape(nr, tq, hd)

    p_sum = p_cmp[0]
    for r in range(1, nr):
        p_sum = p_sum + p_cmp[r]
    ph = p_sum.astype(BF16)
    pm = (p_sum - ph.astype(F32)).astype(BF16)
    ovt = ovt_ref[...]
    p_slc = _dot_nt(ovt, ph) + _dot_nt(ovt, pm)
    jrow = lax.broadcasted_iota(jnp.int32, (nsel, tq), 0)
    blk_t = (t0 + lax.broadcasted_iota(jnp.int32, (nsel, tq), 1)) >> SEL_SHIFT
    forced = (jrow == 0) | (jrow == blk_t) | (jrow == blk_t - 1)
    score = jnp.where(forced, FORCE_SCORE, jnp.where(jrow <= blk_t, p_slc, NEG_INF))
    rank = jnp.zeros((nsel, tq), F32)
    for j in range(nsel):
        sj = score[j:j + 1, :]
        beats = jnp.where(sj > score, 1.0, jnp.where((sj == score) & (jrow > j), 1.0, 0.0))
        rank = rank + beats
    sel_t = jnp.where(rank < float(min(SEL_TOPK, nsel)), 1.0, 0.0)
    sel = jnp.transpose(sel_t).astype(BF16)

    wlen = min(WINDOW + tq, seq)
    w0 = pl.multiple_of(jnp.maximum(t0 + tq - wlen, 0), tq)
    ktw = kwin_s[pl.ds(w0, wlen), :]
    vtw = vwin_s[pl.ds(w0, wlen), :]
    kposw = w0 + lax.broadcasted_iota(jnp.int32, (1, wlen), 1)
    bias_w = jnp.where((kposw <= tpos) & (kposw > tpos - WINDOW), 0.0, NEG_INF)
    sw = _dot_nt(qb, ktw).reshape(nr, tq, wlen) + bias_w[None]
    ew = jnp.exp2((sw - jnp.max(sw, axis=-1, keepdims=True)) * c2)
    o_win = (_dot(ew.reshape(nr * tq, wlen).astype(BF16), vtw).reshape(nr, tq, hd)
             * (1.0 / jnp.sum(ew, axis=-1, keepdims=True)))

    gs = _sigmoid(g_ref[0])
    gates = [jnp.broadcast_to(gs[:, j:j + 1], (tq, hd)) for j in range(nr * 3)]
    o_cw = [gates[3 * r] * o_cmp[r] + gates[3 * r + 2] * o_win[r] for r in range(nr)]

    kb = min(NSA_KB, seq)
    n_kb = (t0 + tq + kb - 1) // kb
    nlb = kb // LANES
    mx_s[...] = jnp.full(mx_s.shape, NEG_INF, F32)
    l_s[...] = jnp.zeros(l_s.shape, F32)
    acc_s[...] = jnp.zeros(acc_s.shape, F32)

    def lane_blocks(x):
        return [x[:, b * LANES:(b + 1) * LANES] for b in range(nlb)]

    def score_step(c, carry):
        k0 = pl.multiple_of(c * kb, kb)
        erow = lax.broadcasted_iota(jnp.int32, (nsel, kb), 0)
        ecol = lax.broadcasted_iota(jnp.int32, (nsel, kb), 1)
        expand = (((k0 + ecol) >> SEL_SHIFT) == erow).astype(BF16)
        chosen = _dot(sel, expand)
        kpos = k0 + lax.broadcasted_iota(jnp.int32, (1, kb), 1)
        bias = jnp.where((chosen > 0.5) & (kpos <= tpos), 0.0, NEG_INF)
        sb = _dot_nt(qb, ksel_s[pl.ds(k0, kb), :]).reshape(nr, tq, kb) + bias[None]
        sb = sb.reshape(nr * tq, kb)
        sbuf_s[c] = sb
        mx = mx_s[...]
        for blk in lane_blocks(sb):
            mx = jnp.maximum(mx, blk)
        mx_s[...] = mx
        return carry

    lax.fori_loop(0, n_kb, score_step, 0)
    m_sel = jnp.max(mx_s[...], axis=-1, keepdims=True)

    def prob_step(c, carry):
        k0 = pl.multiple_of(c * kb, kb)
        pt = jnp.exp2((sbuf_s[c] - m_sel) * c2)
        lsum = l_s[...]
        for blk in lane_blocks(pt):
            lsum = lsum + blk
        l_s[...] = lsum
        acc_s[...] += _dot(pt.astype(BF16), vsel_s[pl.ds(k0, kb), :])
        return carry

    lax.fori_loop(0, n_kb, prob_step, 0)
    o_sel = (acc_s[...] * (1.0 / jnp.sum(l_s[...], axis=-1, keepdims=True))).reshape(nr, tq, hd)

    for r in range(nr):
        o = o_cw[r] + gates[3 * r + 1] * o_sel[r]
        o_ref[:, r * hd:(r + 1) * hd] = o.astype(o_ref.dtype)


def _nsa_tables(seq):
    half = ROPE_DIM // 2
    inv_freq = ROPE_THETA ** (-jnp.arange(half, dtype=F32) / half)
    ang = jnp.arange(seq).astype(F32)[:, None] * inv_freq[None, :]
    cos, sin = jnp.cos(ang), jnp.sin(ang)
    rest = NSA_HEAD_DIM - ROPE_DIM
    cos_t = jnp.concatenate([cos, cos, jnp.ones((seq, rest), F32)], axis=1)
    sin_t = jnp.concatenate([-sin, sin, jnp.zeros((seq, rest), F32)], axis=1)
    n_sel = seq // SEL_BLOCK
    cmp_starts = np.arange(seq // CMP_STRIDE) * CMP_STRIDE
    sel_starts = np.arange(n_sel) * SEL_BLOCK
    overlap = np.clip(np.minimum(cmp_starts[:, None] + CMP_LEN, sel_starts[None, :] + SEL_BLOCK)
                      - np.maximum(cmp_starts[:, None], sel_starts[None, :]), 0, None).astype(np.float32) / CMP_LEN
    return cos_t, sin_t, jnp.asarray(overlap.T, BF16)


def _nsa(pb, gates, pos_k, pos_v, w1k, w2k, w1v, w2v, bsz, seq):
    t = bsz * seq
    tq, hd, g, nr = NSA_TQ, NSA_HEAD_DIM, NSA_KV_GROUPS, NSA_Q_PER_KV
    nq = seq // tq
    nblk = seq // CMP_STRIDE
    nsel = seq // SEL_BLOCK
    cos_t, sin_t, ovt = _nsa_tables(seq)
    kb = min(NSA_KB, seq)
    qblk = B_NQ // (nr * hd)
    kvb = B_NKV // hd

    def kv_spec(split):
        return pl.BlockSpec((seq, hd), lambda b, gi, i: (b, kvb + split * g + gi))

    const2 = lambda b, gi, i: (0, 0)
    return pl.pallas_call(
        functools.partial(_nsa_kernel, seq=seq),
        out_shape=jax.ShapeDtypeStruct((t, NSA_WIDTH), BF16),
        grid=(bsz, g, nq),
        in_specs=[pl.BlockSpec((tq, nr * hd), lambda b, gi, i: (b * nq + i, qblk + gi))]
                 + [kv_spec(sp) for sp in range(6)]
                 + [pl.BlockSpec((1, tq, nr * 3), lambda b, gi, i: (gi, b * nq + i, 0)),
                    pl.BlockSpec((seq, hd), const2),
                    pl.BlockSpec((seq, hd), const2),
                    pl.BlockSpec((CMP_LEN, hd), const2),
                    pl.BlockSpec((CMP_LEN, hd), const2),
                    pl.BlockSpec((CMP_LEN * hd, CMP_HIDDEN), const2),
                    pl.BlockSpec((CMP_HIDDEN, hd), const2),
                    pl.BlockSpec((CMP_LEN * hd, CMP_HIDDEN), const2),
                    pl.BlockSpec((CMP_HIDDEN, hd), const2),
                    pl.BlockSpec((nsel, nblk), const2)],
        out_specs=pl.BlockSpec((tq, nr * hd), lambda b, gi, i: (b * nq + i, gi)),
        scratch_shapes=[pltpu.VMEM((nblk, hd), BF16), pltpu.VMEM((nblk, hd), BF16),
                        pltpu.VMEM((seq, hd), BF16), pltpu.VMEM((seq, hd), BF16),
                        pltpu.VMEM((seq, hd), BF16), pltpu.VMEM((seq, hd), BF16),
                        pltpu.VMEM((seq, hd), F32),
                        pltpu.VMEM((seq // kb, nr * tq, kb), F32),
                        pltpu.VMEM((nr * tq, LANES), F32),
                        pltpu.VMEM((nr * tq, LANES), F32),
                        pltpu.VMEM((nr * tq, hd), F32)],
        compiler_params=_params(("parallel", "parallel", "arbitrary")),
        name="nsa_mixer",
    )(pb, pb, pb, pb, pb, pb, pb, gates, cos_t, sin_t, pos_k, pos_v, w1k, w2k, w1v, w2v, ovt)


def _in_proj(x2d, gain, w_in_stack, layer):
    t = x2d.shape[0]
    tm = min(PROJ_TILE_M, t)
    h = _rms_cast(x2d, gain, tm)
    pa = _proj(h, w_in_stack, layer, 0, PA_COLS, tm, PROJ_TILE_N)
    pb = _proj(h, w_in_stack, layer, PB_START, PB_COLS, tm, PROJ_TILE_N)
    pc = _proj(h, w_in_stack, layer, PC_START, PC_COLS, tm, PROJ_TILE_N)
    ps = _proj_small(h, w_in_stack, layer, tm)
    return pa, pb, pc, ps


def _small_views(ps):
    t = ps.shape[0]
    nsa_g = ps[:, S_NG:S_NG + 3 * NSA_HEADS]
    nsa_g = nsa_g.reshape(t, NSA_KV_GROUPS, NSA_Q_PER_KV * 3).transpose(1, 0, 2)
    dt = ps[:, S_DT:S_DT + SSD_HEADS].reshape(t, SSD_GROUPS, SSD_HEADS_PER_GROUP)
    sub = min(SSD_SUB, t)
    dt_r = dt.reshape(t // sub, sub, SSD_GROUPS, SSD_HEADS_PER_GROUP).transpose(2, 0, 3, 1)
    return nsa_g, jnp.tile(dt.transpose(1, 0, 2), (1, 1, 3)), dt_r


def _layer(x2d, bsz, seq, layer, norm_mix, w_in, gla_gate_w2, gla_gate_b, gla_out_norm,
           nsa_cmp_pos_k, nsa_cmp_pos_v, nsa_cmp_k_w1, nsa_cmp_k_w2, nsa_cmp_v_w1, nsa_cmp_v_w2,
           ssd_conv_w, ssd_conv_b, ssd_dt_bias, ssd_a_log, ssd_d, ssd_out_norm,
           w_branch, w_out, norm_ffn, w_ffn_gate, w_ffn_up, w_ffn_down, norm_final, final_norm):
    t = bsz * seq
    pa, pb, pc, ps = _in_proj(x2d, norm_mix, w_in, layer)
    nsa_g, dt_c, dt_r = _small_views(ps)

    w2pad = jnp.zeros((LANES, GLA_DK), BF16).at[S_GLOW:S_GLOW + GLA_LOWRANK].set(
        gla_gate_w2.astype(BF16))
    o_gla = _gla(pa, pb, ps, w2pad, gla_gate_b, gla_out_norm, bsz, seq, min(256, seq))
    o_nsa = _nsa(pb, nsa_g, nsa_cmp_pos_k, nsa_cmp_pos_v,
                 nsa_cmp_k_w1.astype(BF16), nsa_cmp_k_w2.astype(BF16),
                 nsa_cmp_v_w1.astype(BF16), nsa_cmp_v_w2.astype(BF16), bsz, seq)
    o_ssd = _ssd(pc, dt_c, dt_r, ssd_conv_w, ssd_conv_b, ssd_dt_bias, ssd_a_log, ssd_d,
                 ssd_out_norm, bsz, seq, min(512, seq))

    merged = _merge(o_gla, o_nsa, o_ssd, pa, w_branch.astype(BF16), min(512, t), 512)
    tm = min(PROJ_TILE_M, t)
    x2d = _proj(merged, w_out, layer, 0, D_MODEL, tm, PROJ_TILE_N, residual=x2d, name="out_proj")
    return _ffn(x2d, norm_ffn, norm_final, w_ffn_gate.astype(BF16), w_ffn_up.astype(BF16),
                w_ffn_down.astype(BF16), min(512, t), 512, final_norm)


def kernel(x, norm_mix, w_in, gla_gate_w2, gla_gate_b, gla_out_norm, nsa_cmp_pos_k, nsa_cmp_pos_v,
           nsa_cmp_k_w1, nsa_cmp_k_w2, nsa_cmp_v_w1, nsa_cmp_v_w2, ssd_conv_w, ssd_conv_b,
           ssd_dt_bias, ssd_a_log, ssd_d, ssd_out_norm, w_branch, w_out, norm_ffn, w_ffn_gate,
           w_ffn_up, w_ffn_down, norm_final):
    bsz, seq, d = x.shape
    depth = norm_mix.shape[0]
    x2d = x.reshape(bsz * seq, d)
    for l in range(depth):
        x2d = _layer(x2d, bsz, seq, l, norm_mix[l], w_in, gla_gate_w2[l], gla_gate_b[l],
                     gla_out_norm[l], nsa_cmp_pos_k[l], nsa_cmp_pos_v[l], nsa_cmp_k_w1[l],
                     nsa_cmp_k_w2[l], nsa_cmp_v_w1[l], nsa_cmp_v_w2[l], ssd_conv_w[l],
                     ssd_conv_b[l], ssd_dt_bias[l], ssd_a_log[l], ssd_d[l], ssd_out_norm[l],
                     w_branch[l], w_out, norm_ffn[l], w_ffn_gate[l], w_ffn_up[l],
                     w_ffn_down[l], norm_final, l == depth - 1)
    return x2d.reshape(bsz, seq, d)
```

```python
import functools

import numpy as np
import jax
import jax.numpy as jnp
from jax import lax
from jax.experimental import pallas as pl
from jax.experimental.pallas import tpu as pltpu

F32 = jnp.float32
BF16 = jnp.bfloat16

D_MODEL = 2048
RMS_EPS = 1e-6
LOG2_E = 1.4426950408889634
NEG_INF = -1e30
FORCE_SCORE = 1e9

GLA_HEADS = 4
GLA_DK = D_MODEL // 2
GLA_DV = D_MODEL
GLA_HK = GLA_DK // GLA_HEADS
GLA_HV = GLA_DV // GLA_HEADS
GLA_LOWRANK = 16
GLA_GATE_NORMALIZER = 16.0

NSA_HEAD_DIM = 128
NSA_HEAD_SHIFT = 7
NSA_HEADS = D_MODEL // NSA_HEAD_DIM
NSA_KV_GROUPS = 4
NSA_Q_PER_KV = NSA_HEADS // NSA_KV_GROUPS
NSA_WIDTH = NSA_HEADS * NSA_HEAD_DIM
NSA_KV_WIDTH = NSA_KV_GROUPS * NSA_HEAD_DIM
CMP_LEN = 32
CMP_STRIDE = 16
CMP_HIDDEN = 2 * NSA_HEAD_DIM
SEL_BLOCK = 64
SEL_SHIFT = 6
SEL_TOPK = 16
WINDOW = 512
ROPE_THETA = 500000.0
ROPE_DIM = NSA_HEAD_DIM // 4

SSD_D_INNER = 2 * D_MODEL
SSD_HEAD_DIM = 64
SSD_HEAD_SHIFT = 6
SSD_HEADS = SSD_D_INNER // SSD_HEAD_DIM
SSD_GROUPS = 8
SSD_HEADS_PER_GROUP = SSD_HEADS // SSD_GROUPS
SSD_D_STATE = 128
SSD_CONV = 4
SSD_GROUP_WIDTH = SSD_HEADS_PER_GROUP * SSD_HEAD_DIM
SSD_CONV_DIM = SSD_D_INNER + 2 * SSD_GROUPS * SSD_D_STATE

D_FF = ((8 * D_MODEL + 3 * 256 - 1) // (3 * 256)) * 256

LANES = 128
V7X_VMEM_LIMIT_BYTES = 56 * 1024 * 1024

_REF_SIZES = (3 * D_MODEL, GLA_DK, GLA_DK, GLA_DV, GLA_LOWRANK, GLA_DV,
              NSA_WIDTH, 6 * NSA_KV_WIDTH, 3 * NSA_HEADS,
              SSD_D_INNER, SSD_CONV_DIM, SSD_HEADS)
_REF_OFF = tuple(int(v) for v in np.concatenate([[0], np.cumsum(_REF_SIZES)]))
(_R_GATE, _R_GQ, _R_GK, _R_GV, _R_GLOW, _R_GR, _R_NQ, _R_NKV, _R_NG,
 _R_SZ, _R_SXBC, _R_SDT) = _REF_OFF[:-1]
PA_COLS = _R_GLOW
PB_START, PB_COLS = _R_GR, _R_NG - _R_GR
PC_START, PC_COLS = _R_SZ, _R_SDT - _R_SZ
A_GATE, A_GQ, A_GK, A_GV = _R_GATE, _R_GQ, _R_GK, _R_GV
B_GR, B_NQ, B_NKV = 0, _R_NQ - _R_GR, _R_NKV - _R_GR
C_SZ, C_SXBC = 0, _R_SXBC - _R_SZ
S_BLOCKS = (_R_GLOW // LANES, _R_NG // LANES, _R_SDT // LANES)
S_GLOW = _R_GLOW % LANES
S_NG = LANES + _R_NG % LANES
S_DT = 2 * LANES + _R_SDT % LANES
PROJ_TILE_M = 1024
PROJ_TILE_N = 1024
PROJ_CAST_ROWS = 256
MERGE_TILE_N = 512
FFN_TILE_F = 512


def _params(sem, vmem=V7X_VMEM_LIMIT_BYTES):
    return pltpu.CompilerParams(dimension_semantics=sem, vmem_limit_bytes=vmem)


def _sigmoid(x):
    return 1.0 / (1.0 + jnp.exp(-x))


def _silu(x):
    return x * _sigmoid(x)


def _softplus(x):
    return jnp.maximum(x, 0.0) + jnp.log(1.0 + jnp.exp(-jnp.abs(x)))


def _split3(x):
    hi = x.astype(BF16)
    r1 = x - hi.astype(F32)
    mid = r1.astype(BF16)
    lo = (r1 - mid.astype(F32)).astype(BF16)
    return hi, mid, lo


def _dot(a, b):
    return jnp.dot(a, b, preferred_element_type=F32)


def _dot_nt(a, b):
    return lax.dot_general(a, b, (((1,), (1,)), ((), ())), preferred_element_type=F32)


def _dot_tn(a, b):
    return lax.dot_general(a, b, (((0,), (0,)), ((), ())), preferred_element_type=F32)


def _sel_dot(sel, x):
    hi, mid, lo = _split3(x)
    return _dot(sel, hi) + _dot(sel, mid) + _dot(sel, lo)


def _sel_dot_left(x, sel):
    hi, mid, lo = _split3(x)
    return _dot(hi, sel) + _dot(mid, sel) + _dot(lo, sel)


def _rms_rows(x, gain):
    ms = jnp.mean(x * x, axis=-1, keepdims=True)
    return x * lax.rsqrt(ms + RMS_EPS) * gain


def _rms_cast_kernel(x_ref, g_ref, o_ref):
    o_ref[...] = _rms_rows(x_ref[...], g_ref[...]).astype(o_ref.dtype)


def _rms_cast(x2d, gain, tm):
    t, d = x2d.shape
    return pl.pallas_call(
        _rms_cast_kernel,
        out_shape=jax.ShapeDtypeStruct((t, d), BF16),
        grid=(t // tm,),
        in_specs=[pl.BlockSpec((tm, d), lambda i: (i, 0)),
                  pl.BlockSpec((1, d), lambda i: (0, 0))],
        out_specs=pl.BlockSpec((tm, d), lambda i: (i, 0)),
        compiler_params=_params(("parallel",)),
        name="rms_cast",
    )(x2d, gain.reshape(1, d))


def _proj_kernel(*refs, shift, residual):
    refs = list(refs)
    h_ref, wa_ref = refs[:2]
    wb_ref = refs[2] if shift else None
    res_ref = refs[-3] if residual else None
    o_ref, w_s = refs[-2:]
    tn = w_s.shape[1]

    @pl.when(pl.program_id(1) == 0)
    def _():
        for r0 in range(0, w_s.shape[0], PROJ_CAST_ROWS):
            rows = pl.ds(r0, PROJ_CAST_ROWS)
            w = wa_ref[rows, :]
            if shift:
                wcat = jnp.concatenate([w, wb_ref[rows, :]], axis=1)
                w = pltpu.roll(wcat, tn + LANES - shift, axis=1)[:, :tn]
            w_s[rows, :] = w.astype(BF16)

    y = _dot(h_ref[...], w_s[...])
    o_ref[...] = y + res_ref[...] if residual else y


def _proj(h, w_stack, layer, start, ncols, tm, tn, residual=None, name="in_proj"):
    t, d = h.shape
    shift = start % LANES
    base = (start - shift) // tn
    assert (start - shift) % tn == 0 and ncols % tn == 0
    in_specs = [pl.BlockSpec((tm, d), lambda j, i: (i, 0)),
                pl.BlockSpec((None, d, tn), lambda j, i: (layer, 0, base + j))]
    args = [h, w_stack]
    if shift:
        nxt = tn // LANES
        in_specs.append(pl.BlockSpec((None, d, LANES), lambda j, i: (layer, 0, (base + j + 1) * nxt)))
        args.append(w_stack)
    if residual is not None:
        in_specs.append(pl.BlockSpec((tm, tn), lambda j, i: (i, j)))
        args.append(residual)
    return pl.pallas_call(
        functools.partial(_proj_kernel, shift=shift, residual=residual is not None),
        out_shape=jax.ShapeDtypeStruct((t, ncols), F32),
        grid=(ncols // tn, t // tm),
        in_specs=in_specs,
        out_specs=pl.BlockSpec((tm, tn), lambda j, i: (i, j)),
        scratch_shapes=[pltpu.VMEM((d, tn), BF16)],
        compiler_params=_params(("parallel", "arbitrary")),
        name=name,
    )(*args)


def _proj_small_kernel(h_ref, w0_ref, w1_ref, w2_ref, o_ref, w_s):
    @pl.when(pl.program_id(0) == 0)
    def _():
        for b, w_ref in enumerate((w0_ref, w1_ref, w2_ref)):
            w_s[:, b * LANES:(b + 1) * LANES] = w_ref[...].astype(BF16)

    o_ref[...] = _dot(h_ref[...], w_s[...])


def _proj_small(h, w_stack, layer, tm):
    t, d = h.shape
    nb = len(S_BLOCKS)
    wspecs = [pl.BlockSpec((None, d, LANES), functools.partial(lambda i, blk: (layer, 0, blk), blk=blk))
              for blk in S_BLOCKS]
    return pl.pallas_call(
        _proj_small_kernel,
        out_shape=jax.ShapeDtypeStruct((t, nb * LANES), F32),
        grid=(t // tm,),
        in_specs=[pl.BlockSpec((tm, d), lambda i: (i, 0))] + wspecs,
        out_specs=pl.BlockSpec((tm, nb * LANES), lambda i: (i, 0)),
        scratch_shapes=[pltpu.VMEM((d, nb * LANES), BF16)],
        compiler_params=_params(("arbitrary",)),
        name="in_proj_small",
    )(h, w_stack, w_stack, w_stack)


def _merge_kernel(og_ref, on_ref, os_ref, g0_ref, g1_ref, g2_ref, wb0_ref, wb1_ref, wb2_ref, o_ref):
    m = (_sigmoid(g0_ref[...]) * _dot(og_ref[...], wb0_ref[...])
         + _sigmoid(g1_ref[...]) * _dot(on_ref[...], wb1_ref[...])
         + _sigmoid(g2_ref[...]) * _dot(os_ref[...], wb2_ref[...]))
    o_ref[...] = m.astype(o_ref.dtype)


def _column_tiles(w, tn):
    k, n = w.shape
    return w.reshape(k, n // tn, tn).transpose(1, 0, 2)


def _merge(o_gla, o_nsa, o_ssd, pa, wb, tm):
    t = o_gla.shape[0]
    d = D_MODEL
    nj, _, tn = wb.shape
    gate_blk0 = A_GATE // tn
    return pl.pallas_call(
        _merge_kernel,
        out_shape=jax.ShapeDtypeStruct((t, d), BF16),
        grid=(t // tm, nj),
        in_specs=[pl.BlockSpec((tm, GLA_DV), lambda i, j: (i, 0)),
                  pl.BlockSpec((tm, NSA_WIDTH), lambda i, j: (i, 0)),
                  pl.BlockSpec((tm, SSD_D_INNER), lambda i, j: (i, 0)),
                  pl.BlockSpec((tm, tn), lambda i, j: (i, gate_blk0 + j)),
                  pl.BlockSpec((tm, tn), lambda i, j: (i, gate_blk0 + nj + j)),
                  pl.BlockSpec((tm, tn), lambda i, j: (i, gate_blk0 + 2 * nj + j)),
                  pl.BlockSpec((None, GLA_DV, tn), lambda i, j: (j, 0, 0)),
                  pl.BlockSpec((None, NSA_WIDTH, tn), lambda i, j: (j, 1, 0)),
                  pl.BlockSpec((None, SSD_D_INNER, tn), lambda i, j: (j, 1, 0))],
        out_specs=pl.BlockSpec((tm, tn), lambda i, j: (i, j)),
        compiler_params=_params(("parallel", "arbitrary")),
        name="branch_merge",
    )(o_gla, o_nsa, o_ssd, pa, pa, pa, wb, wb, wb)


def _ffn_kernel(x_ref, g_ref, gf_ref, wg_ref, wu_ref, wd_ref, o_ref, h_ref, a_ref, *, final_norm):
    j = pl.program_id(1)
    last = pl.num_programs(1) - 1

    def hidden():
        h = h_ref[...]
        return (_silu(_dot(h, wg_ref[...])) * _dot(h, wu_ref[...])).astype(BF16)

    @pl.when(j == 0)
    def _():
        x = x_ref[...]
        h_ref[...] = _rms_rows(x, g_ref[...]).astype(BF16)
        o_ref[...] = x
        a_ref[...] = hidden()

    @pl.when((j > 0) & (j < last))
    def _():
        a_new = hidden()
        o_ref[...] += _dot(a_ref[...], wd_ref[...])
        a_ref[...] = a_new

    @pl.when(j == last)
    def _():
        y = o_ref[...] + _dot(a_ref[...], wd_ref[...])
        o_ref[...] = _rms_rows(y, gf_ref[...]) if final_norm else y


def _ffn(x2d, gain, gain_final, wg, wu, wd, tm, final_norm):
    t, d = x2d.shape
    nf, _, tf = wg.shape
    return pl.pallas_call(
        functools.partial(_ffn_kernel, final_norm=final_norm),
        out_shape=jax.ShapeDtypeStruct((t, d), F32),
        grid=(t // tm, nf + 1),
        in_specs=[pl.BlockSpec((tm, d), lambda i, j: (i, 0)),
                  pl.BlockSpec((1, d), lambda i, j: (0, 0)),
                  pl.BlockSpec((1, d), lambda i, j: (0, 0)),
                  pl.BlockSpec((None, d, tf), lambda i, j: (jnp.minimum(j, nf - 1), 0, 0)),
                  pl.BlockSpec((None, d, tf), lambda i, j: (jnp.minimum(j, nf - 1), 0, 0)),
                  pl.BlockSpec((tf, d), lambda i, j: (jnp.maximum(j - 1, 0), 0))],
        out_specs=pl.BlockSpec((tm, d), lambda i, j: (i, 0)),
        scratch_shapes=[pltpu.VMEM((tm, d), BF16), pltpu.VMEM((tm, tf), BF16)],
        compiler_params=_params(("parallel", "arbitrary")),
        name="ffn",
    )(x2d, gain.reshape(1, d), gain_final.reshape(1, d), wg, wu, wd)


GLA_CHUNK = 64
GLA_SUB = 16
GLA_EXP_CAP = 88.0
GLA_HEADS_PER_STEP = 2


def _gla_kernel(q_ref, k_ref, v_ref, r_ref, sm_ref, w2_ref, gb_ref, ng_ref, o_ref, st_ref,
                *, chunks_per_tile):
    @pl.when(pl.program_id(2) == 0)
    def _():
        st_ref[...] = jnp.zeros_like(st_ref)

    c_rows = GLA_CHUNK
    row = lax.broadcasted_iota(jnp.int32, (c_rows, c_rows), 0)
    col = lax.broadcasted_iota(jnp.int32, (c_rows, c_rows), 1)
    tri = (col <= row).astype(BF16)
    srow = lax.broadcasted_iota(jnp.int32, (GLA_SUB, c_rows), 0)
    scol = lax.broadcasted_iota(jnp.int32, (GLA_SUB, c_rows), 1)
    scale = GLA_HK ** -0.5

    smb = sm_ref[...].astype(BF16)
    work = []
    for h in range(GLA_HEADS_PER_STEP):
        kcols = slice(h * GLA_HK, (h + 1) * GLA_HK)
        vcols = slice(h * GLA_HV, (h + 1) * GLA_HV)
        ga = _dot(smb, w2_ref[:, kcols]) + gb_ref[:, kcols]
        log_a_all = -_softplus(-ga) * (1.0 / GLA_GATE_NORMALIZER)
        for c in range(chunks_per_tile):
            rows = slice(c * c_rows, (c + 1) * c_rows)
            bc = _sel_dot(tri, log_a_all[rows])
            q = q_ref[rows, kcols] * scale
            k = k_ref[rows, kcols]
            vb = v_ref[rows, vcols].astype(BF16)
            parts = []
            for i in range(c_rows // GLA_SUB):
                lo = i * GLA_SUB
                ref_pt = bc[lo - 1:lo, :] if i > 0 else jnp.zeros((1, GLA_HK), F32)
                kt = (k * jnp.exp(jnp.minimum(ref_pt - bc, GLA_EXP_CAP))).astype(BF16)
                qi = (q[lo:lo + GLA_SUB, :] * jnp.exp(bc[lo:lo + GLA_SUB, :] - ref_pt)).astype(BF16)
                sc = _dot_nt(qi, kt)
                parts.append(jnp.where(scol <= srow + lo, sc, 0.0))
            intra = _dot(jnp.concatenate(parts, axis=0).astype(BF16), vb)
            b_last = bc[c_rows - 1:c_rows, :]
            khat = (k * jnp.exp(b_last - bc)).astype(BF16)
            work.append((h, rows, vcols, (q * jnp.exp(bc)).astype(BF16), intra,
                         jnp.exp(b_last), _dot_tn(vb, khat)))

    states = [st_ref[h] for h in range(GLA_HEADS_PER_STEP)]
    for h, rows, vcols, qe, intra, decay, incr in work:
        out = _dot_nt(qe, states[h].astype(BF16)) + intra
        states[h] = states[h] * decay + incr
        y = _rms_rows(out, ng_ref[...]) * _silu(r_ref[rows, vcols])
        o_ref[rows, vcols] = y.astype(o_ref.dtype)
    for h in range(GLA_HEADS_PER_STEP):
        st_ref[h] = states[h]


def _gla(pa, pb, ps, w2pad, gate_b, norm_g, bsz, seq, tile):
    t = bsz * seq
    nt = seq // tile
    hps = GLA_HEADS_PER_STEP
    kw, vw = hps * GLA_HK, hps * GLA_HV
    qb, kb = A_GQ // kw, A_GK // kw
    vb, rb = A_GV // vw, B_GR // vw
    return pl.pallas_call(
        functools.partial(_gla_kernel, chunks_per_tile=tile // GLA_CHUNK),
        out_shape=jax.ShapeDtypeStruct((t, GLA_DV), BF16),
        grid=(bsz, GLA_HEADS // hps, nt),
        in_specs=[pl.BlockSpec((tile, kw), lambda b, h, c: (b * nt + c, qb + h)),
                  pl.BlockSpec((tile, kw), lambda b, h, c: (b * nt + c, kb + h)),
                  pl.BlockSpec((tile, vw), lambda b, h, c: (b * nt + c, vb + h)),
                  pl.BlockSpec((tile, vw), lambda b, h, c: (b * nt + c, rb + h)),
                  pl.BlockSpec((tile, LANES), lambda b, h, c: (b * nt + c, 0)),
                  pl.BlockSpec((LANES, kw), lambda b, h, c: (0, h)),
                  pl.BlockSpec((1, kw), lambda b, h, c: (0, h)),
                  pl.BlockSpec((1, GLA_HV), lambda b, h, c: (0, 0))],
        out_specs=pl.BlockSpec((tile, vw), lambda b, h, c: (b * nt + c, h)),
        scratch_shapes=[pltpu.VMEM((hps, GLA_HV, GLA_HK), F32)],
        compiler_params=_params(("parallel", "parallel", "arbitrary")),
        name="gla_mixer",
    )(pa, pa, pa, pb, ps, w2pad, gate_b.reshape(1, GLA_DK), norm_g.reshape(1, GLA_HV))


SSD_SUB = 128
SSD_HIST = 8


def _causal_conv(e_ref, src_ref, w_ref, b_ref):
    n = src_ref.shape[0]
    e_ref[pl.ds(SSD_HIST, n), :] = src_ref[...]
    acc = src_ref[...] * w_ref[SSD_CONV - 1:SSD_CONV, :] + b_ref[...]
    for s in range(1, SSD_CONV):
        acc = acc + e_ref[pl.ds(SSD_HIST - s, n), :] * w_ref[SSD_CONV - 1 - s:SSD_CONV - s, :]
    e_ref[pl.ds(0, SSD_HIST), :] = src_ref[pl.ds(n - SSD_HIST, SSD_HIST), :]
    return _silu(acc)


def _ssd_kernel(xs_ref, bm_ref, cm_ref, z_ref, dtc_ref, dtr_ref,
                wx_ref, wb_ref, wc_ref, bx_ref, bb_ref, bc_ref,
                dbc_ref, dbr_ref, alc_ref, alr_ref, dsk_ref, ng_ref,
                o_ref, st_ref, ex_ref, eb_ref, ec_ref):
    n = xs_ref.shape[0]
    sub = min(SSD_SUB, n)
    hpg, hd = SSD_HEADS_PER_GROUP, SSD_HEAD_DIM

    @pl.when(pl.program_id(2) == 0)
    def _():
        st_ref[...] = jnp.zeros_like(st_ref)
        ex_ref[pl.ds(0, SSD_HIST), :] = jnp.zeros((SSD_HIST, ex_ref.shape[1]), F32)
        eb_ref[pl.ds(0, SSD_HIST), :] = jnp.zeros((SSD_HIST, eb_ref.shape[1]), F32)
        ec_ref[pl.ds(0, SSD_HIST), :] = jnp.zeros((SSD_HIST, ec_ref.shape[1]), F32)

    xs = _causal_conv(ex_ref, xs_ref, wx_ref, bx_ref)
    bmb = _causal_conv(eb_ref, bm_ref, wb_ref, bb_ref).astype(BF16)
    cmb = _causal_conv(ec_ref, cm_ref, wc_ref, bc_ref).astype(BF16)

    w3 = 3 * hpg
    krow = lax.broadcasted_iota(jnp.int32, (w3, hpg * hd), 0)
    kcol = lax.broadcasted_iota(jnp.int32, (w3, hpg * hd), 1) >> SSD_HEAD_SHIFT
    expand3 = ((krow == kcol) | (krow == kcol + hpg) | (krow == kcol + 2 * hpg)).astype(BF16)

    def expand_heads(x):
        hi, mid, lo = _split3(x)
        klane = lax.broadcasted_iota(jnp.int32, x.shape, 1)
        return _dot(jnp.where(klane < hpg, hi, jnp.where(klane < 2 * hpg, mid, lo)), expand3)

    dt_c = _softplus(dtc_ref[0] + dbc_ref[0])
    a_c = -jnp.exp(alc_ref[0])
    a_r = -jnp.exp(alr_ref[0])
    xdt = xs * expand_heads(dt_c)

    row = lax.broadcasted_iota(jnp.int32, (sub, sub), 0)
    col = lax.broadcasted_iota(jnp.int32, (sub, sub), 1)
    tri = (col <= row).astype(BF16)
    triu = (row <= col).astype(BF16)
    causal_bias = jnp.where(col <= row, 0.0, NEG_INF)
    lane = lax.broadcasted_iota(jnp.int32, (sub, 2 * hd), 1)

    for si in range(n // sub):
        rows = slice(si * sub, (si + 1) * sub)
        acum_c = _sel_dot(tri, dt_c[rows] * a_c)
        adt_r = _softplus(dtr_ref[0, si] + dbr_ref[0]) * a_r
        acum_r = _sel_dot_left(adt_r, triu)
        a_last = acum_c[sub - 1:sub, :]
        eac_x = expand_heads(jnp.exp(acum_c))
        dec_x = expand_heads(jnp.exp(a_last - acum_c))
        sdec_x = expand_heads(jnp.broadcast_to(jnp.exp(a_last), (8, w3)))[0:1, :]
        ac2 = acum_c * LOG2_E
        ar2 = acum_r * LOG2_E

        xd, cs, bs = xdt[rows], cmb[rows], bmb[rows]
        cb = _dot_nt(cs, bs)
        pair_out = []
        for pr in range(hpg // 2):
            xp = xd[:, pr * 2 * hd:(pr + 1) * 2 * hd]
            acc = None
            for half in range(2):
                r = 2 * pr + half
                seg = jnp.exp2(ac2[:, r:r + 1] - ar2[r:r + 1, :] + causal_bias)
                rhs = jnp.where((lane >> SSD_HEAD_SHIFT) == half, xp, 0.0).astype(BF16)
                term = _dot((cb * seg).astype(BF16), rhs)
                acc = term if acc is None else acc + term
            pair_out.append(acc)
        y_diag = jnp.concatenate(pair_out, axis=1)

        st = st_ref[...]
        y_off = _dot(cs, st.astype(BF16)) * eac_x
        st_ref[...] = st * sdec_x + _dot_tn(bs, (xd * dec_x).astype(BF16))

        y = (y_diag + y_off + xs[rows] * dsk_ref[0]) * _silu(z_ref[rows, :])
        o_ref[rows, :] = _rms_rows(y, ng_ref[...]).astype(o_ref.dtype)


def _ssd(pc, dt_c, dt_r, conv_w, conv_b, dt_bias, a_log, d_skip, norm_g, bsz, seq, chunk):
    t = bsz * seq
    nc = seq // chunk
    sub = min(SSD_SUB, chunk)
    g, hpg, gw, ns = SSD_GROUPS, SSD_HEADS_PER_GROUP, SSD_GROUP_WIDTH, SSD_D_STATE
    xsb = C_SXBC // gw
    bmb = (C_SXBC + SSD_D_INNER) // ns
    cmb = (C_SXBC + SSD_D_INNER + g * ns) // ns
    zb = C_SZ // gw
    cw_b = SSD_D_INNER // ns
    cw_c = (SSD_D_INNER + g * ns) // ns
    conv_b2 = conv_b.reshape(1, SSD_CONV_DIM)
    dbc = jnp.tile(dt_bias.reshape(g, 1, hpg), (1, 1, 3))
    dbr = dt_bias.reshape(g, hpg, 1)
    alc = jnp.tile(a_log.reshape(g, 1, hpg), (1, 1, 3))
    alr = a_log.reshape(g, hpg, 1)
    dsk = jnp.repeat(d_skip, SSD_HEAD_DIM).reshape(g, 1, gw)
    ng = norm_g.reshape(1, SSD_D_INNER)
    return pl.pallas_call(
        _ssd_kernel,
        out_shape=jax.ShapeDtypeStruct((t, SSD_D_INNER), BF16),
        grid=(bsz, g, nc),
        in_specs=[pl.BlockSpec((chunk, gw), lambda b, gi, c: (b * nc + c, xsb + gi)),
                  pl.BlockSpec((chunk, ns), lambda b, gi, c: (b * nc + c, bmb + gi)),
                  pl.BlockSpec((chunk, ns), lambda b, gi, c: (b * nc + c, cmb + gi)),
                  pl.BlockSpec((chunk, gw), lambda b, gi, c: (b * nc + c, zb + gi)),
                  pl.BlockSpec((1, chunk, 3 * hpg), lambda b, gi, c: (gi, b * nc + c, 0)),
                  pl.BlockSpec((1, chunk // sub, hpg, sub), lambda b, gi, c: (gi, b * nc + c, 0, 0)),
                  pl.BlockSpec((SSD_CONV, gw), lambda b, gi, c: (0, gi)),
                  pl.BlockSpec((SSD_CONV, ns), lambda b, gi, c: (0, cw_b + gi)),
                  pl.BlockSpec((SSD_CONV, ns), lambda b, gi, c: (0, cw_c + gi)),
                  pl.BlockSpec((1, gw), lambda b, gi, c: (0, gi)),
                  pl.BlockSpec((1, ns), lambda b, gi, c: (0, cw_b + gi)),
                  pl.BlockSpec((1, ns), lambda b, gi, c: (0, cw_c + gi)),
                  pl.BlockSpec((1, 1, 3 * hpg), lambda b, gi, c: (gi, 0, 0)),
                  pl.BlockSpec((1, hpg, 1), lambda b, gi, c: (gi, 0, 0)),
                  pl.BlockSpec((1, 1, 3 * hpg), lambda b, gi, c: (gi, 0, 0)),
                  pl.BlockSpec((1, hpg, 1), lambda b, gi, c: (gi, 0, 0)),
                  pl.BlockSpec((1, 1, gw), lambda b, gi, c: (gi, 0, 0)),
                  pl.BlockSpec((1, gw), lambda b, gi, c: (0, gi))],
        out_specs=pl.BlockSpec((chunk, gw), lambda b, gi, c: (b * nc + c, gi)),
        scratch_shapes=[pltpu.VMEM((ns, gw), F32),
                        pltpu.VMEM((SSD_HIST + chunk, gw), F32),
                        pltpu.VMEM((SSD_HIST + chunk, ns), F32),
                        pltpu.VMEM((SSD_HIST + chunk, ns), F32)],
        compiler_params=_params(("parallel", "parallel", "arbitrary")),
        name="ssd_mixer",
    )(pc, pc, pc, pc, dt_c, dt_r, conv_w, conv_w, conv_w, conv_b2, conv_b2, conv_b2,
      dbc, dbr, alc, alr, dsk, ng)


NSA_TQ = 256
NSA_KB = 512


def _rope(x, cos, sin):
    lane = lax.broadcasted_iota(jnp.int32, x.shape, 1)
    half = ROPE_DIM // 2
    swapped = jnp.where(lane < half, pltpu.roll(x, LANES - half, axis=1), pltpu.roll(x, half, axis=1))
    return x * cos + swapped * sin


def _compress(src_ref, pe_ref, w1_ref, w2_ref, nblk):
    hd = NSA_HEAD_DIM
    a0 = jnp.zeros((nblk, CMP_HIDDEN), F32)
    a1 = jnp.zeros((nblk, CMP_HIDDEN), F32)
    for c in range(CMP_STRIDE):
        xc = src_ref[pl.ds(c, nblk, stride=CMP_STRIDE), :]
        a0 = a0 + _dot((xc + pe_ref[c:c + 1, :]).astype(BF16), w1_ref[c * hd:(c + 1) * hd, :])
        c1 = CMP_STRIDE + c
        a1 = a1 + _dot((xc + pe_ref[c1:c1 + 1, :]).astype(BF16), w1_ref[c1 * hd:(c1 + 1) * hd, :])
    hid = a0 + pltpu.roll(a1, nblk - 1, axis=0)
    return _dot(_silu(hid).astype(BF16), w2_ref[...])


def _nsa_kernel(q_ref, kc_ref, vc_ref, ks_ref, vs_ref, kw_ref, vw_ref, g_ref,
                cos_ref, sin_ref, pek_ref, pev_ref, w1k_ref, w2k_ref, w1v_ref, w2v_ref, ovt_ref,
                o_ref, kcmp_s, vcmp_s, ksel_s, vsel_s, kwin_s, vwin_s, krope_s,
                sbuf_s, mx_s, l_s, acc_s, *, seq):
    tq, hd, nr = NSA_TQ, NSA_HEAD_DIM, NSA_Q_PER_KV
    i = pl.program_id(2)
    nblk = seq // CMP_STRIDE
    nsel = seq // SEL_BLOCK
    scale = hd ** -0.5

    @pl.when(i == 0)
    def _():
        cos, sin = cos_ref[...], sin_ref[...]
        krope_s[...] = _rope(kc_ref[...], cos, sin)
        ksel_s[...] = _rope(ks_ref[...], cos, sin).astype(BF16)
        kwin_s[...] = _rope(kw_ref[...], cos, sin).astype(BF16)
        vsel_s[...] = vs_ref[...].astype(BF16)
        vwin_s[...] = vw_ref[...].astype(BF16)
        kcmp_s[...] = _compress(krope_s, pek_ref, w1k_ref, w2k_ref, nblk).astype(BF16)
        vcmp_s[...] = _compress(vc_ref, pev_ref, w1v_ref, w2v_ref, nblk).astype(BF16)

    t0 = pl.multiple_of(i * tq, tq)
    cos = cos_ref[pl.ds(t0, tq), :]
    sin = sin_ref[pl.ds(t0, tq), :]
    qs = [_rope(q_ref[:, r * hd:(r + 1) * hd], cos, sin) for r in range(nr)]
    qb = jnp.concatenate(qs, axis=0).astype(BF16)
    c2 = scale * LOG2_E
    tpos = t0 + lax.broadcasted_iota(jnp.int32, (tq, 1), 0)

    ncol = lax.broadcasted_iota(jnp.int32, (1, nblk), 1)
    ok = (ncol * CMP_STRIDE + (CMP_LEN - 1)) <= tpos
    bias_c = jnp.where(ok, 0.0, NEG_INF)
    s = _dot_nt(qb, kcmp_s[...]).reshape(nr, tq, nblk) + bias_c[None]
    e = jnp.exp2((s - jnp.max(s, axis=-1, keepdims=True)) * c2)
    p_cmp = jnp.where(ok[None], e * (1.0 / jnp.sum(e, axis=-1, keepdims=True)), 0.0)
    o_cmp = _dot(p_cmp.reshape(nr * tq, nblk).astype(BF16), vcmp_s[...]).reshape(nr, tq, hd)

    p_sum = p_cmp[0]
    for r in range(1, nr):
        p_sum = p_sum + p_cmp[r]
    ph = p_sum.astype(BF16)
    pm = (p_sum - ph.astype(F32)).astype(BF16)
    ovt = ovt_ref[...]
    p_slc = _dot_nt(ovt, ph) + _dot_nt(ovt, pm)
    jrow = lax.broadcasted_iota(jnp.int32, (nsel, tq), 0)
    blk_t = (t0 + lax.broadcasted_iota(jnp.int32, (nsel, tq), 1)) >> SEL_SHIFT
    forced = (jrow == 0) | (jrow == blk_t) | (jrow == blk_t - 1)
    score = jnp.where(forced, FORCE_SCORE, jnp.where(jrow <= blk_t, p_slc, NEG_INF))
    rank = jnp.zeros((nsel, tq), F32)
    for j in range(nsel):
        sj = score[j:j + 1, :]
        beats = jnp.where(sj > score, 1.0, jnp.where((sj == score) & (jrow > j), 1.0, 0.0))
        rank = rank + beats
    sel_t = jnp.where(rank < float(min(SEL_TOPK, nsel)), 1.0, 0.0)
    sel = jnp.transpose(sel_t).astype(BF16)

    wlen = min(WINDOW + tq, seq)
    w0 = pl.multiple_of(jnp.maximum(t0 + tq - wlen, 0), tq)
    ktw = kwin_s[pl.ds(w0, wlen), :]
    vtw = vwin_s[pl.ds(w0, wlen), :]
    kposw = w0 + lax.broadcasted_iota(jnp.int32, (1, wlen), 1)
    bias_w = jnp.where((kposw <= tpos) & (kposw > tpos - WINDOW), 0.0, NEG_INF)
    sw = _dot_nt(qb, ktw).reshape(nr, tq, wlen) + bias_w[None]
    ew = jnp.exp2((sw - jnp.max(sw, axis=-1, keepdims=True)) * c2)
    o_win = (_dot(ew.reshape(nr * tq, wlen).astype(BF16), vtw).reshape(nr, tq, hd)
             * (1.0 / jnp.sum(ew, axis=-1, keepdims=True)))

    gs = _sigmoid(g_ref[0])
    gates = [jnp.broadcast_to(gs[:, j:j + 1], (tq, hd)) for j in range(nr * 3)]
    o_cw = [gates[3 * r] * o_cmp[r] + gates[3 * r + 2] * o_win[r] for r in range(nr)]

    kb = min(NSA_KB, seq)
    n_kb = (t0 + tq + kb - 1) // kb
    nlb = kb // LANES
    mx_s[...] = jnp.full(mx_s.shape, NEG_INF, F32)
    l_s[...] = jnp.zeros(l_s.shape, F32)
    acc_s[...] = jnp.zeros(acc_s.shape, F32)

    def lane_blocks(x):
        return [x[:, b * LANES:(b + 1) * LANES] for b in range(nlb)]

    def score_step(c, carry):
        k0 = pl.multiple_of(c * kb, kb)
        erow = lax.broadcasted_iota(jnp.int32, (nsel, kb), 0)
        ecol = lax.broadcasted_iota(jnp.int32, (nsel, kb), 1)
        expand = (((k0 + ecol) >> SEL_SHIFT) == erow).astype(BF16)
        chosen = _dot(sel, expand)
        kpos = k0 + lax.broadcasted_iota(jnp.int32, (1, kb), 1)
        bias = jnp.where((chosen > 0.5) & (kpos <= tpos), 0.0, NEG_INF)
        sb = _dot_nt(qb, ksel_s[pl.ds(k0, kb), :]).reshape(nr, tq, kb) + bias[None]
        sb = sb.reshape(nr * tq, kb)
        sbuf_s[c] = sb
        mx = mx_s[...]
        for blk in lane_blocks(sb):
            mx = jnp.maximum(mx, blk)
        mx_s[...] = mx
        return carry

    lax.fori_loop(0, n_kb, score_step, 0)
    m_sel = jnp.max(mx_s[...], axis=-1, keepdims=True)

    def prob_step(c, carry):
        k0 = pl.multiple_of(c * kb, kb)
        pt = jnp.exp2((sbuf_s[c] - m_sel) * c2)
        lsum = l_s[...]
        for blk in lane_blocks(pt):
            lsum = lsum + blk
        l_s[...] = lsum
        acc_s[...] += _dot(pt.astype(BF16), vsel_s[pl.ds(k0, kb), :])
        return carry

    lax.fori_loop(0, n_kb, prob_step, 0)
    o_sel = (acc_s[...] * (1.0 / jnp.sum(l_s[...], axis=-1, keepdims=True))).reshape(nr, tq, hd)

    for r in range(nr):
        o = o_cw[r] + gates[3 * r + 1] * o_sel[r]
        o_ref[:, r * hd:(r + 1) * hd] = o.astype(o_ref.dtype)


def _nsa_tables(seq):
    half = ROPE_DIM // 2
    inv_freq = ROPE_THETA ** (-jnp.arange(half, dtype=F32) / half)
    ang = jnp.arange(seq).astype(F32)[:, None] * inv_freq[None, :]
    cos, sin = jnp.cos(ang), jnp.sin(ang)
    rest = NSA_HEAD_DIM - ROPE_DIM
    cos_t = jnp.concatenate([cos, cos, jnp.ones((seq, rest), F32)], axis=1)
    sin_t = jnp.concatenate([-sin, sin, jnp.zeros((seq, rest), F32)], axis=1)
    n_sel = seq // SEL_BLOCK
    cmp_starts = np.arange(seq // CMP_STRIDE) * CMP_STRIDE
    sel_starts = np.arange(n_sel) * SEL_BLOCK
    overlap = np.clip(np.minimum(cmp_starts[:, None] + CMP_LEN, sel_starts[None, :] + SEL_BLOCK)
                      - np.maximum(cmp_starts[:, None], sel_starts[None, :]), 0, None).astype(np.float32) / CMP_LEN
    return cos_t, sin_t, jnp.asarray(overlap.T, BF16)


def _nsa(pb, gates, pos_k, pos_v, w1k, w2k, w1v, w2v, bsz, seq):
    t = bsz * seq
    tq, hd, g, nr = NSA_TQ, NSA_HEAD_DIM, NSA_KV_GROUPS, NSA_Q_PER_KV
    nq = seq // tq
    nblk = seq // CMP_STRIDE
    nsel = seq // SEL_BLOCK
    cos_t, sin_t, ovt = _nsa_tables(seq)
    kb = min(NSA_KB, seq)
    qblk = B_NQ // (nr * hd)
    kvb = B_NKV // hd

    def kv_spec(split):
        return pl.BlockSpec((seq, hd), lambda b, gi, i: (b, kvb + split * g + gi))

    const2 = lambda b, gi, i: (0, 0)
    return pl.pallas_call(
        functools.partial(_nsa_kernel, seq=seq),
        out_shape=jax.ShapeDtypeStruct((t, NSA_WIDTH), BF16),
        grid=(bsz, g, nq),
        in_specs=[pl.BlockSpec((tq, nr * hd), lambda b, gi, i: (b * nq + i, qblk + gi))]
                 + [kv_spec(sp) for sp in range(6)]
                 + [pl.BlockSpec((1, tq, nr * 3), lambda b, gi, i: (gi, b * nq + i, 0)),
                    pl.BlockSpec((seq, hd), const2),
                    pl.BlockSpec((seq, hd), const2),
                    pl.BlockSpec((CMP_LEN, hd), const2),
                    pl.BlockSpec((CMP_LEN, hd), const2),
                    pl.BlockSpec((CMP_LEN * hd, CMP_HIDDEN), const2),
                    pl.BlockSpec((CMP_HIDDEN, hd), const2),
                    pl.BlockSpec((CMP_LEN * hd, CMP_HIDDEN), const2),
                    pl.BlockSpec((CMP_HIDDEN, hd), const2),
                    pl.BlockSpec((nsel, nblk), const2)],
        out_specs=pl.BlockSpec((tq, nr * hd), lambda b, gi, i: (b * nq + i, gi)),
        scratch_shapes=[pltpu.VMEM((nblk, hd), BF16), pltpu.VMEM((nblk, hd), BF16),
                        pltpu.VMEM((seq, hd), BF16), pltpu.VMEM((seq, hd), BF16),
                        pltpu.VMEM((seq, hd), BF16), pltpu.VMEM((seq, hd), BF16),
                        pltpu.VMEM((seq, hd), F32),
                        pltpu.VMEM((seq // kb, nr * tq, kb), F32),
                        pltpu.VMEM((nr * tq, LANES), F32),
                        pltpu.VMEM((nr * tq, LANES), F32),
                        pltpu.VMEM((nr * tq, hd), F32)],
        compiler_params=_params(("parallel", "parallel", "arbitrary")),
        name="nsa_mixer",
    )(pb, pb, pb, pb, pb, pb, pb, gates, cos_t, sin_t, pos_k, pos_v, w1k, w2k, w1v, w2v, ovt)


def _in_proj(x2d, gain, w_in_stack, layer):
    t = x2d.shape[0]
    tm = min(PROJ_TILE_M, t)
    h = _rms_cast(x2d, gain, tm)
    pa = _proj(h, w_in_stack, layer, 0, PA_COLS, tm, PROJ_TILE_N)
    pb = _proj(h, w_in_stack, layer, PB_START, PB_COLS, tm, PROJ_TILE_N)
    pc = _proj(h, w_in_stack, layer, PC_START, PC_COLS, tm, PROJ_TILE_N)
    ps = _proj_small(h, w_in_stack, layer, tm)
    return pa, pb, pc, ps


def _small_views(ps):
    t = ps.shape[0]
    nsa_g = ps[:, S_NG:S_NG + 3 * NSA_HEADS]
    nsa_g = nsa_g.reshape(t, NSA_KV_GROUPS, NSA_Q_PER_KV * 3).transpose(1, 0, 2)
    dt = ps[:, S_DT:S_DT + SSD_HEADS].reshape(t, SSD_GROUPS, SSD_HEADS_PER_GROUP)
    sub = min(SSD_SUB, t)
    dt_r = dt.reshape(t // sub, sub, SSD_GROUPS, SSD_HEADS_PER_GROUP).transpose(2, 0, 3, 1)
    return nsa_g, jnp.tile(dt.transpose(1, 0, 2), (1, 1, 3)), dt_r


def _layer(x2d, bsz, seq, layer, norm_mix, w_in, gla_gate_w2, gla_gate_b, gla_out_norm,
           nsa_cmp_pos_k, nsa_cmp_pos_v, nsa_cmp_k_w1, nsa_cmp_k_w2, nsa_cmp_v_w1, nsa_cmp_v_w2,
           ssd_conv_w, ssd_conv_b, ssd_dt_bias, ssd_a_log, ssd_d, ssd_out_norm,
           w_branch, w_out, norm_ffn, w_ffn_gate, w_ffn_up, w_ffn_down, norm_final, final_norm):
    t = bsz * seq
    pa, pb, pc, ps = _in_proj(x2d, norm_mix, w_in, layer)
    nsa_g, dt_c, dt_r = _small_views(ps)

    w2pad = jnp.zeros((LANES, GLA_DK), BF16).at[S_GLOW:S_GLOW + GLA_LOWRANK].set(
        gla_gate_w2.astype(BF16))
    o_gla = _gla(pa, pb, ps, w2pad, gla_gate_b, gla_out_norm, bsz, seq, min(256, seq))
    o_nsa = _nsa(pb, nsa_g, nsa_cmp_pos_k, nsa_cmp_pos_v,
                 nsa_cmp_k_w1.astype(BF16), nsa_cmp_k_w2.astype(BF16),
                 nsa_cmp_v_w1.astype(BF16), nsa_cmp_v_w2.astype(BF16), bsz, seq)
    o_ssd = _ssd(pc, dt_c, dt_r, ssd_conv_w, ssd_conv_b, ssd_dt_bias, ssd_a_log, ssd_d,
                 ssd_out_norm, bsz, seq, min(512, seq))

    merged = _merge(o_gla, o_nsa, o_ssd, pa, _column_tiles(w_branch.astype(BF16), MERGE_TILE_N),
                    min(512, t))
    tm = min(PROJ_TILE_M, t)
    x2d = _proj(merged, w_out, layer, 0, D_MODEL, tm, PROJ_TILE_N, residual=x2d, name="out_proj")
    return _ffn(x2d, norm_ffn, norm_final,
                _column_tiles(w_ffn_gate.astype(BF16), FFN_TILE_F),
                _column_tiles(w_ffn_up.astype(BF16), FFN_TILE_F),
                w_ffn_down.astype(BF16), min(512, t), final_norm)


def kernel(x, norm_mix, w_in, gla_gate_w2, gla_gate_b, gla_out_norm, nsa_cmp_pos_k, nsa_cmp_pos_v,
           nsa_cmp_k_w1, nsa_cmp_k_w2, nsa_cmp_v_w1, nsa_cmp_v_w2, ssd_conv_w, ssd_conv_b,
           ssd_dt_bias, ssd_a_log, ssd_d, ssd_out_norm, w_branch, w_out, norm_ffn, w_ffn_gate,
           w_ffn_up, w_ffn_down, norm_final):
    bsz, seq, d = x.shape
    depth = norm_mix.shape[0]
    x2d = x.reshape(bsz * seq, d)
    for l in range(depth):
        x2d = _layer(x2d, bsz, seq, l, norm_mix[l], w_in, gla_gate_w2[l], gla_gate_b[l],
                     gla_out_norm[l], nsa_cmp_pos_k[l], nsa_cmp_pos_v[l], nsa_cmp_k_w1[l],
                     nsa_cmp_k_w2[l], nsa_cmp_v_w1[l], nsa_cmp_v_w2[l], ssd_conv_w[l],
                     ssd_conv_b[l], ssd_dt_bias[l], ssd_a_log[l], ssd_d[l], ssd_out_norm[l],
                     w_branch[l], w_out, norm_ffn[l], w_ffn_gate[l], w_ffn_up[l],
                     w_ffn_down[l], norm_final, l == depth - 1)
    return x2d.reshape(bsz, seq, d)
```

```python
import functools

import numpy as np
import jax
import jax.numpy as jnp
from jax import lax
from jax.experimental import pallas as pl
from jax.experimental.pallas import tpu as pltpu

F32 = jnp.float32
BF16 = jnp.bfloat16

D_MODEL = 2048
RMS_EPS = 1e-6
LOG2_E = 1.4426950408889634
NEG_INF = -1e30
FORCE_SCORE = 1e9

GLA_HEADS = 4
GLA_DK = D_MODEL // 2
GLA_DV = D_MODEL
GLA_HK = GLA_DK // GLA_HEADS
GLA_HV = GLA_DV // GLA_HEADS
GLA_LOWRANK = 16
GLA_GATE_NORMALIZER = 16.0

NSA_HEAD_DIM = 128
NSA_HEAD_SHIFT = 7
NSA_HEADS = D_MODEL // NSA_HEAD_DIM
NSA_KV_GROUPS = 4
NSA_Q_PER_KV = NSA_HEADS // NSA_KV_GROUPS
NSA_WIDTH = NSA_HEADS * NSA_HEAD_DIM
NSA_KV_WIDTH = NSA_KV_GROUPS * NSA_HEAD_DIM
CMP_LEN = 32
CMP_STRIDE = 16
CMP_HIDDEN = 2 * NSA_HEAD_DIM
SEL_BLOCK = 64
SEL_SHIFT = 6
SEL_TOPK = 16
WINDOW = 512
ROPE_THETA = 500000.0
ROPE_DIM = NSA_HEAD_DIM // 4

SSD_D_INNER = 2 * D_MODEL
SSD_HEAD_DIM = 64
SSD_HEAD_SHIFT = 6
SSD_HEADS = SSD_D_INNER // SSD_HEAD_DIM
SSD_GROUPS = 8
SSD_HEADS_PER_GROUP = SSD_HEADS // SSD_GROUPS
SSD_D_STATE = 128
SSD_CONV = 4
SSD_GROUP_WIDTH = SSD_HEADS_PER_GROUP * SSD_HEAD_DIM
SSD_CONV_DIM = SSD_D_INNER + 2 * SSD_GROUPS * SSD_D_STATE

D_FF = ((8 * D_MODEL + 3 * 256 - 1) // (3 * 256)) * 256

LANES = 128
V7X_VMEM_LIMIT_BYTES = 56 * 1024 * 1024

_REF_SIZES = (3 * D_MODEL, GLA_DK, GLA_DK, GLA_DV, GLA_LOWRANK, GLA_DV,
              NSA_WIDTH, 6 * NSA_KV_WIDTH, 3 * NSA_HEADS,
              SSD_D_INNER, SSD_CONV_DIM, SSD_HEADS)
_REF_OFF = tuple(int(v) for v in np.concatenate([[0], np.cumsum(_REF_SIZES)]))
(_R_GATE, _R_GQ, _R_GK, _R_GV, _R_GLOW, _R_GR, _R_NQ, _R_NKV, _R_NG,
 _R_SZ, _R_SXBC, _R_SDT) = _REF_OFF[:-1]
PA_COLS = _R_GLOW
PB_START, PB_COLS = _R_GR, _R_NG - _R_GR
PC_START, PC_COLS = _R_SZ, _R_SDT - _R_SZ
A_GATE, A_GQ, A_GK, A_GV = _R_GATE, _R_GQ, _R_GK, _R_GV
B_GR, B_NQ, B_NKV = 0, _R_NQ - _R_GR, _R_NKV - _R_GR
C_SZ, C_SXBC = 0, _R_SXBC - _R_SZ
S_BLOCKS = (_R_GLOW // LANES, _R_NG // LANES, _R_SDT // LANES)
S_GLOW = _R_GLOW % LANES
S_NG = LANES + _R_NG % LANES
S_DT = 2 * LANES + _R_SDT % LANES
PROJ_TILE_M = 1024
PROJ_TILE_N = 1024
PROJ_CAST_ROWS = 256
MERGE_TILE_N = 512
FFN_TILE_F = 512


def _params(sem, vmem=V7X_VMEM_LIMIT_BYTES):
    return pltpu.CompilerParams(dimension_semantics=sem, vmem_limit_bytes=vmem)


def _sigmoid(x):
    return 1.0 / (1.0 + jnp.exp(-x))


def _silu(x):
    return x * _sigmoid(x)


def _softplus(x):
    return jnp.maximum(x, 0.0) + jnp.log(1.0 + jnp.exp(-jnp.abs(x)))


def _split3(x):
    hi = x.astype(BF16)
    r1 = x - hi.astype(F32)
    mid = r1.astype(BF16)
    lo = (r1 - mid.astype(F32)).astype(BF16)
    return hi, mid, lo


def _dot(a, b):
    return jnp.dot(a, b, preferred_element_type=F32)


def _dot_nt(a, b):
    return lax.dot_general(a, b, (((1,), (1,)), ((), ())), preferred_element_type=F32)


def _dot_tn(a, b):
    return lax.dot_general(a, b, (((0,), (0,)), ((), ())), preferred_element_type=F32)


def _sel_dot(sel, x):
    hi, mid, lo = _split3(x)
    return _dot(sel, hi) + _dot(sel, mid) + _dot(sel, lo)


def _sel_dot_left(x, sel):
    hi, mid, lo = _split3(x)
    return _dot(hi, sel) + _dot(mid, sel) + _dot(lo, sel)


def _rms_rows(x, gain):
    ms = jnp.mean(x * x, axis=-1, keepdims=True)
    return x * lax.rsqrt(ms + RMS_EPS) * gain


def _rms_cast_kernel(x_ref, g_ref, o_ref):
    o_ref[...] = _rms_rows(x_ref[...], g_ref[...]).astype(o_ref.dtype)


def _rms_cast(x2d, gain, tm):
    t, d = x2d.shape
    return pl.pallas_call(
        _rms_cast_kernel,
        out_shape=jax.ShapeDtypeStruct((t, d), BF16),
        grid=(t // tm,),
        in_specs=[pl.BlockSpec((tm, d), lambda i: (i, 0)),
                  pl.BlockSpec((1, d), lambda i: (0, 0))],
        out_specs=pl.BlockSpec((tm, d), lambda i: (i, 0)),
        compiler_params=_params(("parallel",)),
        name="rms_cast",
    )(x2d, gain.reshape(1, d))


def _proj_kernel(*refs, shift, residual):
    refs = list(refs)
    h_ref, wa_ref = refs[:2]
    wb_ref = refs[2] if shift else None
    res_ref = refs[-3] if residual else None
    o_ref, w_s = refs[-2:]
    tn = w_s.shape[1]

    @pl.when(pl.program_id(1) == 0)
    def _():
        for r0 in range(0, w_s.shape[0], PROJ_CAST_ROWS):
            rows = pl.ds(r0, PROJ_CAST_ROWS)
            w = wa_ref[rows, :]
            if shift:
                wcat = jnp.concatenate([w, wb_ref[rows, :]], axis=1)
                w = pltpu.roll(wcat, tn + LANES - shift, axis=1)[:, :tn]
            w_s[rows, :] = w.astype(BF16)

    y = _dot(h_ref[...], w_s[...])
    o_ref[...] = y + res_ref[...] if residual else y


def _proj(h, w_stack, layer, start, ncols, tm, tn, residual=None, name="in_proj"):
    t, d = h.shape
    shift = start % LANES
    base = (start - shift) // tn
    assert (start - shift) % tn == 0 and ncols % tn == 0
    in_specs = [pl.BlockSpec((tm, d), lambda j, i: (i, 0)),
                pl.BlockSpec((None, d, tn), lambda j, i: (layer, 0, base + j))]
    args = [h, w_stack]
    if shift:
        nxt = tn // LANES
        in_specs.append(pl.BlockSpec((None, d, LANES), lambda j, i: (layer, 0, (base + j + 1) * nxt)))
        args.append(w_stack)
    if residual is not None:
        in_specs.append(pl.BlockSpec((tm, tn), lambda j, i: (i, j)))
        args.append(residual)
    return pl.pallas_call(
        functools.partial(_proj_kernel, shift=shift, residual=residual is not None),
        out_shape=jax.ShapeDtypeStruct((t, ncols), F32),
        grid=(ncols // tn, t // tm),
        in_specs=in_specs,
        out_specs=pl.BlockSpec((tm, tn), lambda j, i: (i, j)),
        scratch_shapes=[pltpu.VMEM((d, tn), BF16)],
        compiler_params=_params(("parallel", "arbitrary")),
        name=name,
    )(*args)


def _proj_small_kernel(h_ref, w0_ref, w1_ref, w2_ref, o_ref, w_s):
    @pl.when(pl.program_id(0) == 0)
    def _():
        for b, w_ref in enumerate((w0_ref, w1_ref, w2_ref)):
            w_s[:, b * LANES:(b + 1) * LANES] = w_ref[...].astype(BF16)

    o_ref[...] = _dot(h_ref[...], w_s[...])


def _proj_small(h, w_stack, layer, tm):
    t, d = h.shape
    nb = len(S_BLOCKS)
    wspecs = [pl.BlockSpec((None, d, LANES), functools.partial(lambda i, blk: (layer, 0, blk), blk=blk))
              for blk in S_BLOCKS]
    return pl.pallas_call(
        _proj_small_kernel,
        out_shape=jax.ShapeDtypeStruct((t, nb * LANES), F32),
        grid=(t // tm,),
        in_specs=[pl.BlockSpec((tm, d), lambda i: (i, 0))] + wspecs,
        out_specs=pl.BlockSpec((tm, nb * LANES), lambda i: (i, 0)),
        scratch_shapes=[pltpu.VMEM((d, nb * LANES), BF16)],
        compiler_params=_params(("arbitrary",)),
        name="in_proj_small",
    )(h, w_stack, w_stack, w_stack)


def _merge_kernel(og_ref, on_ref, os_ref, g0_ref, g1_ref, g2_ref, wb0_ref, wb1_ref, wb2_ref, o_ref):
    m = (_sigmoid(g0_ref[...]) * _dot(og_ref[...], wb0_ref[...])
         + _sigmoid(g1_ref[...]) * _dot(on_ref[...], wb1_ref[...])
         + _sigmoid(g2_ref[...]) * _dot(os_ref[...], wb2_ref[...]))
    o_ref[...] = m.astype(o_ref.dtype)


def _merge(o_gla, o_nsa, o_ssd, pa, wb, tm, tn):
    t = o_gla.shape[0]
    d = D_MODEL
    nj = d // tn
    gate_blk0 = A_GATE // tn
    return pl.pallas_call(
        _merge_kernel,
        out_shape=jax.ShapeDtypeStruct((t, d), BF16),
        grid=(t // tm, nj),
        in_specs=[pl.BlockSpec((tm, GLA_DV), lambda i, j: (i, 0)),
                  pl.BlockSpec((tm, NSA_WIDTH), lambda i, j: (i, 0)),
                  pl.BlockSpec((tm, SSD_D_INNER), lambda i, j: (i, 0)),
                  pl.BlockSpec((tm, tn), lambda i, j: (i, gate_blk0 + j)),
                  pl.BlockSpec((tm, tn), lambda i, j: (i, gate_blk0 + nj + j)),
                  pl.BlockSpec((tm, tn), lambda i, j: (i, gate_blk0 + 2 * nj + j)),
                  pl.BlockSpec((GLA_DV, tn), lambda i, j: (0, j)),
                  pl.BlockSpec((NSA_WIDTH, tn), lambda i, j: (1, j)),
                  pl.BlockSpec((SSD_D_INNER, tn), lambda i, j: (1, j))],
        out_specs=pl.BlockSpec((tm, tn), lambda i, j: (i, j)),
        compiler_params=_params(("parallel", "arbitrary")),
        name="branch_merge",
    )(o_gla, o_nsa, o_ssd, pa, pa, pa, wb, wb, wb)


def _ffn_kernel(x_ref, g_ref, gf_ref, wg_ref, wu_ref, wd_ref, o_ref, h_ref, *, final_norm):
    j = pl.program_id(1)

    @pl.when(j == 0)
    def _():
        x = x_ref[...]
        h_ref[...] = _rms_rows(x, g_ref[...]).astype(BF16)
        o_ref[...] = x

    h = h_ref[...]
    a = _silu(_dot(h, wg_ref[...])) * _dot(h, wu_ref[...])
    o_ref[...] += _dot(a.astype(BF16), wd_ref[...])

    if final_norm:
        @pl.when(j == pl.num_programs(1) - 1)
        def _():
            o_ref[...] = _rms_rows(o_ref[...], gf_ref[...])


def _ffn(x2d, gain, gain_final, wg, wu, wd, tm, tf, final_norm):
    t, d = x2d.shape
    f = wg.shape[1]
    return pl.pallas_call(
        functools.partial(_ffn_kernel, final_norm=final_norm),
        out_shape=jax.ShapeDtypeStruct((t, d), F32),
        grid=(t // tm, f // tf),
        in_specs=[pl.BlockSpec((tm, d), lambda i, j: (i, 0)),
                  pl.BlockSpec((1, d), lambda i, j: (0, 0)),
                  pl.BlockSpec((1, d), lambda i, j: (0, 0)),
                  pl.BlockSpec((d, tf), lambda i, j: (0, j)),
                  pl.BlockSpec((d, tf), lambda i, j: (0, j)),
                  pl.BlockSpec((tf, d), lambda i, j: (j, 0))],
        out_specs=pl.BlockSpec((tm, d), lambda i, j: (i, 0)),
        scratch_shapes=[pltpu.VMEM((tm, d), BF16)],
        compiler_params=_params(("parallel", "arbitrary")),
        name="ffn",
    )(x2d, gain.reshape(1, d), gain_final.reshape(1, d), wg, wu, wd)


GLA_CHUNK = 64
GLA_SUB = 16
GLA_EXP_CAP = 88.0
GLA_HEADS_PER_STEP = 2
GLA_TILE = 256


def _gla_kernel(q_ref, k_ref, v_ref, r_ref, sm_ref, w2_ref, gb_ref, ng_ref, o_ref, st_ref,
                *, chunks_per_tile):
    @pl.when(pl.program_id(2) == 0)
    def _():
        st_ref[...] = jnp.zeros_like(st_ref)

    c_rows = GLA_CHUNK
    row = lax.broadcasted_iota(jnp.int32, (c_rows, c_rows), 0)
    col = lax.broadcasted_iota(jnp.int32, (c_rows, c_rows), 1)
    tri = (col <= row).astype(BF16)
    srow = lax.broadcasted_iota(jnp.int32, (GLA_SUB, c_rows), 0)
    scol = lax.broadcasted_iota(jnp.int32, (GLA_SUB, c_rows), 1)
    scale = GLA_HK ** -0.5

    smb = sm_ref[...].astype(BF16)
    work = []
    for h in range(GLA_HEADS_PER_STEP):
        kcols = slice(h * GLA_HK, (h + 1) * GLA_HK)
        vcols = slice(h * GLA_HV, (h + 1) * GLA_HV)
        ga = _dot(smb, w2_ref[:, kcols]) + gb_ref[:, kcols]
        log_a_all = -_softplus(-ga) * (1.0 / GLA_GATE_NORMALIZER)
        for c in range(chunks_per_tile):
            rows = slice(c * c_rows, (c + 1) * c_rows)
            bc = _sel_dot(tri, log_a_all[rows])
            q = q_ref[rows, kcols] * scale
            k = k_ref[rows, kcols]
            vb = v_ref[rows, vcols].astype(BF16)
            parts = []
            for i in range(c_rows // GLA_SUB):
                lo = i * GLA_SUB
                ref_pt = bc[lo - 1:lo, :] if i > 0 else jnp.zeros((1, GLA_HK), F32)
                kt = (k * jnp.exp(jnp.minimum(ref_pt - bc, GLA_EXP_CAP))).astype(BF16)
                qi = (q[lo:lo + GLA_SUB, :] * jnp.exp(bc[lo:lo + GLA_SUB, :] - ref_pt)).astype(BF16)
                sc = _dot_nt(qi, kt)
                parts.append(jnp.where(scol <= srow + lo, sc, 0.0))
            intra = _dot(jnp.concatenate(parts, axis=0).astype(BF16), vb)
            b_last = bc[c_rows - 1:c_rows, :]
            khat = (k * jnp.exp(b_last - bc)).astype(BF16)
            work.append((h, rows, vcols, (q * jnp.exp(bc)).astype(BF16), intra,
                         jnp.exp(b_last), _dot_tn(vb, khat)))

    states = [st_ref[h] for h in range(GLA_HEADS_PER_STEP)]
    for h, rows, vcols, qe, intra, decay, incr in work:
        out = _dot_nt(qe, states[h].astype(BF16)) + intra
        states[h] = states[h] * decay + incr
        y = _rms_rows(out, ng_ref[...]) * _silu(r_ref[rows, vcols])
        o_ref[rows, vcols] = y.astype(o_ref.dtype)
    for h in range(GLA_HEADS_PER_STEP):
        st_ref[h] = states[h]


def _gla(pa, pb, ps, w2pad, gate_b, norm_g, bsz, seq, tile):
    t = bsz * seq
    nt = seq // tile
    hps = GLA_HEADS_PER_STEP
    kw, vw = hps * GLA_HK, hps * GLA_HV
    qb, kb = A_GQ // kw, A_GK // kw
    vb, rb = A_GV // vw, B_GR // vw
    return pl.pallas_call(
        functools.partial(_gla_kernel, chunks_per_tile=tile // GLA_CHUNK),
        out_shape=jax.ShapeDtypeStruct((t, GLA_DV), BF16),
        grid=(bsz, GLA_HEADS // hps, nt),
        in_specs=[pl.BlockSpec((tile, kw), lambda b, h, c: (b * nt + c, qb + h)),
                  pl.BlockSpec((tile, kw), lambda b, h, c: (b * nt + c, kb + h)),
                  pl.BlockSpec((tile, vw), lambda b, h, c: (b * nt + c, vb + h)),
                  pl.BlockSpec((tile, vw), lambda b, h, c: (b * nt + c, rb + h)),
                  pl.BlockSpec((tile, LANES), lambda b, h, c: (b * nt + c, 0)),
                  pl.BlockSpec((LANES, kw), lambda b, h, c: (0, h)),
                  pl.BlockSpec((1, kw), lambda b, h, c: (0, h)),
                  pl.BlockSpec((1, GLA_HV), lambda b, h, c: (0, 0))],
        out_specs=pl.BlockSpec((tile, vw), lambda b, h, c: (b * nt + c, h)),
        scratch_shapes=[pltpu.VMEM((hps, GLA_HV, GLA_HK), F32)],
        compiler_params=_params(("parallel", "parallel", "arbitrary")),
        name="gla_mixer",
    )(pa, pa, pa, pb, ps, w2pad, gate_b.reshape(1, GLA_DK), norm_g.reshape(1, GLA_HV))


SSD_SUB = 128
SSD_HIST = 8


def _causal_conv(e_ref, src_ref, w_ref, b_ref):
    n = src_ref.shape[0]
    e_ref[pl.ds(SSD_HIST, n), :] = src_ref[...]
    acc = src_ref[...] * w_ref[SSD_CONV - 1:SSD_CONV, :] + b_ref[...]
    for s in range(1, SSD_CONV):
        acc = acc + e_ref[pl.ds(SSD_HIST - s, n), :] * w_ref[SSD_CONV - 1 - s:SSD_CONV - s, :]
    e_ref[pl.ds(0, SSD_HIST), :] = src_ref[pl.ds(n - SSD_HIST, SSD_HIST), :]
    return _silu(acc)


def _ssd_kernel(xs_ref, bm_ref, cm_ref, z_ref, dtc_ref, dtr_ref,
                wx_ref, wb_ref, wc_ref, bx_ref, bb_ref, bc_ref,
                dbc_ref, dbr_ref, alc_ref, alr_ref, dsk_ref, ng_ref,
                o_ref, st_ref, ex_ref, eb_ref, ec_ref):
    n = xs_ref.shape[0]
    sub = min(SSD_SUB, n)
    hpg, hd = SSD_HEADS_PER_GROUP, SSD_HEAD_DIM

    @pl.when(pl.program_id(2) == 0)
    def _():
        st_ref[...] = jnp.zeros_like(st_ref)
        ex_ref[pl.ds(0, SSD_HIST), :] = jnp.zeros((SSD_HIST, ex_ref.shape[1]), F32)
        eb_ref[pl.ds(0, SSD_HIST), :] = jnp.zeros((SSD_HIST, eb_ref.shape[1]), F32)
        ec_ref[pl.ds(0, SSD_HIST), :] = jnp.zeros((SSD_HIST, ec_ref.shape[1]), F32)

    xs = _causal_conv(ex_ref, xs_ref, wx_ref, bx_ref)
    bmb = _causal_conv(eb_ref, bm_ref, wb_ref, bb_ref).astype(BF16)
    cmb = _causal_conv(ec_ref, cm_ref, wc_ref, bc_ref).astype(BF16)

    w3 = 3 * hpg
    krow = lax.broadcasted_iota(jnp.int32, (w3, hpg * hd), 0)
    kcol = lax.broadcasted_iota(jnp.int32, (w3, hpg * hd), 1) >> SSD_HEAD_SHIFT
    expand3 = ((krow == kcol) | (krow == kcol + hpg) | (krow == kcol + 2 * hpg)).astype(BF16)

    def expand_heads(x):
        hi, mid, lo = _split3(x)
        klane = lax.broadcasted_iota(jnp.int32, x.shape, 1)
        return _dot(jnp.where(klane < hpg, hi, jnp.where(klane < 2 * hpg, mid, lo)), expand3)

    dt_c = _softplus(dtc_ref[0] + dbc_ref[0])
    a_c = -jnp.exp(alc_ref[0])
    a_r = -jnp.exp(alr_ref[0])
    xdt = xs * expand_heads(dt_c)

    row = lax.broadcasted_iota(jnp.int32, (sub, sub), 0)
    col = lax.broadcasted_iota(jnp.int32, (sub, sub), 1)
    tri = (col <= row).astype(BF16)
    triu = (row <= col).astype(BF16)
    causal_bias = jnp.where(col <= row, 0.0, NEG_INF)
    lane = lax.broadcasted_iota(jnp.int32, (sub, 2 * hd), 1)

    for si in range(n // sub):
        rows = slice(si * sub, (si + 1) * sub)
        acum_c = _sel_dot(tri, dt_c[rows] * a_c)
        adt_r = _softplus(dtr_ref[0, si] + dbr_ref[0]) * a_r
        acum_r = _sel_dot_left(adt_r, triu)
        a_last = acum_c[sub - 1:sub, :]
        eac_x = expand_heads(jnp.exp(acum_c))
        dec_x = expand_heads(jnp.exp(a_last - acum_c))
        sdec_x = expand_heads(jnp.broadcast_to(jnp.exp(a_last), (8, w3)))[0:1, :]
        ac2 = acum_c * LOG2_E
        ar2 = acum_r * LOG2_E

        xd, cs, bs = xdt[rows], cmb[rows], bmb[rows]
        cb = _dot_nt(cs, bs)
        pair_out = []
        for pr in range(hpg // 2):
            xp = xd[:, pr * 2 * hd:(pr + 1) * 2 * hd]
            acc = None
            for half in range(2):
                r = 2 * pr + half
                seg = jnp.exp2(ac2[:, r:r + 1] - ar2[r:r + 1, :] + causal_bias)
                rhs = jnp.where((lane >> SSD_HEAD_SHIFT) == half, xp, 0.0).astype(BF16)
                term = _dot((cb * seg).astype(BF16), rhs)
                acc = term if acc is None else acc + term
            pair_out.append(acc)
        y_diag = jnp.concatenate(pair_out, axis=1)

        st = st_ref[...]
        y_off = _dot(cs, st.astype(BF16)) * eac_x
        st_ref[...] = st * sdec_x + _dot_tn(bs, (xd * dec_x).astype(BF16))

        y = (y_diag + y_off + xs[rows] * dsk_ref[0]) * _silu(z_ref[rows, :])
        o_ref[rows, :] = _rms_rows(y, ng_ref[...]).astype(o_ref.dtype)


def _ssd(pc, dt_c, dt_r, conv_w, conv_b, dt_bias, a_log, d_skip, norm_g, bsz, seq, chunk):
    t = bsz * seq
    nc = seq // chunk
    sub = min(SSD_SUB, chunk)
    g, hpg, gw, ns = SSD_GROUPS, SSD_HEADS_PER_GROUP, SSD_GROUP_WIDTH, SSD_D_STATE
    xsb = C_SXBC // gw
    bmb = (C_SXBC + SSD_D_INNER) // ns
    cmb = (C_SXBC + SSD_D_INNER + g * ns) // ns
    zb = C_SZ // gw
    cw_b = SSD_D_INNER // ns
    cw_c = (SSD_D_INNER + g * ns) // ns
    conv_b2 = conv_b.reshape(1, SSD_CONV_DIM)
    dbc = jnp.tile(dt_bias.reshape(g, 1, hpg), (1, 1, 3))
    dbr = dt_bias.reshape(g, hpg, 1)
    alc = jnp.tile(a_log.reshape(g, 1, hpg), (1, 1, 3))
    alr = a_log.reshape(g, hpg, 1)
    dsk = jnp.repeat(d_skip, SSD_HEAD_DIM).reshape(g, 1, gw)
    ng = norm_g.reshape(1, SSD_D_INNER)
    return pl.pallas_call(
        _ssd_kernel,
        out_shape=jax.ShapeDtypeStruct((t, SSD_D_INNER), BF16),
        grid=(bsz, g, nc),
        in_specs=[pl.BlockSpec((chunk, gw), lambda b, gi, c: (b * nc + c, xsb + gi)),
                  pl.BlockSpec((chunk, ns), lambda b, gi, c: (b * nc + c, bmb + gi)),
                  pl.BlockSpec((chunk, ns), lambda b, gi, c: (b * nc + c, cmb + gi)),
                  pl.BlockSpec((chunk, gw), lambda b, gi, c: (b * nc + c, zb + gi)),
                  pl.BlockSpec((1, chunk, 3 * hpg), lambda b, gi, c: (gi, b * nc + c, 0)),
                  pl.BlockSpec((1, chunk // sub, hpg, sub), lambda b, gi, c: (gi, b * nc + c, 0, 0)),
                  pl.BlockSpec((SSD_CONV, gw), lambda b, gi, c: (0, gi)),
                  pl.BlockSpec((SSD_CONV, ns), lambda b, gi, c: (0, cw_b + gi)),
                  pl.BlockSpec((SSD_CONV, ns), lambda b, gi, c: (0, cw_c + gi)),
                  pl.BlockSpec((1, gw), lambda b, gi, c: (0, gi)),
                  pl.BlockSpec((1, ns), lambda b, gi, c: (0, cw_b + gi)),
                  pl.BlockSpec((1, ns), lambda b, gi, c: (0, cw_c + gi)),
                  pl.BlockSpec((1, 1, 3 * hpg), lambda b, gi, c: (gi, 0, 0)),
                  pl.BlockSpec((1, hpg, 1), lambda b, gi, c: (gi, 0, 0)),
                  pl.BlockSpec((1, 1, 3 * hpg), lambda b, gi, c: (gi, 0, 0)),
                  pl.BlockSpec((1, hpg, 1), lambda b, gi, c: (gi, 0, 0)),
                  pl.BlockSpec((1, 1, gw), lambda b, gi, c: (gi, 0, 0)),
                  pl.BlockSpec((1, gw), lambda b, gi, c: (0, gi))],
        out_specs=pl.BlockSpec((chunk, gw), lambda b, gi, c: (b * nc + c, gi)),
        scratch_shapes=[pltpu.VMEM((ns, gw), F32),
                        pltpu.VMEM((SSD_HIST + chunk, gw), F32),
                        pltpu.VMEM((SSD_HIST + chunk, ns), F32),
                        pltpu.VMEM((SSD_HIST + chunk, ns), F32)],
        compiler_params=_params(("parallel", "parallel", "arbitrary")),
        name="ssd_mixer",
    )(pc, pc, pc, pc, dt_c, dt_r, conv_w, conv_w, conv_w, conv_b2, conv_b2, conv_b2,
      dbc, dbr, alc, alr, dsk, ng)


NSA_TQ = 256
NSA_KB = 512


def _rope(x, cos, sin):
    lane = lax.broadcasted_iota(jnp.int32, x.shape, 1)
    half = ROPE_DIM // 2
    swapped = jnp.where(lane < half, pltpu.roll(x, LANES - half, axis=1), pltpu.roll(x, half, axis=1))
    return x * cos + swapped * sin


def _compress(src_ref, pe_ref, w1_ref, w2_ref, nblk):
    hd = NSA_HEAD_DIM
    a0 = jnp.zeros((nblk, CMP_HIDDEN), F32)
    a1 = jnp.zeros((nblk, CMP_HIDDEN), F32)
    for c in range(CMP_STRIDE):
        xc = src_ref[pl.ds(c, nblk, stride=CMP_STRIDE), :]
        a0 = a0 + _dot((xc + pe_ref[c:c + 1, :]).astype(BF16), w1_ref[c * hd:(c + 1) * hd, :])
        c1 = CMP_STRIDE + c
        a1 = a1 + _dot((xc + pe_ref[c1:c1 + 1, :]).astype(BF16), w1_ref[c1 * hd:(c1 + 1) * hd, :])
    hid = a0 + pltpu.roll(a1, nblk - 1, axis=0)
    return _dot(_silu(hid).astype(BF16), w2_ref[...])


def _nsa_kernel(q_ref, kc_ref, vc_ref, ks_ref, vs_ref, kw_ref, vw_ref, g_ref,
                cos_ref, sin_ref, pek_ref, pev_ref, w1k_ref, w2k_ref, w1v_ref, w2v_ref, ovt_ref,
                o_ref, kcmp_s, vcmp_s, ksel_s, vsel_s, kwin_s, vwin_s, krope_s,
                sbuf_s, mx_s, acc_s, *, seq):
    tq, hd, nr = NSA_TQ, NSA_HEAD_DIM, NSA_Q_PER_KV
    i = pl.program_id(2)
    nblk = seq // CMP_STRIDE
    nsel = seq // SEL_BLOCK
    scale = hd ** -0.5

    @pl.when(i == 0)
    def _():
        cos, sin = cos_ref[...], sin_ref[...]
        krope_s[...] = _rope(kc_ref[...], cos, sin)
        ksel_s[...] = _rope(ks_ref[...], cos, sin).astype(BF16)
        kwin_s[...] = _rope(kw_ref[...], cos, sin).astype(BF16)
        vsel_s[:, :hd] = vs_ref[...].astype(BF16)
        vsel_s[:, hd:] = jnp.ones((seq, hd), BF16)
        vwin_s[:, :hd] = vw_ref[...].astype(BF16)
        vwin_s[:, hd:] = jnp.ones((seq, hd), BF16)
        kcmp_s[...] = _compress(krope_s, pek_ref, w1k_ref, w2k_ref, nblk).astype(BF16)
        vcmp_s[:, :hd] = _compress(vc_ref, pev_ref, w1v_ref, w2v_ref, nblk).astype(BF16)
        vcmp_s[:, hd:] = jnp.ones((nblk, hd), BF16)

    t0 = pl.multiple_of(i * tq, tq)
    cos = cos_ref[pl.ds(t0, tq), :]
    sin = sin_ref[pl.ds(t0, tq), :]
    qs = [_rope(q_ref[:, r * hd:(r + 1) * hd], cos, sin) for r in range(nr)]
    qb = jnp.concatenate(qs, axis=0).astype(BF16)
    c2 = scale * LOG2_E
    tpos = t0 + lax.broadcasted_iota(jnp.int32, (tq, 1), 0)

    ncol = lax.broadcasted_iota(jnp.int32, (1, nblk), 1)
    ok = (ncol * CMP_STRIDE + (CMP_LEN - 1)) <= tpos
    bias_c = jnp.where(ok, 0.0, NEG_INF)
    s = _dot_nt(qb, kcmp_s[...]).reshape(nr, tq, nblk) + bias_c[None]
    e = jnp.exp2((s - jnp.max(s, axis=-1, keepdims=True)) * c2)
    oc = _dot(e.reshape(nr * tq, nblk).astype(BF16), vcmp_s[...]).reshape(nr, tq, 2 * hd)
    inv_lc = 1.0 / oc[:, :, hd:]
    has_cmp = (tpos >= CMP_LEN - 1)[None]
    o_cmp = jnp.where(has_cmp, oc[:, :, :hd] * inv_lc, 0.0)
    p_cmp = jnp.where(ok[None], e * inv_lc[:, :, :nblk], 0.0)

    p_sum = p_cmp[0]
    for r in range(1, nr):
        p_sum = p_sum + p_cmp[r]
    ph = p_sum.astype(BF16)
    pm = (p_sum - ph.astype(F32)).astype(BF16)
    ovt = ovt_ref[...]
    p_slc = _dot_nt(ovt, ph) + _dot_nt(ovt, pm)
    jrow = lax.broadcasted_iota(jnp.int32, (nsel, tq), 0)
    blk_t = (t0 + lax.broadcasted_iota(jnp.int32, (nsel, tq), 1)) >> SEL_SHIFT
    forced = (jrow == 0) | (jrow == blk_t) | (jrow == blk_t - 1)
    score = jnp.where(forced, FORCE_SCORE, jnp.where(jrow <= blk_t, p_slc, NEG_INF))
    rank = jnp.zeros((nsel, tq), F32)
    for j in range(nsel):
        sj = score[j:j + 1, :]
        beats = jnp.where(sj > score, 1.0, jnp.where((sj == score) & (jrow > j), 1.0, 0.0))
        rank = rank + beats
    sel_t = jnp.where(rank < float(min(SEL_TOPK, nsel)), 1.0, 0.0)
    sel = jnp.transpose(sel_t).astype(BF16)

    wlen = min(WINDOW + tq, seq)
    w0 = pl.multiple_of(jnp.maximum(t0 + tq - wlen, 0), tq)
    ktw = kwin_s[pl.ds(w0, wlen), :]
    vtw = vwin_s[pl.ds(w0, wlen), :]
    kposw = w0 + lax.broadcasted_iota(jnp.int32, (1, wlen), 1)
    bias_w = jnp.where((kposw <= tpos) & (kposw > tpos - WINDOW), 0.0, NEG_INF)
    sw = _dot_nt(qb, ktw).reshape(nr, tq, wlen) + bias_w[None]
    ew = jnp.exp2((sw - jnp.max(sw, axis=-1, keepdims=True)) * c2)
    ow = _dot(ew.reshape(nr * tq, wlen).astype(BF16), vtw).reshape(nr, tq, 2 * hd)
    o_win = ow[:, :, :hd] * (1.0 / ow[:, :, hd:])

    ng = nr * 3
    gs = _sigmoid(g_ref[0])
    g_hi, g_mid, g_lo = _split3(gs)
    glane = lax.broadcasted_iota(jnp.int32, gs.shape, 1)
    g_terms = jnp.where(glane < ng, g_hi, jnp.where(glane < 2 * ng, g_mid, g_lo))
    grow = lax.broadcasted_iota(jnp.int32, (3 * ng, ng * hd), 0)
    gcol = lax.broadcasted_iota(jnp.int32, (3 * ng, ng * hd), 1) >> NSA_HEAD_SHIFT
    gsel = (grow == gcol) | (grow == gcol + ng) | (grow == gcol + 2 * ng)
    gexp = _dot(g_terms, gsel.astype(BF16))
    gates = [gexp[:, j * hd:(j + 1) * hd] for j in range(ng)]
    o_cw = [gates[3 * r] * o_cmp[r] + gates[3 * r + 2] * o_win[r] for r in range(nr)]

    kb = min(NSA_KB, seq)
    n_kb = (t0 + tq + kb - 1) // kb
    nlb = kb // LANES
    mx_s[...] = jnp.full(mx_s.shape, NEG_INF, F32)
    acc_s[...] = jnp.zeros(acc_s.shape, F32)

    def lane_blocks(x):
        return [x[:, b * LANES:(b + 1) * LANES] for b in range(nlb)]

    def score_step(c, carry):
        k0 = pl.multiple_of(c * kb, kb)
        erow = lax.broadcasted_iota(jnp.int32, (nsel, kb), 0)
        ecol = lax.broadcasted_iota(jnp.int32, (nsel, kb), 1)
        expand = (((k0 + ecol) >> SEL_SHIFT) == erow).astype(BF16)
        chosen = _dot(sel, expand)
        kpos = k0 + lax.broadcasted_iota(jnp.int32, (1, kb), 1)
        bias = jnp.where((chosen > 0.5) & (kpos <= tpos), 0.0, NEG_INF)
        sb = _dot_nt(qb, ksel_s[pl.ds(k0, kb), :]).reshape(nr, tq, kb) + bias[None]
        sb = sb.reshape(nr * tq, kb)
        sbuf_s[c] = sb
        mx = mx_s[...]
        for blk in lane_blocks(sb):
            mx = jnp.maximum(mx, blk)
        mx_s[...] = mx
        return carry

    lax.fori_loop(0, n_kb, score_step, 0)
    m_sel = jnp.max(mx_s[...], axis=-1, keepdims=True)

    def prob_step(c, carry):
        k0 = pl.multiple_of(c * kb, kb)
        pt = jnp.exp2((sbuf_s[c] - m_sel) * c2)
        acc_s[...] += _dot(pt.astype(BF16), vsel_s[pl.ds(k0, kb), :])
        return carry

    lax.fori_loop(0, n_kb, prob_step, 0)
    o_sel = (acc_s[:, :hd] * (1.0 / acc_s[:, hd:])).reshape(nr, tq, hd)

    for r in range(nr):
        o = o_cw[r] + gates[3 * r + 1] * o_sel[r]
        o_ref[:, r * hd:(r + 1) * hd] = o.astype(o_ref.dtype)


def _nsa_tables(seq):
    half = ROPE_DIM // 2
    inv_freq = ROPE_THETA ** (-jnp.arange(half, dtype=F32) / half)
    ang = jnp.arange(seq).astype(F32)[:, None] * inv_freq[None, :]
    cos, sin = jnp.cos(ang), jnp.sin(ang)
    rest = NSA_HEAD_DIM - ROPE_DIM
    cos_t = jnp.concatenate([cos, cos, jnp.ones((seq, rest), F32)], axis=1)
    sin_t = jnp.concatenate([-sin, sin, jnp.zeros((seq, rest), F32)], axis=1)
    n_sel = seq // SEL_BLOCK
    cmp_starts = np.arange(seq // CMP_STRIDE) * CMP_STRIDE
    sel_starts = np.arange(n_sel) * SEL_BLOCK
    overlap = np.clip(np.minimum(cmp_starts[:, None] + CMP_LEN, sel_starts[None, :] + SEL_BLOCK)
                      - np.maximum(cmp_starts[:, None], sel_starts[None, :]), 0, None).astype(np.float32) / CMP_LEN
    return cos_t, sin_t, jnp.asarray(overlap.T, BF16)


def _nsa(pb, gates, pos_k, pos_v, w1k, w2k, w1v, w2v, bsz, seq):
    t = bsz * seq
    tq, hd, g, nr = NSA_TQ, NSA_HEAD_DIM, NSA_KV_GROUPS, NSA_Q_PER_KV
    nq = seq // tq
    nblk = seq // CMP_STRIDE
    nsel = seq // SEL_BLOCK
    cos_t, sin_t, ovt = _nsa_tables(seq)
    kb = min(NSA_KB, seq)
    qblk = B_NQ // (nr * hd)
    kvb = B_NKV // hd

    def kv_spec(split):
        return pl.BlockSpec((seq, hd), lambda b, gi, i: (b, kvb + split * g + gi))

    const2 = lambda b, gi, i: (0, 0)
    return pl.pallas_call(
        functools.partial(_nsa_kernel, seq=seq),
        out_shape=jax.ShapeDtypeStruct((t, NSA_WIDTH), BF16),
        grid=(bsz, g, nq),
        in_specs=[pl.BlockSpec((tq, nr * hd), lambda b, gi, i: (b * nq + i, qblk + gi))]
                 + [kv_spec(sp) for sp in range(6)]
                 + [pl.BlockSpec((1, tq, nr * 9), lambda b, gi, i: (gi, b * nq + i, 0)),
                    pl.BlockSpec((seq, hd), const2),
                    pl.BlockSpec((seq, hd), const2),
                    pl.BlockSpec((CMP_LEN, hd), const2),
                    pl.BlockSpec((CMP_LEN, hd), const2),
                    pl.BlockSpec((CMP_LEN * hd, CMP_HIDDEN), const2),
                    pl.BlockSpec((CMP_HIDDEN, hd), const2),
                    pl.BlockSpec((CMP_LEN * hd, CMP_HIDDEN), const2),
                    pl.BlockSpec((CMP_HIDDEN, hd), const2),
                    pl.BlockSpec((nsel, nblk), const2)],
        out_specs=pl.BlockSpec((tq, nr * hd), lambda b, gi, i: (b * nq + i, gi)),
        scratch_shapes=[pltpu.VMEM((nblk, hd), BF16), pltpu.VMEM((nblk, 2 * hd), BF16),
                        pltpu.VMEM((seq, hd), BF16), pltpu.VMEM((seq, 2 * hd), BF16),
                        pltpu.VMEM((seq, hd), BF16), pltpu.VMEM((seq, 2 * hd), BF16),
                        pltpu.VMEM((seq, hd), F32),
                        pltpu.VMEM((seq // kb, nr * tq, kb), F32),
                        pltpu.VMEM((nr * tq, LANES), F32),
                        pltpu.VMEM((nr * tq, 2 * hd), F32)],
        compiler_params=_params(("parallel", "parallel", "arbitrary")),
        name="nsa_mixer",
    )(pb, pb, pb, pb, pb, pb, pb, gates, cos_t, sin_t, pos_k, pos_v, w1k, w2k, w1v, w2v, ovt)


def _in_proj(x2d, gain, w_in_stack, layer):
    t = x2d.shape[0]
    tm = min(PROJ_TILE_M, t)
    h = _rms_cast(x2d, gain, tm)
    pa = _proj(h, w_in_stack, layer, 0, PA_COLS, tm, PROJ_TILE_N)
    pb = _proj(h, w_in_stack, layer, PB_START, PB_COLS, tm, PROJ_TILE_N)
    pc = _proj(h, w_in_stack, layer, PC_START, PC_COLS, tm, PROJ_TILE_N)
    ps = _proj_small(h, w_in_stack, layer, tm)
    return pa, pb, pc, ps


def _small_views(ps):
    t = ps.shape[0]
    nsa_g = ps[:, S_NG:S_NG + 3 * NSA_HEADS]
    nsa_g = nsa_g.reshape(t, NSA_KV_GROUPS, NSA_Q_PER_KV * 3).transpose(1, 0, 2)
    nsa_g = jnp.tile(nsa_g, (1, 1, 3))
    dt = ps[:, S_DT:S_DT + SSD_HEADS].reshape(t, SSD_GROUPS, SSD_HEADS_PER_GROUP)
    sub = min(SSD_SUB, t)
    dt_r = dt.reshape(t // sub, sub, SSD_GROUPS, SSD_HEADS_PER_GROUP).transpose(2, 0, 3, 1)
    return nsa_g, jnp.tile(dt.transpose(1, 0, 2), (1, 1, 3)), dt_r


def _layer(x2d, bsz, seq, layer, norm_mix, w_in, gla_gate_w2, gla_gate_b, gla_out_norm,
           nsa_cmp_pos_k, nsa_cmp_pos_v, nsa_cmp_k_w1, nsa_cmp_k_w2, nsa_cmp_v_w1, nsa_cmp_v_w2,
           ssd_conv_w, ssd_conv_b, ssd_dt_bias, ssd_a_log, ssd_d, ssd_out_norm,
           w_branch, w_out, norm_ffn, w_ffn_gate, w_ffn_up, w_ffn_down, norm_final, final_norm):
    t = bsz * seq
    pa, pb, pc, ps = _in_proj(x2d, norm_mix, w_in, layer)
    nsa_g, dt_c, dt_r = _small_views(ps)

    w2pad = jnp.zeros((LANES, GLA_DK), BF16).at[S_GLOW:S_GLOW + GLA_LOWRANK].set(
        gla_gate_w2.astype(BF16))
    o_gla = _gla(pa, pb, ps, w2pad, gla_gate_b, gla_out_norm, bsz, seq, min(GLA_TILE, seq))
    o_nsa = _nsa(pb, nsa_g, nsa_cmp_pos_k, nsa_cmp_pos_v,
                 nsa_cmp_k_w1.astype(BF16), nsa_cmp_k_w2.astype(BF16),
                 nsa_cmp_v_w1.astype(BF16), nsa_cmp_v_w2.astype(BF16), bsz, seq)
    o_ssd = _ssd(pc, dt_c, dt_r, ssd_conv_w, ssd_conv_b, ssd_dt_bias, ssd_a_log, ssd_d,
                 ssd_out_norm, bsz, seq, min(512, seq))

    merged = _merge(o_gla, o_nsa, o_ssd, pa, w_branch.astype(BF16), min(512, t), MERGE_TILE_N)
    tm = min(PROJ_TILE_M, t)
    x2d = _proj(merged, w_out, layer, 0, D_MODEL, tm, PROJ_TILE_N, residual=x2d, name="out_proj")
    return _ffn(x2d, norm_ffn, norm_final, w_ffn_gate.astype(BF16), w_ffn_up.astype(BF16),
                w_ffn_down.astype(BF16), min(512, t), FFN_TILE_F, final_norm)


def kernel(x, norm_mix, w_in, gla_gate_w2, gla_gate_b, gla_out_norm, nsa_cmp_pos_k, nsa_cmp_pos_v,
           nsa_cmp_k_w1, nsa_cmp_k_w2, nsa_cmp_v_w1, nsa_cmp_v_w2, ssd_conv_w, ssd_conv_b,
           ssd_dt_bias, ssd_a_log, ssd_d, ssd_out_norm, w_branch, w_out, norm_ffn, w_ffn_gate,
           w_ffn_up, w_ffn_down, norm_final):
    bsz, seq, d = x.shape
    depth = norm_mix.shape[0]
    x2d = x.reshape(bsz * seq, d)
    for l in range(depth):
        x2d = _layer(x2d, bsz, seq, l, norm_mix[l], w_in, gla_gate_w2[l], gla_gate_b[l],
                     gla_out_norm[l], nsa_cmp_pos_k[l], nsa_cmp_pos_v[l], nsa_cmp_k_w1[l],
                     nsa_cmp_k_w2[l], nsa_cmp_v_w1[l], nsa_cmp_v_w2[l], ssd_conv_w[l],
                     ssd_conv_b[l], ssd_dt_bias[l], ssd_a_log[l], ssd_d[l], ssd_out_norm[l],
                     w_branch[l], w_out, norm_ffn[l], w_ffn_gate[l], w_ffn_up[l],
                     w_ffn_down[l], norm_final, l == depth - 1)
    return x2d.reshape(bsz, seq, d)
```

```python
import functools

import numpy as np
import jax
import jax.numpy as jnp
from jax import lax
from jax.experimental import pallas as pl
from jax.experimental.pallas import tpu as pltpu

F32 = jnp.float32
BF16 = jnp.bfloat16

D_MODEL = 2048
RMS_EPS = 1e-6
LOG2_E = 1.4426950408889634
NEG_INF = -1e30
FORCE_SCORE = 1e9

GLA_HEADS = 4
GLA_DK = D_MODEL // 2
GLA_DV = D_MODEL
GLA_HK = GLA_DK // GLA_HEADS
GLA_HV = GLA_DV // GLA_HEADS
GLA_LOWRANK = 16
GLA_GATE_NORMALIZER = 16.0

NSA_HEAD_DIM = 128
NSA_HEAD_SHIFT = 7
NSA_HEADS = D_MODEL // NSA_HEAD_DIM
NSA_KV_GROUPS = 4
NSA_Q_PER_KV = NSA_HEADS // NSA_KV_GROUPS
NSA_WIDTH = NSA_HEADS * NSA_HEAD_DIM
NSA_KV_WIDTH = NSA_KV_GROUPS * NSA_HEAD_DIM
CMP_LEN = 32
CMP_STRIDE = 16
CMP_HIDDEN = 2 * NSA_HEAD_DIM
SEL_BLOCK = 64
SEL_SHIFT = 6
SEL_TOPK = 16
WINDOW = 512
ROPE_THETA = 500000.0
ROPE_DIM = NSA_HEAD_DIM // 4

SSD_D_INNER = 2 * D_MODEL
SSD_HEAD_DIM = 64
SSD_HEAD_SHIFT = 6
SSD_HEADS = SSD_D_INNER // SSD_HEAD_DIM
SSD_GROUPS = 8
SSD_HEADS_PER_GROUP = SSD_HEADS // SSD_GROUPS
SSD_D_STATE = 128
SSD_CONV = 4
SSD_GROUP_WIDTH = SSD_HEADS_PER_GROUP * SSD_HEAD_DIM
SSD_CONV_DIM = SSD_D_INNER + 2 * SSD_GROUPS * SSD_D_STATE

D_FF = ((8 * D_MODEL + 3 * 256 - 1) // (3 * 256)) * 256

LANES = 128
V7X_VMEM_LIMIT_BYTES = 56 * 1024 * 1024

_REF_SIZES = (3 * D_MODEL, GLA_DK, GLA_DK, GLA_DV, GLA_LOWRANK, GLA_DV,
              NSA_WIDTH, 6 * NSA_KV_WIDTH, 3 * NSA_HEADS,
              SSD_D_INNER, SSD_CONV_DIM, SSD_HEADS)
_REF_OFF = tuple(int(v) for v in np.concatenate([[0], np.cumsum(_REF_SIZES)]))
(_R_GATE, _R_GQ, _R_GK, _R_GV, _R_GLOW, _R_GR, _R_NQ, _R_NKV, _R_NG,
 _R_SZ, _R_SXBC, _R_SDT) = _REF_OFF[:-1]
PA_COLS = _R_GLOW
PB_START, PB_COLS = _R_GR, _R_NG - _R_GR
PC_START, PC_COLS = _R_SZ, _R_SDT - _R_SZ
A_GATE, A_GQ, A_GK, A_GV = _R_GATE, _R_GQ, _R_GK, _R_GV
B_GR, B_NQ, B_NKV = 0, _R_NQ - _R_GR, _R_NKV - _R_GR
C_SZ, C_SXBC = 0, _R_SXBC - _R_SZ
S_BLOCKS = (_R_GLOW // LANES, _R_NG // LANES, _R_SDT // LANES)
S_GLOW = _R_GLOW % LANES
S_NG = LANES + _R_NG % LANES
S_DT = 2 * LANES + _R_SDT % LANES
PROJ_TILE_M = 1024
PROJ_TILE_N = 1024
PROJ_CAST_ROWS = 256
MERGE_TILE_N = 512
FFN_TILE_F = 512


def _params(sem, vmem=V7X_VMEM_LIMIT_BYTES):
    return pltpu.CompilerParams(dimension_semantics=sem, vmem_limit_bytes=vmem)


def _sigmoid(x):
    return 1.0 / (1.0 + jnp.exp(-x))


def _silu(x):
    return x * _sigmoid(x)


def _softplus(x):
    return jnp.maximum(x, 0.0) + jnp.log(1.0 + jnp.exp(-jnp.abs(x)))


def _split3(x):
    hi = x.astype(BF16)
    r1 = x - hi.astype(F32)
    mid = r1.astype(BF16)
    lo = (r1 - mid.astype(F32)).astype(BF16)
    return hi, mid, lo


def _dot(a, b):
    return jnp.dot(a, b, preferred_element_type=F32)


def _dot_nt(a, b):
    return lax.dot_general(a, b, (((1,), (1,)), ((), ())), preferred_element_type=F32)


def _dot_tn(a, b):
    return lax.dot_general(a, b, (((0,), (0,)), ((), ())), preferred_element_type=F32)


def _sel_dot(sel, x):
    hi, mid, lo = _split3(x)
    return _dot(sel, hi) + _dot(sel, mid) + _dot(sel, lo)


def _sel_dot_left(x, sel):
    hi, mid, lo = _split3(x)
    return _dot(hi, sel) + _dot(mid, sel) + _dot(lo, sel)


def _rms_rows(x, gain):
    ms = jnp.mean(x * x, axis=-1, keepdims=True)
    return x * lax.rsqrt(ms + RMS_EPS) * gain


def _rms_cast_kernel(x_ref, g_ref, o_ref):
    o_ref[...] = _rms_rows(x_ref[...], g_ref[...]).astype(o_ref.dtype)


def _rms_cast(x2d, gain, tm):
    t, d = x2d.shape
    return pl.pallas_call(
        _rms_cast_kernel,
        out_shape=jax.ShapeDtypeStruct((t, d), BF16),
        grid=(t // tm,),
        in_specs=[pl.BlockSpec((tm, d), lambda i: (i, 0)),
                  pl.BlockSpec((1, d), lambda i: (0, 0))],
        out_specs=pl.BlockSpec((tm, d), lambda i: (i, 0)),
        compiler_params=_params(("parallel",)),
        name="rms_cast",
    )(x2d, gain.reshape(1, d))


def _proj_kernel(*refs, shift, residual):
    refs = list(refs)
    h_ref, wa_ref = refs[:2]
    wb_ref = refs[2] if shift else None
    res_ref = refs[-3] if residual else None
    o_ref, w_s = refs[-2:]
    tn = w_s.shape[1]

    @pl.when(pl.program_id(1) == 0)
    def _():
        for r0 in range(0, w_s.shape[0], PROJ_CAST_ROWS):
            rows = pl.ds(r0, PROJ_CAST_ROWS)
            w = wa_ref[rows, :]
            if shift:
                wcat = jnp.concatenate([w, wb_ref[rows, :]], axis=1)
                w = pltpu.roll(wcat, tn + LANES - shift, axis=1)[:, :tn]
            w_s[rows, :] = w.astype(BF16)

    y = _dot(h_ref[...], w_s[...])
    o_ref[...] = y + res_ref[...] if residual else y


def _proj(h, w_stack, layer, start, ncols, tm, tn, residual=None, name="in_proj"):
    t, d = h.shape
    shift = start % LANES
    base = (start - shift) // tn
    assert (start - shift) % tn == 0 and ncols % tn == 0
    in_specs = [pl.BlockSpec((tm, d), lambda j, i: (i, 0)),
                pl.BlockSpec((None, d, tn), lambda j, i: (layer, 0, base + j))]
    args = [h, w_stack]
    if shift:
        nxt = tn // LANES
        in_specs.append(pl.BlockSpec((None, d, LANES), lambda j, i: (layer, 0, (base + j + 1) * nxt)))
        args.append(w_stack)
    if residual is not None:
        in_specs.append(pl.BlockSpec((tm, tn), lambda j, i: (i, j)))
        args.append(residual)
    return pl.pallas_call(
        functools.partial(_proj_kernel, shift=shift, residual=residual is not None),
        out_shape=jax.ShapeDtypeStruct((t, ncols), F32),
        grid=(ncols // tn, t // tm),
        in_specs=in_specs,
        out_specs=pl.BlockSpec((tm, tn), lambda j, i: (i, j)),
        scratch_shapes=[pltpu.VMEM((d, tn), BF16)],
        compiler_params=_params(("parallel", "arbitrary")),
        name=name,
    )(*args)


def _proj_small_kernel(h_ref, w0_ref, w1_ref, w2_ref, o_ref, w_s):
    @pl.when(pl.program_id(0) == 0)
    def _():
        for b, w_ref in enumerate((w0_ref, w1_ref, w2_ref)):
            w_s[:, b * LANES:(b + 1) * LANES] = w_ref[...].astype(BF16)

    o_ref[...] = _dot(h_ref[...], w_s[...])


def _proj_small(h, w_stack, layer, tm):
    t, d = h.shape
    nb = len(S_BLOCKS)
    wspecs = [pl.BlockSpec((None, d, LANES), functools.partial(lambda i, blk: (layer, 0, blk), blk=blk))
              for blk in S_BLOCKS]
    return pl.pallas_call(
        _proj_small_kernel,
        out_shape=jax.ShapeDtypeStruct((t, nb * LANES), F32),
        grid=(t // tm,),
        in_specs=[pl.BlockSpec((tm, d), lambda i: (i, 0))] + wspecs,
        out_specs=pl.BlockSpec((tm, nb * LANES), lambda i: (i, 0)),
        scratch_shapes=[pltpu.VMEM((d, nb * LANES), BF16)],
        compiler_params=_params(("arbitrary",)),
        name="in_proj_small",
    )(h, w_stack, w_stack, w_stack)


def _merge_kernel(og_ref, on_ref, os_ref, g0_ref, g1_ref, g2_ref, wb0_ref, wb1_ref, wb2_ref, o_ref):
    m = (_sigmoid(g0_ref[...]) * _dot(og_ref[...], wb0_ref[...])
         + _sigmoid(g1_ref[...]) * _dot(on_ref[...], wb1_ref[...])
         + _sigmoid(g2_ref[...]) * _dot(os_ref[...], wb2_ref[...]))
    o_ref[...] = m.astype(o_ref.dtype)


def _merge(o_gla, o_nsa, o_ssd, pa, wb, tm, tn):
    t = o_gla.shape[0]
    d = D_MODEL
    nj = d // tn
    gate_blk0 = A_GATE // tn
    return pl.pallas_call(
        _merge_kernel,
        out_shape=jax.ShapeDtypeStruct((t, d), BF16),
        grid=(t // tm, nj),
        in_specs=[pl.BlockSpec((tm, GLA_DV), lambda i, j: (i, 0)),
                  pl.BlockSpec((tm, NSA_WIDTH), lambda i, j: (i, 0)),
                  pl.BlockSpec((tm, SSD_D_INNER), lambda i, j: (i, 0)),
                  pl.BlockSpec((tm, tn), lambda i, j: (i, gate_blk0 + j)),
                  pl.BlockSpec((tm, tn), lambda i, j: (i, gate_blk0 + nj + j)),
                  pl.BlockSpec((tm, tn), lambda i, j: (i, gate_blk0 + 2 * nj + j)),
                  pl.BlockSpec((GLA_DV, tn), lambda i, j: (0, j)),
                  pl.BlockSpec((NSA_WIDTH, tn), lambda i, j: (1, j)),
                  pl.BlockSpec((SSD_D_INNER, tn), lambda i, j: (1, j))],
        out_specs=pl.BlockSpec((tm, tn), lambda i, j: (i, j)),
        compiler_params=_params(("parallel", "arbitrary")),
        name="branch_merge",
    )(o_gla, o_nsa, o_ssd, pa, pa, pa, wb, wb, wb)


def _ffn_kernel(x_ref, g_ref, gf_ref, wg_ref, wu_ref, wd_ref, o_ref, h_ref, *, final_norm):
    j = pl.program_id(1)

    @pl.when(j == 0)
    def _():
        x = x_ref[...]
        h_ref[...] = _rms_rows(x, g_ref[...]).astype(BF16)
        o_ref[...] = x

    h = h_ref[...]
    a = _silu(_dot(h, wg_ref[...])) * _dot(h, wu_ref[...])
    o_ref[...] += _dot(a.astype(BF16), wd_ref[...])

    if final_norm:
        @pl.when(j == pl.num_programs(1) - 1)
        def _():
            o_ref[...] = _rms_rows(o_ref[...], gf_ref[...])


def _ffn(x2d, gain, gain_final, wg, wu, wd, tm, tf, final_norm):
    t, d = x2d.shape
    f = wg.shape[1]
    return pl.pallas_call(
        functools.partial(_ffn_kernel, final_norm=final_norm),
        out_shape=jax.ShapeDtypeStruct((t, d), F32),
        grid=(t // tm, f // tf),
        in_specs=[pl.BlockSpec((tm, d), lambda i, j: (i, 0)),
                  pl.BlockSpec((1, d), lambda i, j: (0, 0)),
                  pl.BlockSpec((1, d), lambda i, j: (0, 0)),
                  pl.BlockSpec((d, tf), lambda i, j: (0, j)),
                  pl.BlockSpec((d, tf), lambda i, j: (0, j)),
                  pl.BlockSpec((tf, d), lambda i, j: (j, 0))],
        out_specs=pl.BlockSpec((tm, d), lambda i, j: (i, 0)),
        scratch_shapes=[pltpu.VMEM((tm, d), BF16)],
        compiler_params=_params(("parallel", "arbitrary")),
        name="ffn",
    )(x2d, gain.reshape(1, d), gain_final.reshape(1, d), wg, wu, wd)


GLA_CHUNK = 128
GLA_SUB = 16
GLA_EXP_CAP = 88.0
GLA_HEADS_PER_STEP = 4
GLA_TILE = 256


def _gla_kernel(q_ref, k_ref, v_ref, r_ref, sm_ref, w2_ref, gb_ref, ng_ref, o_ref, st_ref,
                *, chunks_per_tile):
    @pl.when(pl.program_id(2) == 0)
    def _():
        st_ref[...] = jnp.zeros_like(st_ref)

    c_rows = GLA_CHUNK
    row = lax.broadcasted_iota(jnp.int32, (c_rows, c_rows), 0)
    col = lax.broadcasted_iota(jnp.int32, (c_rows, c_rows), 1)
    tri = (col <= row).astype(BF16)
    srow = lax.broadcasted_iota(jnp.int32, (GLA_SUB, c_rows), 0)
    scol = lax.broadcasted_iota(jnp.int32, (GLA_SUB, c_rows), 1)
    scale = GLA_HK ** -0.5

    smb = sm_ref[...].astype(BF16)
    work = []
    for h in range(GLA_HEADS_PER_STEP):
        kcols = slice(h * GLA_HK, (h + 1) * GLA_HK)
        vcols = slice(h * GLA_HV, (h + 1) * GLA_HV)
        ga = _dot(smb, w2_ref[:, kcols]) + gb_ref[:, kcols]
        log_a_all = -_softplus(-ga) * (1.0 / GLA_GATE_NORMALIZER)
        for c in range(chunks_per_tile):
            rows = slice(c * c_rows, (c + 1) * c_rows)
            bc = _sel_dot(tri, log_a_all[rows])
            q = q_ref[rows, kcols] * scale
            k = k_ref[rows, kcols]
            vb = v_ref[rows, vcols].astype(BF16)
            parts = []
            for i in range(c_rows // GLA_SUB):
                lo = i * GLA_SUB
                ref_pt = bc[lo - 1:lo, :] if i > 0 else jnp.zeros((1, GLA_HK), F32)
                kt = (k * jnp.exp(jnp.minimum(ref_pt - bc, GLA_EXP_CAP))).astype(BF16)
                qi = (q[lo:lo + GLA_SUB, :] * jnp.exp(bc[lo:lo + GLA_SUB, :] - ref_pt)).astype(BF16)
                sc = _dot_nt(qi, kt)
                parts.append(jnp.where(scol <= srow + lo, sc, 0.0))
            intra = _dot(jnp.concatenate(parts, axis=0).astype(BF16), vb)
            b_last = bc[c_rows - 1:c_rows, :]
            khat = (k * jnp.exp(b_last - bc)).astype(BF16)
            work.append((h, rows, vcols, (q * jnp.exp(bc)).astype(BF16), intra,
                         jnp.exp(b_last), _dot_tn(vb, khat)))

    states = [st_ref[h] for h in range(GLA_HEADS_PER_STEP)]
    for h, rows, vcols, qe, intra, decay, incr in work:
        out = _dot_nt(qe, states[h].astype(BF16)) + intra
        states[h] = states[h] * decay + incr
        y = _rms_rows(out, ng_ref[...]) * _silu(r_ref[rows, vcols])
        o_ref[rows, vcols] = y.astype(o_ref.dtype)
    for h in range(GLA_HEADS_PER_STEP):
        st_ref[h] = states[h]


def _gla(pa, pb, ps, w2pad, gate_b, norm_g, bsz, seq, tile):
    t = bsz * seq
    nt = seq // tile
    hps = GLA_HEADS_PER_STEP
    kw, vw = hps * GLA_HK, hps * GLA_HV
    qb, kb = A_GQ // kw, A_GK // kw
    vb, rb = A_GV // vw, B_GR // vw
    return pl.pallas_call(
        functools.partial(_gla_kernel, chunks_per_tile=tile // GLA_CHUNK),
        out_shape=jax.ShapeDtypeStruct((t, GLA_DV), BF16),
        grid=(bsz, GLA_HEADS // hps, nt),
        in_specs=[pl.BlockSpec((tile, kw), lambda b, h, c: (b * nt + c, qb + h)),
                  pl.BlockSpec((tile, kw), lambda b, h, c: (b * nt + c, kb + h)),
                  pl.BlockSpec((tile, vw), lambda b, h, c: (b * nt + c, vb + h)),
                  pl.BlockSpec((tile, vw), lambda b, h, c: (b * nt + c, rb + h)),
                  pl.BlockSpec((tile, LANES), lambda b, h, c: (b * nt + c, 0)),
                  pl.BlockSpec((LANES, kw), lambda b, h, c: (0, h)),
                  pl.BlockSpec((1, kw), lambda b, h, c: (0, h)),
                  pl.BlockSpec((1, GLA_HV), lambda b, h, c: (0, 0))],
        out_specs=pl.BlockSpec((tile, vw), lambda b, h, c: (b * nt + c, h)),
        scratch_shapes=[pltpu.VMEM((hps, GLA_HV, GLA_HK), F32)],
        compiler_params=_params(("parallel", "parallel", "arbitrary")),
        name="gla_mixer",
    )(pa, pa, pa, pb, ps, w2pad, gate_b.reshape(1, GLA_DK), norm_g.reshape(1, GLA_HV))


SSD_STEP_ROWS = 1024
SSD_SUB = 128
SSD_HIST = 8


def _causal_conv(e_ref, src_ref, w_ref, b_ref):
    n = src_ref.shape[0]
    e_ref[pl.ds(SSD_HIST, n), :] = src_ref[...]
    acc = src_ref[...] * w_ref[SSD_CONV - 1:SSD_CONV, :] + b_ref[...]
    for s in range(1, SSD_CONV):
        acc = acc + e_ref[pl.ds(SSD_HIST - s, n), :] * w_ref[SSD_CONV - 1 - s:SSD_CONV - s, :]
    e_ref[pl.ds(0, SSD_HIST), :] = src_ref[pl.ds(n - SSD_HIST, SSD_HIST), :]
    return _silu(acc)


def _ssd_kernel(xs_ref, bm_ref, cm_ref, z_ref, dtc_ref, dtr_ref,
                wx_ref, wb_ref, wc_ref, bx_ref, bb_ref, bc_ref,
                dbc_ref, dbr_ref, alc_ref, alr_ref, dsk_ref, ng_ref,
                o_ref, st_ref, ex_ref, eb_ref, ec_ref):
    n = xs_ref.shape[0]
    sub = min(SSD_SUB, n)
    hpg, hd = SSD_HEADS_PER_GROUP, SSD_HEAD_DIM

    @pl.when(pl.program_id(2) == 0)
    def _():
        st_ref[...] = jnp.zeros_like(st_ref)
        ex_ref[pl.ds(0, SSD_HIST), :] = jnp.zeros((SSD_HIST, ex_ref.shape[1]), F32)
        eb_ref[pl.ds(0, SSD_HIST), :] = jnp.zeros((SSD_HIST, eb_ref.shape[1]), F32)
        ec_ref[pl.ds(0, SSD_HIST), :] = jnp.zeros((SSD_HIST, ec_ref.shape[1]), F32)

    xs = _causal_conv(ex_ref, xs_ref, wx_ref, bx_ref)
    bmb = _causal_conv(eb_ref, bm_ref, wb_ref, bb_ref).astype(BF16)
    cmb = _causal_conv(ec_ref, cm_ref, wc_ref, bc_ref).astype(BF16)

    w3 = 3 * hpg
    krow = lax.broadcasted_iota(jnp.int32, (w3, hpg * hd), 0)
    kcol = lax.broadcasted_iota(jnp.int32, (w3, hpg * hd), 1) >> SSD_HEAD_SHIFT
    expand3 = ((krow == kcol) | (krow == kcol + hpg) | (krow == kcol + 2 * hpg)).astype(BF16)

    def expand_heads(x):
        hi, mid, lo = _split3(x)
        klane = lax.broadcasted_iota(jnp.int32, x.shape, 1)
        return _dot(jnp.where(klane < hpg, hi, jnp.where(klane < 2 * hpg, mid, lo)), expand3)

    dt_c = _softplus(dtc_ref[0] + dbc_ref[0])
    a_c = -jnp.exp(alc_ref[0])
    a_r = -jnp.exp(alr_ref[0])
    xdt = xs * expand_heads(dt_c)

    row = lax.broadcasted_iota(jnp.int32, (sub, sub), 0)
    col = lax.broadcasted_iota(jnp.int32, (sub, sub), 1)
    tri = (col <= row).astype(BF16)
    triu = (row <= col).astype(BF16)
    causal_bias = jnp.where(col <= row, 0.0, NEG_INF)
    lane = lax.broadcasted_iota(jnp.int32, (sub, 2 * hd), 1)

    for si in range(n // sub):
        rows = slice(si * sub, (si + 1) * sub)
        acum_c = _sel_dot(tri, dt_c[rows] * a_c)
        adt_r = _softplus(dtr_ref[0, si] + dbr_ref[0]) * a_r
        acum_r = _sel_dot_left(adt_r, triu)
        a_last = acum_c[sub - 1:sub, :]
        eac_x = expand_heads(jnp.exp(acum_c))
        dec_x = expand_heads(jnp.exp(a_last - acum_c))
        sdec_x = expand_heads(jnp.broadcast_to(jnp.exp(a_last), (8, w3)))[0:1, :]
        ac2 = acum_c * LOG2_E
        ar2 = acum_r * LOG2_E

        xd, cs, bs = xdt[rows], cmb[rows], bmb[rows]
        cb = _dot_nt(cs, bs)
        pair_out = []
        for pr in range(hpg // 2):
            xp = xd[:, pr * 2 * hd:(pr + 1) * 2 * hd]
            acc = None
            for half in range(2):
                r = 2 * pr + half
                seg = jnp.exp2(ac2[:, r:r + 1] - ar2[r:r + 1, :] + causal_bias)
                rhs = jnp.where((lane >> SSD_HEAD_SHIFT) == half, xp, 0.0).astype(BF16)
                term = _dot((cb * seg).astype(BF16), rhs)
                acc = term if acc is None else acc + term
            pair_out.append(acc)
        y_diag = jnp.concatenate(pair_out, axis=1)

        st = st_ref[...]
        y_off = _dot(cs, st.astype(BF16)) * eac_x
        st_ref[...] = st * sdec_x + _dot_tn(bs, (xd * dec_x).astype(BF16))

        y = (y_diag + y_off + xs[rows] * dsk_ref[0]) * _silu(z_ref[rows, :])
        o_ref[rows, :] = _rms_rows(y, ng_ref[...]).astype(o_ref.dtype)


def _ssd(pc, dt_c, dt_r, conv_w, conv_b, dt_bias, a_log, d_skip, norm_g, bsz, seq, chunk):
    t = bsz * seq
    nc = seq // chunk
    sub = min(SSD_SUB, chunk)
    g, hpg, gw, ns = SSD_GROUPS, SSD_HEADS_PER_GROUP, SSD_GROUP_WIDTH, SSD_D_STATE
    xsb = C_SXBC // gw
    bmb = (C_SXBC + SSD_D_INNER) // ns
    cmb = (C_SXBC + SSD_D_INNER + g * ns) // ns
    zb = C_SZ // gw
    cw_b = SSD_D_INNER // ns
    cw_c = (SSD_D_INNER + g * ns) // ns
    conv_b2 = conv_b.reshape(1, SSD_CONV_DIM)
    dbc = jnp.tile(dt_bias.reshape(g, 1, hpg), (1, 1, 3))
    dbr = dt_bias.reshape(g, hpg, 1)
    alc = jnp.tile(a_log.reshape(g, 1, hpg), (1, 1, 3))
    alr = a_log.reshape(g, hpg, 1)
    dsk = jnp.repeat(d_skip, SSD_HEAD_DIM).reshape(g, 1, gw)
    ng = norm_g.reshape(1, SSD_D_INNER)
    return pl.pallas_call(
        _ssd_kernel,
        out_shape=jax.ShapeDtypeStruct((t, SSD_D_INNER), BF16),
        grid=(bsz, g, nc),
        in_specs=[pl.BlockSpec((chunk, gw), lambda b, gi, c: (b * nc + c, xsb + gi)),
                  pl.BlockSpec((chunk, ns), lambda b, gi, c: (b * nc + c, bmb + gi)),
                  pl.BlockSpec((chunk, ns), lambda b, gi, c: (b * nc + c, cmb + gi)),
                  pl.BlockSpec((chunk, gw), lambda b, gi, c: (b * nc + c, zb + gi)),
                  pl.BlockSpec((1, chunk, 3 * hpg), lambda b, gi, c: (gi, b * nc + c, 0)),
                  pl.BlockSpec((1, chunk // sub, hpg, sub), lambda b, gi, c: (gi, b * nc + c, 0, 0)),
                  pl.BlockSpec((SSD_CONV, gw), lambda b, gi, c: (0, gi)),
                  pl.BlockSpec((SSD_CONV, ns), lambda b, gi, c: (0, cw_b + gi)),
                  pl.BlockSpec((SSD_CONV, ns), lambda b, gi, c: (0, cw_c + gi)),
                  pl.BlockSpec((1, gw), lambda b, gi, c: (0, gi)),
                  pl.BlockSpec((1, ns), lambda b, gi, c: (0, cw_b + gi)),
                  pl.BlockSpec((1, ns), lambda b, gi, c: (0, cw_c + gi)),
                  pl.BlockSpec((1, 1, 3 * hpg), lambda b, gi, c: (gi, 0, 0)),
                  pl.BlockSpec((1, hpg, 1), lambda b, gi, c: (gi, 0, 0)),
                  pl.BlockSpec((1, 1, 3 * hpg), lambda b, gi, c: (gi, 0, 0)),
                  pl.BlockSpec((1, hpg, 1), lambda b, gi, c: (gi, 0, 0)),
                  pl.BlockSpec((1, 1, gw), lambda b, gi, c: (gi, 0, 0)),
                  pl.BlockSpec((1, gw), lambda b, gi, c: (0, gi))],
        out_specs=pl.BlockSpec((chunk, gw), lambda b, gi, c: (b * nc + c, gi)),
        scratch_shapes=[pltpu.VMEM((ns, gw), F32),
                        pltpu.VMEM((SSD_HIST + chunk, gw), F32),
                        pltpu.VMEM((SSD_HIST + chunk, ns), F32),
                        pltpu.VMEM((SSD_HIST + chunk, ns), F32)],
        compiler_params=_params(("parallel", "parallel", "arbitrary")),
        name="ssd_mixer",
    )(pc, pc, pc, pc, dt_c, dt_r, conv_w, conv_w, conv_w, conv_b2, conv_b2, conv_b2,
      dbc, dbr, alc, alr, dsk, ng)


NSA_TQ = 256
NSA_KB = 512


def _rope(x, cos, sin):
    lane = lax.broadcasted_iota(jnp.int32, x.shape, 1)
    half = ROPE_DIM // 2
    swapped = jnp.where(lane < half, pltpu.roll(x, LANES - half, axis=1), pltpu.roll(x, half, axis=1))
    return x * cos + swapped * sin


def _compress(src_ref, pe_ref, w1_ref, w2_ref, nblk):
    hd = NSA_HEAD_DIM
    a0 = jnp.zeros((nblk, CMP_HIDDEN), F32)
    a1 = jnp.zeros((nblk, CMP_HIDDEN), F32)
    for c in range(CMP_STRIDE):
        xc = src_ref[pl.ds(c, nblk, stride=CMP_STRIDE), :]
        a0 = a0 + _dot((xc + pe_ref[c:c + 1, :]).astype(BF16), w1_ref[c * hd:(c + 1) * hd, :])
        c1 = CMP_STRIDE + c
        a1 = a1 + _dot((xc + pe_ref[c1:c1 + 1, :]).astype(BF16), w1_ref[c1 * hd:(c1 + 1) * hd, :])
    hid = a0 + pltpu.roll(a1, nblk - 1, axis=0)
    return _dot(_silu(hid).astype(BF16), w2_ref[...])


def _nsa_kernel(q_ref, kc_ref, vc_ref, ks_ref, vs_ref, kw_ref, vw_ref, g_ref,
                cos_ref, sin_ref, pek_ref, pev_ref, w1k_ref, w2k_ref, w1v_ref, w2v_ref, ovt_ref,
                o_ref, kcmp_s, vcmp_s, ksel_s, vsel_s, kwin_s, vwin_s, krope_s,
                sbuf_s, mx_s, acc_s, *, seq):
    tq, hd, nr = NSA_TQ, NSA_HEAD_DIM, NSA_Q_PER_KV
    i = pl.program_id(2)
    nblk = seq // CMP_STRIDE
    nsel = seq // SEL_BLOCK
    scale = hd ** -0.5

    @pl.when(i == 0)
    def _():
        cos, sin = cos_ref[...], sin_ref[...]
        krope_s[...] = _rope(kc_ref[...], cos, sin)
        ksel_s[...] = _rope(ks_ref[...], cos, sin).astype(BF16)
        kwin_s[...] = _rope(kw_ref[...], cos, sin).astype(BF16)
        vsel_s[:, :hd] = vs_ref[...].astype(BF16)
        vsel_s[:, hd:] = jnp.ones((seq, hd), BF16)
        vwin_s[:, :hd] = vw_ref[...].astype(BF16)
        vwin_s[:, hd:] = jnp.ones((seq, hd), BF16)
        kcmp_s[...] = _compress(krope_s, pek_ref, w1k_ref, w2k_ref, nblk).astype(BF16)
        vcmp_s[:, :hd] = _compress(vc_ref, pev_ref, w1v_ref, w2v_ref, nblk).astype(BF16)
        vcmp_s[:, hd:] = jnp.ones((nblk, hd), BF16)

    t0 = pl.multiple_of(i * tq, tq)
    cos = cos_ref[pl.ds(t0, tq), :]
    sin = sin_ref[pl.ds(t0, tq), :]
    qs = [_rope(q_ref[:, r * hd:(r + 1) * hd], cos, sin) for r in range(nr)]
    qb = jnp.concatenate(qs, axis=0).astype(BF16)
    c2 = scale * LOG2_E
    tpos = t0 + lax.broadcasted_iota(jnp.int32, (tq, 1), 0)

    ncol = lax.broadcasted_iota(jnp.int32, (1, nblk), 1)
    ok = (ncol * CMP_STRIDE + (CMP_LEN - 1)) <= tpos
    bias_c = jnp.where(ok, 0.0, NEG_INF)
    s = _dot_nt(qb, kcmp_s[...]).reshape(nr, tq, nblk) + bias_c[None]
    e = jnp.exp2((s - jnp.max(s, axis=-1, keepdims=True)) * c2)
    oc = _dot(e.reshape(nr * tq, nblk).astype(BF16), vcmp_s[...]).reshape(nr, tq, 2 * hd)
    inv_lc = 1.0 / oc[:, :, hd:]
    has_cmp = (tpos >= CMP_LEN - 1)[None]
    o_cmp = jnp.where(has_cmp, oc[:, :, :hd] * inv_lc, 0.0)
    p_cmp = jnp.where(ok[None], e * inv_lc[:, :, :nblk], 0.0)

    p_sum = p_cmp[0]
    for r in range(1, nr):
        p_sum = p_sum + p_cmp[r]
    ph = p_sum.astype(BF16)
    pm = (p_sum - ph.astype(F32)).astype(BF16)
    ovt = ovt_ref[...]
    p_slc = _dot_nt(ovt, ph) + _dot_nt(ovt, pm)
    jrow = lax.broadcasted_iota(jnp.int32, (nsel, tq), 0)
    blk_t = (t0 + lax.broadcasted_iota(jnp.int32, (nsel, tq), 1)) >> SEL_SHIFT
    forced = (jrow == 0) | (jrow == blk_t) | (jrow == blk_t - 1)
    score = jnp.where(forced, FORCE_SCORE, jnp.where(jrow <= blk_t, p_slc, NEG_INF))
    rank = jnp.zeros((nsel, tq), F32)
    for j in range(nsel):
        sj = score[j:j + 1, :]
        beats = jnp.where(sj > score, 1.0, jnp.where((sj == score) & (jrow > j), 1.0, 0.0))
        rank = rank + beats
    sel_t = jnp.where(rank < float(min(SEL_TOPK, nsel)), 1.0, 0.0)
    sel = jnp.transpose(sel_t).astype(BF16)

    wlen = min(WINDOW + tq, seq)
    w0 = pl.multiple_of(jnp.maximum(t0 + tq - wlen, 0), tq)
    ktw = kwin_s[pl.ds(w0, wlen), :]
    vtw = vwin_s[pl.ds(w0, wlen), :]
    kposw = w0 + lax.broadcasted_iota(jnp.int32, (1, wlen), 1)
    bias_w = jnp.where((kposw <= tpos) & (kposw > tpos - WINDOW), 0.0, NEG_INF)
    sw = _dot_nt(qb, ktw).reshape(nr, tq, wlen) + bias_w[None]
    ew = jnp.exp2((sw - jnp.max(sw, axis=-1, keepdims=True)) * c2)
    ow = _dot(ew.reshape(nr * tq, wlen).astype(BF16), vtw).reshape(nr, tq, 2 * hd)
    o_win = ow[:, :, :hd] * (1.0 / ow[:, :, hd:])

    ng = nr * 3
    gs = _sigmoid(g_ref[0])
    g_hi, g_mid, g_lo = _split3(gs)
    glane = lax.broadcasted_iota(jnp.int32, gs.shape, 1)
    g_terms = jnp.where(glane < ng, g_hi, jnp.where(glane < 2 * ng, g_mid, g_lo))
    grow = lax.broadcasted_iota(jnp.int32, (3 * ng, ng * hd), 0)
    gcol = lax.broadcasted_iota(jnp.int32, (3 * ng, ng * hd), 1) >> NSA_HEAD_SHIFT
    gsel = (grow == gcol) | (grow == gcol + ng) | (grow == gcol + 2 * ng)
    gexp = _dot(g_terms, gsel.astype(BF16))
    gates = [gexp[:, j * hd:(j + 1) * hd] for j in range(ng)]
    o_cw = [gates[3 * r] * o_cmp[r] + gates[3 * r + 2] * o_win[r] for r in range(nr)]

    kb = min(NSA_KB, seq)
    n_kb = (t0 + tq + kb - 1) // kb
    nlb = kb // LANES
    mx_s[...] = jnp.full(mx_s.shape, NEG_INF, F32)
    acc_s[...] = jnp.zeros(acc_s.shape, F32)

    def lane_blocks(x):
        return [x[:, b * LANES:(b + 1) * LANES] for b in range(nlb)]

    def score_step(c, carry):
        k0 = pl.multiple_of(c * kb, kb)
        erow = lax.broadcasted_iota(jnp.int32, (nsel, kb), 0)
        ecol = lax.broadcasted_iota(jnp.int32, (nsel, kb), 1)
        expand = (((k0 + ecol) >> SEL_SHIFT) == erow).astype(BF16)
        chosen = _dot(sel, expand)
        kpos = k0 + lax.broadcasted_iota(jnp.int32, (1, kb), 1)
        bias = jnp.where((chosen > 0.5) & (kpos <= tpos), 0.0, NEG_INF)
        sb = _dot_nt(qb, ksel_s[pl.ds(k0, kb), :]).reshape(nr, tq, kb) + bias[None]
        sb = sb.reshape(nr * tq, kb)
        sbuf_s[c] = sb
        mx = mx_s[...]
        for blk in lane_blocks(sb):
            mx = jnp.maximum(mx, blk)
        mx_s[...] = mx
        return carry

    lax.fori_loop(0, n_kb, score_step, 0)
    m_sel = jnp.max(mx_s[...], axis=-1, keepdims=True)

    def prob_step(c, carry):
        k0 = pl.multiple_of(c * kb, kb)
        pt = jnp.exp2((sbuf_s[c] - m_sel) * c2)
        acc_s[...] += _dot(pt.astype(BF16), vsel_s[pl.ds(k0, kb), :])
        return carry

    lax.fori_loop(0, n_kb, prob_step, 0)
    o_sel = (acc_s[:, :hd] * (1.0 / acc_s[:, hd:])).reshape(nr, tq, hd)

    for r in range(nr):
        o = o_cw[r] + gates[3 * r + 1] * o_sel[r]
        o_ref[:, r * hd:(r + 1) * hd] = o.astype(o_ref.dtype)


def _nsa_tables(seq):
    half = ROPE_DIM // 2
    inv_freq = ROPE_THETA ** (-jnp.arange(half, dtype=F32) / half)
    ang = jnp.arange(seq).astype(F32)[:, None] * inv_freq[None, :]
    cos, sin = jnp.cos(ang), jnp.sin(ang)
    rest = NSA_HEAD_DIM - ROPE_DIM
    cos_t = jnp.concatenate([cos, cos, jnp.ones((seq, rest), F32)], axis=1)
    sin_t = jnp.concatenate([-sin, sin, jnp.zeros((seq, rest), F32)], axis=1)
    n_sel = seq // SEL_BLOCK
    cmp_starts = np.arange(seq // CMP_STRIDE) * CMP_STRIDE
    sel_starts = np.arange(n_sel) * SEL_BLOCK
    overlap = np.clip(np.minimum(cmp_starts[:, None] + CMP_LEN, sel_starts[None, :] + SEL_BLOCK)
                      - np.maximum(cmp_starts[:, None], sel_starts[None, :]), 0, None).astype(np.float32) / CMP_LEN
    return cos_t, sin_t, jnp.asarray(overlap.T, BF16)


def _nsa(pb, gates, pos_k, pos_v, w1k, w2k, w1v, w2v, bsz, seq):
    t = bsz * seq
    tq, hd, g, nr = NSA_TQ, NSA_HEAD_DIM, NSA_KV_GROUPS, NSA_Q_PER_KV
    nq = seq // tq
    nblk = seq // CMP_STRIDE
    nsel = seq // SEL_BLOCK
    cos_t, sin_t, ovt = _nsa_tables(seq)
    kb = min(NSA_KB, seq)
    qblk = B_NQ // (nr * hd)
    kvb = B_NKV // hd

    def kv_spec(split):
        return pl.BlockSpec((seq, hd), lambda b, gi, i: (b, kvb + split * g + gi))

    const2 = lambda b, gi, i: (0, 0)
    return pl.pallas_call(
        functools.partial(_nsa_kernel, seq=seq),
        out_shape=jax.ShapeDtypeStruct((t, NSA_WIDTH), BF16),
        grid=(bsz, g, nq),
        in_specs=[pl.BlockSpec((tq, nr * hd), lambda b, gi, i: (b * nq + i, qblk + gi))]
                 + [kv_spec(sp) for sp in range(6)]
                 + [pl.BlockSpec((1, tq, nr * 9), lambda b, gi, i: (gi, b * nq + i, 0)),
                    pl.BlockSpec((seq, hd), const2),
                    pl.BlockSpec((seq, hd), const2),
                    pl.BlockSpec((CMP_LEN, hd), const2),
                    pl.BlockSpec((CMP_LEN, hd), const2),
                    pl.BlockSpec((CMP_LEN * hd, CMP_HIDDEN), const2),
                    pl.BlockSpec((CMP_HIDDEN, hd), const2),
                    pl.BlockSpec((CMP_LEN * hd, CMP_HIDDEN), const2),
                    pl.BlockSpec((CMP_HIDDEN, hd), const2),
                    pl.BlockSpec((nsel, nblk), const2)],
        out_specs=pl.BlockSpec((tq, nr * hd), lambda b, gi, i: (b * nq + i, gi)),
        scratch_shapes=[pltpu.VMEM((nblk, hd), BF16), pltpu.VMEM((nblk, 2 * hd), BF16),
                        pltpu.VMEM((seq, hd), BF16), pltpu.VMEM((seq, 2 * hd), BF16),
                        pltpu.VMEM((seq, hd), BF16), pltpu.VMEM((seq, 2 * hd), BF16),
                        pltpu.VMEM((seq, hd), F32),
                        pltpu.VMEM((seq // kb, nr * tq, kb), F32),
                        pltpu.VMEM((nr * tq, LANES), F32),
                        pltpu.VMEM((nr * tq, 2 * hd), F32)],
        compiler_params=_params(("parallel", "parallel", "arbitrary")),
        name="nsa_mixer",
    )(pb, pb, pb, pb, pb, pb, pb, gates, cos_t, sin_t, pos_k, pos_v, w1k, w2k, w1v, w2v, ovt)


def _in_proj(x2d, gain, w_in_stack, layer):
    t = x2d.shape[0]
    tm = min(PROJ_TILE_M, t)
    h = _rms_cast(x2d, gain, tm)
    pa = _proj(h, w_in_stack, layer, 0, PA_COLS, tm, PROJ_TILE_N)
    pb = _proj(h, w_in_stack, layer, PB_START, PB_COLS, tm, PROJ_TILE_N)
    pc = _proj(h, w_in_stack, layer, PC_START, PC_COLS, tm, PROJ_TILE_N)
    ps = _proj_small(h, w_in_stack, layer, tm)
    return pa, pb, pc, ps


def _small_views(ps):
    t = ps.shape[0]
    nsa_g = ps[:, S_NG:S_NG + 3 * NSA_HEADS]
    nsa_g = nsa_g.reshape(t, NSA_KV_GROUPS, NSA_Q_PER_KV * 3).transpose(1, 0, 2)
    nsa_g = jnp.tile(nsa_g, (1, 1, 3))
    dt = ps[:, S_DT:S_DT + SSD_HEADS].reshape(t, SSD_GROUPS, SSD_HEADS_PER_GROUP)
    sub = min(SSD_SUB, t)
    dt_r = dt.reshape(t // sub, sub, SSD_GROUPS, SSD_HEADS_PER_GROUP).transpose(2, 0, 3, 1)
    return nsa_g, jnp.tile(dt.transpose(1, 0, 2), (1, 1, 3)), dt_r


def _layer(x2d, bsz, seq, layer, norm_mix, w_in, gla_gate_w2, gla_gate_b, gla_out_norm,
           nsa_cmp_pos_k, nsa_cmp_pos_v, nsa_cmp_k_w1, nsa_cmp_k_w2, nsa_cmp_v_w1, nsa_cmp_v_w2,
           ssd_conv_w, ssd_conv_b, ssd_dt_bias, ssd_a_log, ssd_d, ssd_out_norm,
           w_branch, w_out, norm_ffn, w_ffn_gate, w_ffn_up, w_ffn_down, norm_final, final_norm):
    t = bsz * seq
    pa, pb, pc, ps = _in_proj(x2d, norm_mix, w_in, layer)
    nsa_g, dt_c, dt_r = _small_views(ps)

    w2pad = jnp.zeros((LANES, GLA_DK), BF16).at[S_GLOW:S_GLOW + GLA_LOWRANK].set(
        gla_gate_w2.astype(BF16))
    o_gla = _gla(pa, pb, ps, w2pad, gla_gate_b, gla_out_norm, bsz, seq, min(GLA_TILE, seq))
    o_nsa = _nsa(pb, nsa_g, nsa_cmp_pos_k, nsa_cmp_pos_v,
                 nsa_cmp_k_w1.astype(BF16), nsa_cmp_k_w2.astype(BF16),
                 nsa_cmp_v_w1.astype(BF16), nsa_cmp_v_w2.astype(BF16), bsz, seq)
    o_ssd = _ssd(pc, dt_c, dt_r, ssd_conv_w, ssd_conv_b, ssd_dt_bias, ssd_a_log, ssd_d,
                 ssd_out_norm, bsz, seq, min(SSD_STEP_ROWS, seq))

    merged = _merge(o_gla, o_nsa, o_ssd, pa, w_branch.astype(BF16), min(512, t), MERGE_TILE_N)
    tm = min(PROJ_TILE_M, t)
    x2d = _proj(merged, w_out, layer, 0, D_MODEL, tm, PROJ_TILE_N, residual=x2d, name="out_proj")
    return _ffn(x2d, norm_ffn, norm_final, w_ffn_gate.astype(BF16), w_ffn_up.astype(BF16),
                w_ffn_down.astype(BF16), min(512, t), FFN_TILE_F, final_norm)


def kernel(x, norm_mix, w_in, gla_gate_w2, gla_gate_b, gla_out_norm, nsa_cmp_pos_k, nsa_cmp_pos_v,
           nsa_cmp_k_w1, nsa_cmp_k_w2, nsa_cmp_v_w1, nsa_cmp_v_w2, ssd_conv_w, ssd_conv_b,
           ssd_dt_bias, ssd_a_log, ssd_d, ssd_out_norm, w_branch, w_out, norm_ffn, w_ffn_gate,
           w_ffn_up, w_ffn_down, norm_final):
    bsz, seq, d = x.shape
    depth = norm_mix.shape[0]
    x2d = x.reshape(bsz * seq, d)
    for l in range(depth):
        x2d = _layer(x2d, bsz, seq, l, norm_mix[l], w_in, gla_gate_w2[l], gla_gate_b[l],
                     gla_out_norm[l], nsa_cmp_pos_k[l], nsa_cmp_pos_v[l], nsa_cmp_k_w1[l],
                     nsa_cmp_k_w2[l], nsa_cmp_v_w1[l], nsa_cmp_v_w2[l], ssd_conv_w[l],
                     ssd_conv_b[l], ssd_dt_bias[l], ssd_a_log[l], ssd_d[l], ssd_out_norm[l],
                     w_branch[l], w_out, norm_ffn[l], w_ffn_gate[l], w_ffn_up[l],
                     w_ffn_down[l], norm_final, l == depth - 1)
    return x2d.reshape(bsz, seq, d)
```

```python
import functools

import numpy as np
import jax
import jax.numpy as jnp
from jax import lax
from jax.experimental import pallas as pl
from jax.experimental.pallas import tpu as pltpu

F32 = jnp.float32
BF16 = jnp.bfloat16

D_MODEL = 2048
RMS_EPS = 1e-6
LOG2_E = 1.4426950408889634
NEG_INF = -1e30
FORCE_SCORE = 1e9

GLA_HEADS = 4
GLA_DK = D_MODEL // 2
GLA_DV = D_MODEL
GLA_HK = GLA_DK // GLA_HEADS
GLA_HV = GLA_DV // GLA_HEADS
GLA_LOWRANK = 16
GLA_GATE_NORMALIZER = 16.0

NSA_HEAD_DIM = 128
NSA_HEAD_SHIFT = 7
NSA_HEADS = D_MODEL // NSA_HEAD_DIM
NSA_KV_GROUPS = 4
NSA_Q_PER_KV = NSA_HEADS // NSA_KV_GROUPS
NSA_WIDTH = NSA_HEADS * NSA_HEAD_DIM
NSA_KV_WIDTH = NSA_KV_GROUPS * NSA_HEAD_DIM
CMP_LEN = 32
CMP_STRIDE = 16
CMP_HIDDEN = 2 * NSA_HEAD_DIM
SEL_BLOCK = 64
SEL_SHIFT = 6
SEL_TOPK = 16
WINDOW = 512
ROPE_THETA = 500000.0
ROPE_DIM = NSA_HEAD_DIM // 4

SSD_D_INNER = 2 * D_MODEL
SSD_HEAD_DIM = 64
SSD_HEAD_SHIFT = 6
SSD_HEADS = SSD_D_INNER // SSD_HEAD_DIM
SSD_GROUPS = 8
SSD_HEADS_PER_GROUP = SSD_HEADS // SSD_GROUPS
SSD_D_STATE = 128
SSD_CONV = 4
SSD_GROUP_WIDTH = SSD_HEADS_PER_GROUP * SSD_HEAD_DIM
SSD_CONV_DIM = SSD_D_INNER + 2 * SSD_GROUPS * SSD_D_STATE

D_FF = ((8 * D_MODEL + 3 * 256 - 1) // (3 * 256)) * 256

LANES = 128
V7X_VMEM_LIMIT_BYTES = 56 * 1024 * 1024

_REF_SIZES = (3 * D_MODEL, GLA_DK, GLA_DK, GLA_DV, GLA_LOWRANK, GLA_DV,
              NSA_WIDTH, 6 * NSA_KV_WIDTH, 3 * NSA_HEADS,
              SSD_D_INNER, SSD_CONV_DIM, SSD_HEADS)
_REF_OFF = tuple(int(v) for v in np.concatenate([[0], np.cumsum(_REF_SIZES)]))
(_R_GATE, _R_GQ, _R_GK, _R_GV, _R_GLOW, _R_GR, _R_NQ, _R_NKV, _R_NG,
 _R_SZ, _R_SXBC, _R_SDT) = _REF_OFF[:-1]
PA_COLS = _R_GLOW
PB_START, PB_COLS = _R_GR, _R_NG - _R_GR
PC_START, PC_COLS = _R_SZ, _R_SDT - _R_SZ
A_GATE, A_GQ, A_GK, A_GV = _R_GATE, _R_GQ, _R_GK, _R_GV
B_GR, B_NQ, B_NKV = 0, _R_NQ - _R_GR, _R_NKV - _R_GR
C_SZ, C_SXBC = 0, _R_SXBC - _R_SZ
S_BLOCKS = (_R_GLOW // LANES, _R_NG // LANES, _R_SDT // LANES)
S_GLOW = _R_GLOW % LANES
S_NG = LANES + _R_NG % LANES
S_DT = 2 * LANES + _R_SDT % LANES
PROJ_TILE_M = 1024
PROJ_TILE_N = 1024
PROJ_CAST_ROWS = 256
MERGE_TILE_M = 512
MERGE_TILE_N = 512
FFN_TILE_M = 512
FFN_TILE_F = 512


def _params(sem, vmem=V7X_VMEM_LIMIT_BYTES):
    return pltpu.CompilerParams(dimension_semantics=sem, vmem_limit_bytes=vmem)


def _sigmoid(x):
    return 1.0 / (1.0 + jnp.exp(-x))


def _silu(x):
    return x * _sigmoid(x)


def _softplus(x):
    return jnp.maximum(x, 0.0) + jnp.log(1.0 + jnp.exp(-jnp.abs(x)))


def _split3(x):
    hi = x.astype(BF16)
    r1 = x - hi.astype(F32)
    mid = r1.astype(BF16)
    lo = (r1 - mid.astype(F32)).astype(BF16)
    return hi, mid, lo


def _dot(a, b):
    return jnp.dot(a, b, preferred_element_type=F32)


def _dot_nt(a, b):
    return lax.dot_general(a, b, (((1,), (1,)), ((), ())), preferred_element_type=F32)


def _dot_tn(a, b):
    return lax.dot_general(a, b, (((0,), (0,)), ((), ())), preferred_element_type=F32)


def _sel_dot(sel, x):
    hi, mid, lo = _split3(x)
    return _dot(sel, hi) + _dot(sel, mid) + _dot(sel, lo)


def _sel_dot_left(x, sel):
    hi, mid, lo = _split3(x)
    return _dot(hi, sel) + _dot(mid, sel) + _dot(lo, sel)


def _rms_rows(x, gain):
    ms = jnp.mean(x * x, axis=-1, keepdims=True)
    return x * lax.rsqrt(ms + RMS_EPS) * gain


def _rms_cast_kernel(x_ref, g_ref, o_ref):
    o_ref[...] = _rms_rows(x_ref[...], g_ref[...]).astype(o_ref.dtype)


def _rms_cast(x2d, gain, tm):
    t, d = x2d.shape
    return pl.pallas_call(
        _rms_cast_kernel,
        out_shape=jax.ShapeDtypeStruct((t, d), BF16),
        grid=(t // tm,),
        in_specs=[pl.BlockSpec((tm, d), lambda i: (i, 0)),
                  pl.BlockSpec((1, d), lambda i: (0, 0))],
        out_specs=pl.BlockSpec((tm, d), lambda i: (i, 0)),
        compiler_params=_params(("parallel",)),
        name="rms_cast",
    )(x2d, gain.reshape(1, d))


def _proj_kernel(*refs, shift, residual):
    refs = list(refs)
    h_ref, wa_ref = refs[:2]
    wb_ref = refs[2] if shift else None
    res_ref = refs[-3] if residual else None
    o_ref, w_s = refs[-2:]
    tn = w_s.shape[1]

    @pl.when(pl.program_id(1) == 0)
    def _():
        for r0 in range(0, w_s.shape[0], PROJ_CAST_ROWS):
            rows = pl.ds(r0, PROJ_CAST_ROWS)
            w = wa_ref[rows, :]
            if shift:
                wcat = jnp.concatenate([w, wb_ref[rows, :]], axis=1)
                w = pltpu.roll(wcat, tn + LANES - shift, axis=1)[:, :tn]
            w_s[rows, :] = w.astype(BF16)

    y = _dot(h_ref[...], w_s[...])
    o_ref[...] = y + res_ref[...] if residual else y


def _proj(h, w_stack, layer, start, ncols, tm, tn, residual=None, name="in_proj"):
    t, d = h.shape
    shift = start % LANES
    base = (start - shift) // tn
    assert (start - shift) % tn == 0 and ncols % tn == 0
    in_specs = [pl.BlockSpec((tm, d), lambda j, i: (i, 0)),
                pl.BlockSpec((None, d, tn), lambda j, i: (layer, 0, base + j))]
    args = [h, w_stack]
    if shift:
        nxt = tn // LANES
        in_specs.append(pl.BlockSpec((None, d, LANES), lambda j, i: (layer, 0, (base + j + 1) * nxt)))
        args.append(w_stack)
    if residual is not None:
        in_specs.append(pl.BlockSpec((tm, tn), lambda j, i: (i, j)))
        args.append(residual)
    return pl.pallas_call(
        functools.partial(_proj_kernel, shift=shift, residual=residual is not None),
        out_shape=jax.ShapeDtypeStruct((t, ncols), F32),
        grid=(ncols // tn, t // tm),
        in_specs=in_specs,
        out_specs=pl.BlockSpec((tm, tn), lambda j, i: (i, j)),
        scratch_shapes=[pltpu.VMEM((d, tn), BF16)],
        compiler_params=_params(("parallel", "arbitrary")),
        name=name,
    )(*args)


def _proj_small_kernel(h_ref, w0_ref, w1_ref, w2_ref, o_ref, w_s):
    @pl.when(pl.program_id(0) == 0)
    def _():
        for b, w_ref in enumerate((w0_ref, w1_ref, w2_ref)):
            w_s[:, b * LANES:(b + 1) * LANES] = w_ref[...].astype(BF16)

    o_ref[...] = _dot(h_ref[...], w_s[...])


def _proj_small(h, w_stack, layer, tm):
    t, d = h.shape
    nb = len(S_BLOCKS)
    wspecs = [pl.BlockSpec((None, d, LANES), functools.partial(lambda i, blk: (layer, 0, blk), blk=blk))
              for blk in S_BLOCKS]
    return pl.pallas_call(
        _proj_small_kernel,
        out_shape=jax.ShapeDtypeStruct((t, nb * LANES), F32),
        grid=(t // tm,),
        in_specs=[pl.BlockSpec((tm, d), lambda i: (i, 0))] + wspecs,
        out_specs=pl.BlockSpec((tm, nb * LANES), lambda i: (i, 0)),
        scratch_shapes=[pltpu.VMEM((d, nb * LANES), BF16)],
        compiler_params=_params(("arbitrary",)),
        name="in_proj_small",
    )(h, w_stack, w_stack, w_stack)


def _merge_kernel(og_ref, on_ref, os_ref, g0_ref, g1_ref, g2_ref, wb0_ref, wb1_ref, wb2_ref, o_ref):
    m = (_sigmoid(g0_ref[...]) * _dot(og_ref[...], wb0_ref[...])
         + _sigmoid(g1_ref[...]) * _dot(on_ref[...], wb1_ref[...])
         + _sigmoid(g2_ref[...]) * _dot(os_ref[...], wb2_ref[...]))
    o_ref[...] = m.astype(o_ref.dtype)


def _merge(o_gla, o_nsa, o_ssd, pa, wb, tm, tn):
    t = o_gla.shape[0]
    d = D_MODEL
    nj = d // tn
    gate_blk0 = A_GATE // tn
    return pl.pallas_call(
        _merge_kernel,
        out_shape=jax.ShapeDtypeStruct((t, d), BF16),
        grid=(t // tm, nj),
        in_specs=[pl.BlockSpec((tm, GLA_DV), lambda i, j: (i, 0)),
                  pl.BlockSpec((tm, NSA_WIDTH), lambda i, j: (i, 0)),
                  pl.BlockSpec((tm, SSD_D_INNER), lambda i, j: (i, 0)),
                  pl.BlockSpec((tm, tn), lambda i, j: (i, gate_blk0 + j)),
                  pl.BlockSpec((tm, tn), lambda i, j: (i, gate_blk0 + nj + j)),
                  pl.BlockSpec((tm, tn), lambda i, j: (i, gate_blk0 + 2 * nj + j)),
                  pl.BlockSpec((GLA_DV, tn), lambda i, j: (0, j)),
                  pl.BlockSpec((NSA_WIDTH, tn), lambda i, j: (1, j)),
                  pl.BlockSpec((SSD_D_INNER, tn), lambda i, j: (1, j))],
        out_specs=pl.BlockSpec((tm, tn), lambda i, j: (i, j)),
        compiler_params=_params(("parallel", "arbitrary")),
        name="branch_merge",
    )(o_gla, o_nsa, o_ssd, pa, pa, pa, wb, wb, wb)


def _ffn_kernel(x_ref, g_ref, gf_ref, wg_ref, wu_ref, wd_ref, o_ref, h_ref, *, final_norm):
    j = pl.program_id(1)

    @pl.when(j == 0)
    def _():
        x = x_ref[...]
        h_ref[...] = _rms_rows(x, g_ref[...]).astype(BF16)
        o_ref[...] = x

    h = h_ref[...]
    a = _silu(_dot(h, wg_ref[...])) * _dot(h, wu_ref[...])
    o_ref[...] += _dot(a.astype(BF16), wd_ref[...])

    if final_norm:
        @pl.when(j == pl.num_programs(1) - 1)
        def _():
            o_ref[...] = _rms_rows(o_ref[...], gf_ref[...])


def _ffn(x2d, gain, gain_final, wg, wu, wd, tm, tf, final_norm):
    t, d = x2d.shape
    f = wg.shape[1]
    return pl.pallas_call(
        functools.partial(_ffn_kernel, final_norm=final_norm),
        out_shape=jax.ShapeDtypeStruct((t, d), F32),
        grid=(t // tm, f // tf),
        in_specs=[pl.BlockSpec((tm, d), lambda i, j: (i, 0)),
                  pl.BlockSpec((1, d), lambda i, j: (0, 0)),
                  pl.BlockSpec((1, d), lambda i, j: (0, 0)),
                  pl.BlockSpec((d, tf), lambda i, j: (0, j)),
                  pl.BlockSpec((d, tf), lambda i, j: (0, j)),
                  pl.BlockSpec((tf, d), lambda i, j: (j, 0))],
        out_specs=pl.BlockSpec((tm, d), lambda i, j: (i, 0)),
        scratch_shapes=[pltpu.VMEM((tm, d), BF16)],
        compiler_params=_params(("parallel", "arbitrary")),
        name="ffn",
    )(x2d, gain.reshape(1, d), gain_final.reshape(1, d), wg, wu, wd)


GLA_CHUNK = 128
GLA_SUB = 16
GLA_EXP_CAP = 88.0
GLA_HEADS_PER_STEP = 4
GLA_TILE = 256


def _gla_kernel(q_ref, k_ref, v_ref, r_ref, sm_ref, w2_ref, gb_ref, ng_ref, o_ref, st_ref,
                *, chunks_per_tile):
    @pl.when(pl.program_id(2) == 0)
    def _():
        st_ref[...] = jnp.zeros_like(st_ref)

    c_rows = GLA_CHUNK
    row = lax.broadcasted_iota(jnp.int32, (c_rows, c_rows), 0)
    col = lax.broadcasted_iota(jnp.int32, (c_rows, c_rows), 1)
    tri = (col <= row).astype(BF16)
    srow = lax.broadcasted_iota(jnp.int32, (GLA_SUB, c_rows), 0)
    scol = lax.broadcasted_iota(jnp.int32, (GLA_SUB, c_rows), 1)
    scale = GLA_HK ** -0.5

    smb = sm_ref[...].astype(BF16)
    work = []
    for h in range(GLA_HEADS_PER_STEP):
        kcols = slice(h * GLA_HK, (h + 1) * GLA_HK)
        vcols = slice(h * GLA_HV, (h + 1) * GLA_HV)
        ga = _dot(smb, w2_ref[:, kcols]) + gb_ref[:, kcols]
        log_a_all = -_softplus(-ga) * (1.0 / GLA_GATE_NORMALIZER)
        for c in range(chunks_per_tile):
            rows = slice(c * c_rows, (c + 1) * c_rows)
            bc = _sel_dot(tri, log_a_all[rows])
            q = q_ref[rows, kcols] * scale
            k = k_ref[rows, kcols]
            vb = v_ref[rows, vcols].astype(BF16)
            parts = []
            for i in range(c_rows // GLA_SUB):
                lo = i * GLA_SUB
                ref_pt = bc[lo - 1:lo, :] if i > 0 else jnp.zeros((1, GLA_HK), F32)
                kt = (k * jnp.exp(jnp.minimum(ref_pt - bc, GLA_EXP_CAP))).astype(BF16)
                qi = (q[lo:lo + GLA_SUB, :] * jnp.exp(bc[lo:lo + GLA_SUB, :] - ref_pt)).astype(BF16)
                sc = _dot_nt(qi, kt)
                parts.append(jnp.where(scol <= srow + lo, sc, 0.0))
            intra = _dot(jnp.concatenate(parts, axis=0).astype(BF16), vb)
            b_last = bc[c_rows - 1:c_rows, :]
            khat = (k * jnp.exp(b_last - bc)).astype(BF16)
            work.append((h, rows, vcols, (q * jnp.exp(bc)).astype(BF16), intra,
                         jnp.exp(b_last), _dot_tn(vb, khat)))

    states = [st_ref[h] for h in range(GLA_HEADS_PER_STEP)]
    for h, rows, vcols, qe, intra, decay, incr in work:
        out = _dot_nt(qe, states[h].astype(BF16)) + intra
        states[h] = states[h] * decay + incr
        y = _rms_rows(out, ng_ref[...]) * _silu(r_ref[rows, vcols])
        o_ref[rows, vcols] = y.astype(o_ref.dtype)
    for h in range(GLA_HEADS_PER_STEP):
        st_ref[h] = states[h]


def _gla(pa, pb, ps, w2pad, gate_b, norm_g, bsz, seq, tile):
    t = bsz * seq
    nt = seq // tile
    hps = GLA_HEADS_PER_STEP
    kw, vw = hps * GLA_HK, hps * GLA_HV
    qb, kb = A_GQ // kw, A_GK // kw
    vb, rb = A_GV // vw, B_GR // vw
    return pl.pallas_call(
        functools.partial(_gla_kernel, chunks_per_tile=tile // GLA_CHUNK),
        out_shape=jax.ShapeDtypeStruct((t, GLA_DV), BF16),
        grid=(bsz, GLA_HEADS // hps, nt),
        in_specs=[pl.BlockSpec((tile, kw), lambda b, h, c: (b * nt + c, qb + h)),
                  pl.BlockSpec((tile, kw), lambda b, h, c: (b * nt + c, kb + h)),
                  pl.BlockSpec((tile, vw), lambda b, h, c: (b * nt + c, vb + h)),
                  pl.BlockSpec((tile, vw), lambda b, h, c: (b * nt + c, rb + h)),
                  pl.BlockSpec((tile, LANES), lambda b, h, c: (b * nt + c, 0)),
                  pl.BlockSpec((LANES, kw), lambda b, h, c: (0, h)),
                  pl.BlockSpec((1, kw), lambda b, h, c: (0, h)),
                  pl.BlockSpec((1, GLA_HV), lambda b, h, c: (0, 0))],
        out_specs=pl.BlockSpec((tile, vw), lambda b, h, c: (b * nt + c, h)),
        scratch_shapes=[pltpu.VMEM((hps, GLA_HV, GLA_HK), F32)],
        compiler_params=_params(("parallel", "parallel", "arbitrary")),
        name="gla_mixer",
    )(pa, pa, pa, pb, ps, w2pad, gate_b.reshape(1, GLA_DK), norm_g.reshape(1, GLA_HV))


SSD_STEP_ROWS = 1024
SSD_SUB = 128
SSD_HIST = 8


def _causal_conv(e_ref, src_ref, w_ref, b_ref):
    n = src_ref.shape[0]
    e_ref[pl.ds(SSD_HIST, n), :] = src_ref[...]
    acc = src_ref[...] * w_ref[SSD_CONV - 1:SSD_CONV, :] + b_ref[...]
    for s in range(1, SSD_CONV):
        acc = acc + e_ref[pl.ds(SSD_HIST - s, n), :] * w_ref[SSD_CONV - 1 - s:SSD_CONV - s, :]
    e_ref[pl.ds(0, SSD_HIST), :] = src_ref[pl.ds(n - SSD_HIST, SSD_HIST), :]
    return _silu(acc)


def _ssd_kernel(xs_ref, bm_ref, cm_ref, z_ref, dtc_ref, dtr_ref,
                wx_ref, wb_ref, wc_ref, bx_ref, bb_ref, bc_ref,
                dbc_ref, dbr_ref, alc_ref, alr_ref, dsk_ref, ng_ref,
                o_ref, st_ref, ex_ref, eb_ref, ec_ref):
    n = xs_ref.shape[0]
    sub = min(SSD_SUB, n)
    hpg, hd = SSD_HEADS_PER_GROUP, SSD_HEAD_DIM

    @pl.when(pl.program_id(2) == 0)
    def _():
        st_ref[...] = jnp.zeros_like(st_ref)
        ex_ref[pl.ds(0, SSD_HIST), :] = jnp.zeros((SSD_HIST, ex_ref.shape[1]), F32)
        eb_ref[pl.ds(0, SSD_HIST), :] = jnp.zeros((SSD_HIST, eb_ref.shape[1]), F32)
        ec_ref[pl.ds(0, SSD_HIST), :] = jnp.zeros((SSD_HIST, ec_ref.shape[1]), F32)

    xs = _causal_conv(ex_ref, xs_ref, wx_ref, bx_ref)
    bmb = _causal_conv(eb_ref, bm_ref, wb_ref, bb_ref).astype(BF16)
    cmb = _causal_conv(ec_ref, cm_ref, wc_ref, bc_ref).astype(BF16)

    w3 = 3 * hpg
    krow = lax.broadcasted_iota(jnp.int32, (w3, hpg * hd), 0)
    kcol = lax.broadcasted_iota(jnp.int32, (w3, hpg * hd), 1) >> SSD_HEAD_SHIFT
    expand3 = ((krow == kcol) | (krow == kcol + hpg) | (krow == kcol + 2 * hpg)).astype(BF16)

    def expand_heads(x):
        hi, mid, lo = _split3(x)
        klane = lax.broadcasted_iota(jnp.int32, x.shape, 1)
        return _dot(jnp.where(klane < hpg, hi, jnp.where(klane < 2 * hpg, mid, lo)), expand3)

    dt_c = _softplus(dtc_ref[0] + dbc_ref[0])
    a_c = -jnp.exp(alc_ref[0])
    a_r = -jnp.exp(alr_ref[0])
    xdt = xs * expand_heads(dt_c)

    row = lax.broadcasted_iota(jnp.int32, (sub, sub), 0)
    col = lax.broadcasted_iota(jnp.int32, (sub, sub), 1)
    tri = (col <= row).astype(BF16)
    triu = (row <= col).astype(BF16)
    causal_bias = jnp.where(col <= row, 0.0, NEG_INF)
    lane = lax.broadcasted_iota(jnp.int32, (sub, 2 * hd), 1)

    for si in range(n // sub):
        rows = slice(si * sub, (si + 1) * sub)
        acum_c = _sel_dot(tri, dt_c[rows] * a_c)
        adt_r = _softplus(dtr_ref[0, si] + dbr_ref[0]) * a_r
        acum_r = _sel_dot_left(adt_r, triu)
        a_last = acum_c[sub - 1:sub, :]
        eac_x = expand_heads(jnp.exp(acum_c))
        dec_x = expand_heads(jnp.exp(a_last - acum_c))
        sdec_x = expand_heads(jnp.broadcast_to(jnp.exp(a_last), (8, w3)))[0:1, :]
        ac2 = acum_c * LOG2_E
        ar2 = acum_r * LOG2_E

        xd, cs, bs = xdt[rows], cmb[rows], bmb[rows]
        cb = _dot_nt(cs, bs)
        pair_out = []
        for pr in range(hpg // 2):
            xp = xd[:, pr * 2 * hd:(pr + 1) * 2 * hd]
            acc = None
            for half in range(2):
                r = 2 * pr + half
                seg = jnp.exp2(ac2[:, r:r + 1] - ar2[r:r + 1, :] + causal_bias)
                rhs = jnp.where((lane >> SSD_HEAD_SHIFT) == half, xp, 0.0).astype(BF16)
                term = _dot((cb * seg).astype(BF16), rhs)
                acc = term if acc is None else acc + term
            pair_out.append(acc)
        y_diag = jnp.concatenate(pair_out, axis=1)

        st = st_ref[...]
        y_off = _dot(cs, st.astype(BF16)) * eac_x
        st_ref[...] = st * sdec_x + _dot_tn(bs, (xd * dec_x).astype(BF16))

        y = (y_diag + y_off + xs[rows] * dsk_ref[0]) * _silu(z_ref[rows, :])
        o_ref[rows, :] = _rms_rows(y, ng_ref[...]).astype(o_ref.dtype)


def _ssd(pc, dt_c, dt_r, conv_w, conv_b, dt_bias, a_log, d_skip, norm_g, bsz, seq, chunk):
    t = bsz * seq
    nc = seq // chunk
    sub = min(SSD_SUB, chunk)
    g, hpg, gw, ns = SSD_GROUPS, SSD_HEADS_PER_GROUP, SSD_GROUP_WIDTH, SSD_D_STATE
    xsb = C_SXBC // gw
    bmb = (C_SXBC + SSD_D_INNER) // ns
    cmb = (C_SXBC + SSD_D_INNER + g * ns) // ns
    zb = C_SZ // gw
    cw_b = SSD_D_INNER // ns
    cw_c = (SSD_D_INNER + g * ns) // ns
    conv_b2 = conv_b.reshape(1, SSD_CONV_DIM)
    dbc = jnp.tile(dt_bias.reshape(g, 1, hpg), (1, 1, 3))
    dbr = dt_bias.reshape(g, hpg, 1)
    alc = jnp.tile(a_log.reshape(g, 1, hpg), (1, 1, 3))
    alr = a_log.reshape(g, hpg, 1)
    dsk = jnp.repeat(d_skip, SSD_HEAD_DIM).reshape(g, 1, gw)
    ng = norm_g.reshape(1, SSD_D_INNER)
    return pl.pallas_call(
        _ssd_kernel,
        out_shape=jax.ShapeDtypeStruct((t, SSD_D_INNER), BF16),
        grid=(bsz, g, nc),
        in_specs=[pl.BlockSpec((chunk, gw), lambda b, gi, c: (b * nc + c, xsb + gi)),
                  pl.BlockSpec((chunk, ns), lambda b, gi, c: (b * nc + c, bmb + gi)),
                  pl.BlockSpec((chunk, ns), lambda b, gi, c: (b * nc + c, cmb + gi)),
                  pl.BlockSpec((chunk, gw), lambda b, gi, c: (b * nc + c, zb + gi)),
                  pl.BlockSpec((1, chunk, 3 * hpg), lambda b, gi, c: (gi, b * nc + c, 0)),
                  pl.BlockSpec((1, chunk // sub, hpg, sub), lambda b, gi, c: (gi, b * nc + c, 0, 0)),
                  pl.BlockSpec((SSD_CONV, gw), lambda b, gi, c: (0, gi)),
                  pl.BlockSpec((SSD_CONV, ns), lambda b, gi, c: (0, cw_b + gi)),
                  pl.BlockSpec((SSD_CONV, ns), lambda b, gi, c: (0, cw_c + gi)),
                  pl.BlockSpec((1, gw), lambda b, gi, c: (0, gi)),
                  pl.BlockSpec((1, ns), lambda b, gi, c: (0, cw_b + gi)),
                  pl.BlockSpec((1, ns), lambda b, gi, c: (0, cw_c + gi)),
                  pl.BlockSpec((1, 1, 3 * hpg), lambda b, gi, c: (gi, 0, 0)),
                  pl.BlockSpec((1, hpg, 1), lambda b, gi, c: (gi, 0, 0)),
                  pl.BlockSpec((1, 1, 3 * hpg), lambda b, gi, c: (gi, 0, 0)),
                  pl.BlockSpec((1, hpg, 1), lambda b, gi, c: (gi, 0, 0)),
                  pl.BlockSpec((1, 1, gw), lambda b, gi, c: (gi, 0, 0)),
                  pl.BlockSpec((1, gw), lambda b, gi, c: (0, gi))],
        out_specs=pl.BlockSpec((chunk, gw), lambda b, gi, c: (b * nc + c, gi)),
        scratch_shapes=[pltpu.VMEM((ns, gw), F32),
                        pltpu.VMEM((SSD_HIST + chunk, gw), F32),
                        pltpu.VMEM((SSD_HIST + chunk, ns), F32),
                        pltpu.VMEM((SSD_HIST + chunk, ns), F32)],
        compiler_params=_params(("parallel", "parallel", "arbitrary")),
        name="ssd_mixer",
    )(pc, pc, pc, pc, dt_c, dt_r, conv_w, conv_w, conv_w, conv_b2, conv_b2, conv_b2,
      dbc, dbr, alc, alr, dsk, ng)


NSA_TQ = 256
NSA_KB = 512


def _rope(x, cos, sin):
    lane = lax.broadcasted_iota(jnp.int32, x.shape, 1)
    half = ROPE_DIM // 2
    swapped = jnp.where(lane < half, pltpu.roll(x, LANES - half, axis=1), pltpu.roll(x, half, axis=1))
    return x * cos + swapped * sin


def _compress(src_ref, pe_ref, w1_ref, w2_ref, nblk):
    hd = NSA_HEAD_DIM
    a0 = jnp.zeros((nblk, CMP_HIDDEN), F32)
    a1 = jnp.zeros((nblk, CMP_HIDDEN), F32)
    for c in range(CMP_STRIDE):
        xc = src_ref[pl.ds(c, nblk, stride=CMP_STRIDE), :]
        a0 = a0 + _dot((xc + pe_ref[c:c + 1, :]).astype(BF16), w1_ref[c * hd:(c + 1) * hd, :])
        c1 = CMP_STRIDE + c
        a1 = a1 + _dot((xc + pe_ref[c1:c1 + 1, :]).astype(BF16), w1_ref[c1 * hd:(c1 + 1) * hd, :])
    hid = a0 + pltpu.roll(a1, nblk - 1, axis=0)
    return _dot(_silu(hid).astype(BF16), w2_ref[...])


def _nsa_kernel(q_ref, kc_ref, vc_ref, ks_ref, vs_ref, kw_ref, vw_ref, g_ref,
                cos_ref, sin_ref, pek_ref, pev_ref, w1k_ref, w2k_ref, w1v_ref, w2v_ref, ovt_ref,
                o_ref, kcmp_s, vcmp_s, ksel_s, vsel_s, kwin_s, vwin_s, krope_s,
                sbuf_s, mx_s, acc_s, *, seq):
    tq, hd, nr = NSA_TQ, NSA_HEAD_DIM, NSA_Q_PER_KV
    i = pl.program_id(2)
    nblk = seq // CMP_STRIDE
    nsel = seq // SEL_BLOCK
    scale = hd ** -0.5

    @pl.when(i == 0)
    def _():
        cos, sin = cos_ref[...], sin_ref[...]
        krope_s[...] = _rope(kc_ref[...], cos, sin)
        ksel_s[...] = _rope(ks_ref[...], cos, sin).astype(BF16)
        kwin_s[...] = _rope(kw_ref[...], cos, sin).astype(BF16)
        vsel_s[:, :hd] = vs_ref[...].astype(BF16)
        vsel_s[:, hd:] = jnp.ones((seq, hd), BF16)
        vwin_s[:, :hd] = vw_ref[...].astype(BF16)
        vwin_s[:, hd:] = jnp.ones((seq, hd), BF16)
        kcmp_s[...] = _compress(krope_s, pek_ref, w1k_ref, w2k_ref, nblk).astype(BF16)
        vcmp_s[:, :hd] = _compress(vc_ref, pev_ref, w1v_ref, w2v_ref, nblk).astype(BF16)
        vcmp_s[:, hd:] = jnp.ones((nblk, hd), BF16)

    t0 = pl.multiple_of(i * tq, tq)
    cos = cos_ref[pl.ds(t0, tq), :]
    sin = sin_ref[pl.ds(t0, tq), :]
    qs = [_rope(q_ref[:, r * hd:(r + 1) * hd], cos, sin) for r in range(nr)]
    qb = jnp.concatenate(qs, axis=0).astype(BF16)
    c2 = scale * LOG2_E
    tpos = t0 + lax.broadcasted_iota(jnp.int32, (tq, 1), 0)

    ncol = lax.broadcasted_iota(jnp.int32, (1, nblk), 1)
    ok = (ncol * CMP_STRIDE + (CMP_LEN - 1)) <= tpos
    bias_c = jnp.where(ok, 0.0, NEG_INF)
    s = _dot_nt(qb, kcmp_s[...]).reshape(nr, tq, nblk) + bias_c[None]
    e = jnp.exp2((s - jnp.max(s, axis=-1, keepdims=True)) * c2)
    oc = _dot(e.reshape(nr * tq, nblk).astype(BF16), vcmp_s[...]).reshape(nr, tq, 2 * hd)
    inv_lc = 1.0 / oc[:, :, hd:]
    has_cmp = (tpos >= CMP_LEN - 1)[None]
    o_cmp = jnp.where(has_cmp, oc[:, :, :hd] * inv_lc, 0.0)
    p_cmp = jnp.where(ok[None], e * inv_lc[:, :, :nblk], 0.0)

    p_sum = p_cmp[0]
    for r in range(1, nr):
        p_sum = p_sum + p_cmp[r]
    ph = p_sum.astype(BF16)
    pm = (p_sum - ph.astype(F32)).astype(BF16)
    ovt = ovt_ref[...]
    p_slc = _dot_nt(ovt, ph) + _dot_nt(ovt, pm)
    jrow = lax.broadcasted_iota(jnp.int32, (nsel, tq), 0)
    blk_t = (t0 + lax.broadcasted_iota(jnp.int32, (nsel, tq), 1)) >> SEL_SHIFT
    forced = (jrow == 0) | (jrow == blk_t) | (jrow == blk_t - 1)
    score = jnp.where(forced, FORCE_SCORE, jnp.where(jrow <= blk_t, p_slc, NEG_INF))
    rank = jnp.zeros((nsel, tq), F32)
    for j in range(nsel):
        sj = score[j:j + 1, :]
        beats = jnp.where(sj > score, 1.0, jnp.where((sj == score) & (jrow > j), 1.0, 0.0))
        rank = rank + beats
    sel_t = jnp.where(rank < float(min(SEL_TOPK, nsel)), 1.0, 0.0)
    sel = jnp.transpose(sel_t).astype(BF16)

    wlen = min(WINDOW + tq, seq)
    w0 = pl.multiple_of(jnp.maximum(t0 + tq - wlen, 0), tq)
    ktw = kwin_s[pl.ds(w0, wlen), :]
    vtw = vwin_s[pl.ds(w0, wlen), :]
    kposw = w0 + lax.broadcasted_iota(jnp.int32, (1, wlen), 1)
    bias_w = jnp.where((kposw <= tpos) & (kposw > tpos - WINDOW), 0.0, NEG_INF)
    sw = _dot_nt(qb, ktw).reshape(nr, tq, wlen) + bias_w[None]
    ew = jnp.exp2((sw - jnp.max(sw, axis=-1, keepdims=True)) * c2)
    ow = _dot(ew.reshape(nr * tq, wlen).astype(BF16), vtw).reshape(nr, tq, 2 * hd)
    o_win = ow[:, :, :hd] * (1.0 / ow[:, :, hd:])

    ng = nr * 3
    gs = _sigmoid(g_ref[0])
    g_hi, g_mid, g_lo = _split3(gs)
    glane = lax.broadcasted_iota(jnp.int32, gs.shape, 1)
    g_terms = jnp.where(glane < ng, g_hi, jnp.where(glane < 2 * ng, g_mid, g_lo))
    grow = lax.broadcasted_iota(jnp.int32, (3 * ng, ng * hd), 0)
    gcol = lax.broadcasted_iota(jnp.int32, (3 * ng, ng * hd), 1) >> NSA_HEAD_SHIFT
    gsel = (grow == gcol) | (grow == gcol + ng) | (grow == gcol + 2 * ng)
    gexp = _dot(g_terms, gsel.astype(BF16))
    gates = [gexp[:, j * hd:(j + 1) * hd] for j in range(ng)]
    o_cw = [gates[3 * r] * o_cmp[r] + gates[3 * r + 2] * o_win[r] for r in range(nr)]

    kb = min(NSA_KB, seq)
    n_kb = (t0 + tq + kb - 1) // kb
    nlb = kb // LANES
    mx_s[...] = jnp.full(mx_s.shape, NEG_INF, F32)
    acc_s[...] = jnp.zeros(acc_s.shape, F32)

    def lane_blocks(x):
        return [x[:, b * LANES:(b + 1) * LANES] for b in range(nlb)]

    def score_step(c, carry):
        k0 = pl.multiple_of(c * kb, kb)
        erow = lax.broadcasted_iota(jnp.int32, (nsel, kb), 0)
        ecol = lax.broadcasted_iota(jnp.int32, (nsel, kb), 1)
        expand = (((k0 + ecol) >> SEL_SHIFT) == erow).astype(BF16)
        chosen = _dot(sel, expand)
        kpos = k0 + lax.broadcasted_iota(jnp.int32, (1, kb), 1)
        bias = jnp.where((chosen > 0.5) & (kpos <= tpos), 0.0, NEG_INF)
        sb = _dot_nt(qb, ksel_s[pl.ds(k0, kb), :]).reshape(nr, tq, kb) + bias[None]
        sb = sb.reshape(nr * tq, kb)
        sbuf_s[c] = sb
        mx = mx_s[...]
        for blk in lane_blocks(sb):
            mx = jnp.maximum(mx, blk)
        mx_s[...] = mx
        return carry

    lax.fori_loop(0, n_kb, score_step, 0)
    m_sel = jnp.max(mx_s[...], axis=-1, keepdims=True)

    def prob_step(c, carry):
        k0 = pl.multiple_of(c * kb, kb)
        pt = jnp.exp2((sbuf_s[c] - m_sel) * c2)
        acc_s[...] += _dot(pt.astype(BF16), vsel_s[pl.ds(k0, kb), :])
        return carry

    lax.fori_loop(0, n_kb, prob_step, 0)
    o_sel = (acc_s[:, :hd] * (1.0 / acc_s[:, hd:])).reshape(nr, tq, hd)

    for r in range(nr):
        o = o_cw[r] + gates[3 * r + 1] * o_sel[r]
        o_ref[:, r * hd:(r + 1) * hd] = o.astype(o_ref.dtype)


def _nsa_tables(seq):
    half = ROPE_DIM // 2
    inv_freq = ROPE_THETA ** (-jnp.arange(half, dtype=F32) / half)
    ang = jnp.arange(seq).astype(F32)[:, None] * inv_freq[None, :]
    cos, sin = jnp.cos(ang), jnp.sin(ang)
    rest = NSA_HEAD_DIM - ROPE_DIM
    cos_t = jnp.concatenate([cos, cos, jnp.ones((seq, rest), F32)], axis=1)
    sin_t = jnp.concatenate([-sin, sin, jnp.zeros((seq, rest), F32)], axis=1)
    n_sel = seq // SEL_BLOCK
    cmp_starts = np.arange(seq // CMP_STRIDE) * CMP_STRIDE
    sel_starts = np.arange(n_sel) * SEL_BLOCK
    overlap = np.clip(np.minimum(cmp_starts[:, None] + CMP_LEN, sel_starts[None, :] + SEL_BLOCK)
                      - np.maximum(cmp_starts[:, None], sel_starts[None, :]), 0, None).astype(np.float32) / CMP_LEN
    return cos_t, sin_t, jnp.asarray(overlap.T, BF16)


def _nsa(pb, gates, pos_k, pos_v, w1k, w2k, w1v, w2v, bsz, seq):
    t = bsz * seq
    tq, hd, g, nr = NSA_TQ, NSA_HEAD_DIM, NSA_KV_GROUPS, NSA_Q_PER_KV
    nq = seq // tq
    nblk = seq // CMP_STRIDE
    nsel = seq // SEL_BLOCK
    cos_t, sin_t, ovt = _nsa_tables(seq)
    kb = min(NSA_KB, seq)
    qblk = B_NQ // (nr * hd)
    kvb = B_NKV // hd

    def kv_spec(split):
        return pl.BlockSpec((seq, hd), lambda b, gi, i: (b, kvb + split * g + gi))

    const2 = lambda b, gi, i: (0, 0)
    return pl.pallas_call(
        functools.partial(_nsa_kernel, seq=seq),
        out_shape=jax.ShapeDtypeStruct((t, NSA_WIDTH), BF16),
        grid=(bsz, g, nq),
        in_specs=[pl.BlockSpec((tq, nr * hd), lambda b, gi, i: (b * nq + i, qblk + gi))]
                 + [kv_spec(sp) for sp in range(6)]
                 + [pl.BlockSpec((1, tq, nr * 9), lambda b, gi, i: (gi, b * nq + i, 0)),
                    pl.BlockSpec((seq, hd), const2),
                    pl.BlockSpec((seq, hd), const2),
                    pl.BlockSpec((CMP_LEN, hd), const2),
                    pl.BlockSpec((CMP_LEN, hd), const2),
                    pl.BlockSpec((CMP_LEN * hd, CMP_HIDDEN), const2),
                    pl.BlockSpec((CMP_HIDDEN, hd), const2),
                    pl.BlockSpec((CMP_LEN * hd, CMP_HIDDEN), const2),
                    pl.BlockSpec((CMP_HIDDEN, hd), const2),
                    pl.BlockSpec((nsel, nblk), const2)],
        out_specs=pl.BlockSpec((tq, nr * hd), lambda b, gi, i: (b * nq + i, gi)),
        scratch_shapes=[pltpu.VMEM((nblk, hd), BF16), pltpu.VMEM((nblk, 2 * hd), BF16),
                        pltpu.VMEM((seq, hd), BF16), pltpu.VMEM((seq, 2 * hd), BF16),
                        pltpu.VMEM((seq, hd), BF16), pltpu.VMEM((seq, 2 * hd), BF16),
                        pltpu.VMEM((seq, hd), F32),
                        pltpu.VMEM((seq // kb, nr * tq, kb), F32),
                        pltpu.VMEM((nr * tq, LANES), F32),
                        pltpu.VMEM((nr * tq, 2 * hd), F32)],
        compiler_params=_params(("parallel", "parallel", "arbitrary")),
        name="nsa_mixer",
    )(pb, pb, pb, pb, pb, pb, pb, gates, cos_t, sin_t, pos_k, pos_v, w1k, w2k, w1v, w2v, ovt)


def _in_proj(x2d, gain, w_in_stack, layer):
    t = x2d.shape[0]
    tm = min(PROJ_TILE_M, t)
    h = _rms_cast(x2d, gain, tm)
    pa = _proj(h, w_in_stack, layer, 0, PA_COLS, tm, PROJ_TILE_N)
    pb = _proj(h, w_in_stack, layer, PB_START, PB_COLS, tm, PROJ_TILE_N)
    pc = _proj(h, w_in_stack, layer, PC_START, PC_COLS, tm, PROJ_TILE_N)
    ps = _proj_small(h, w_in_stack, layer, tm)
    return pa, pb, pc, ps


def _small_views(ps):
    t = ps.shape[0]
    nsa_g = ps[:, S_NG:S_NG + 3 * NSA_HEADS]
    nsa_g = nsa_g.reshape(t, NSA_KV_GROUPS, NSA_Q_PER_KV * 3).transpose(1, 0, 2)
    nsa_g = jnp.tile(nsa_g, (1, 1, 3))
    dt = ps[:, S_DT:S_DT + SSD_HEADS].reshape(t, SSD_GROUPS, SSD_HEADS_PER_GROUP)
    sub = min(SSD_SUB, t)
    dt_r = dt.reshape(t // sub, sub, SSD_GROUPS, SSD_HEADS_PER_GROUP).transpose(2, 0, 3, 1)
    return nsa_g, jnp.tile(dt.transpose(1, 0, 2), (1, 1, 3)), dt_r


def _layer(x2d, bsz, seq, layer, norm_mix, w_in, gla_gate_w2, gla_gate_b, gla_out_norm,
           nsa_cmp_pos_k, nsa_cmp_pos_v, nsa_cmp_k_w1, nsa_cmp_k_w2, nsa_cmp_v_w1, nsa_cmp_v_w2,
           ssd_conv_w, ssd_conv_b, ssd_dt_bias, ssd_a_log, ssd_d, ssd_out_norm,
           w_branch, w_out, norm_ffn, w_ffn_gate, w_ffn_up, w_ffn_down, norm_final, final_norm):
    t = bsz * seq
    pa, pb, pc, ps = _in_proj(x2d, norm_mix, w_in, layer)
    nsa_g, dt_c, dt_r = _small_views(ps)

    w2pad = jnp.zeros((LANES, GLA_DK), BF16).at[S_GLOW:S_GLOW + GLA_LOWRANK].set(
        gla_gate_w2.astype(BF16))
    o_gla = _gla(pa, pb, ps, w2pad, gla_gate_b, gla_out_norm, bsz, seq, min(GLA_TILE, seq))
    o_nsa = _nsa(pb, nsa_g, nsa_cmp_pos_k, nsa_cmp_pos_v,
                 nsa_cmp_k_w1.astype(BF16), nsa_cmp_k_w2.astype(BF16),
                 nsa_cmp_v_w1.astype(BF16), nsa_cmp_v_w2.astype(BF16), bsz, seq)
    o_ssd = _ssd(pc, dt_c, dt_r, ssd_conv_w, ssd_conv_b, ssd_dt_bias, ssd_a_log, ssd_d,
                 ssd_out_norm, bsz, seq, min(SSD_STEP_ROWS, seq))

    merged = _merge(o_gla, o_nsa, o_ssd, pa, w_branch.astype(BF16), min(MERGE_TILE_M, t),
                    MERGE_TILE_N)
    tm = min(PROJ_TILE_M, t)
    x2d = _proj(merged, w_out, layer, 0, D_MODEL, tm, PROJ_TILE_N, residual=x2d, name="out_proj")
    return _ffn(x2d, norm_ffn, norm_final, w_ffn_gate.astype(BF16), w_ffn_up.astype(BF16),
                w_ffn_down.astype(BF16), min(FFN_TILE_M, t), FFN_TILE_F, final_norm)


def kernel(x, norm_mix, w_in, gla_gate_w2, gla_gate_b, gla_out_norm, nsa_cmp_pos_k, nsa_cmp_pos_v,
           nsa_cmp_k_w1, nsa_cmp_k_w2, nsa_cmp_v_w1, nsa_cmp_v_w2, ssd_conv_w, ssd_conv_b,
           ssd_dt_bias, ssd_a_log, ssd_d, ssd_out_norm, w_branch, w_out, norm_ffn, w_ffn_gate,
           w_ffn_up, w_ffn_down, norm_final):
    bsz, seq, d = x.shape
    depth = norm_mix.shape[0]
    x2d = x.reshape(bsz * seq, d)
    for l in range(depth):
        x2d = _layer(x2d, bsz, seq, l, norm_mix[l], w_in, gla_gate_w2[l], gla_gate_b[l],
                     gla_out_norm[l], nsa_cmp_pos_k[l], nsa_cmp_pos_v[l], nsa_cmp_k_w1[l],
                     nsa_cmp_k_w2[l], nsa_cmp_v_w1[l], nsa_cmp_v_w2[l], ssd_conv_w[l],
                     ssd_conv_b[l], ssd_dt_bias[l], ssd_a_log[l], ssd_d[l], ssd_out_norm[l],
                     w_branch[l], w_out, norm_ffn[l], w_ffn_gate[l], w_ffn_up[l],
                     w_ffn_down[l], norm_final, l == depth - 1)
    return x2d.reshape(bsz, seq, d)
```

```python
import functools

import numpy as np
import jax
import jax.numpy as jnp
from jax import lax
from jax.experimental import pallas as pl
from jax.experimental.pallas import tpu as pltpu

F32 = jnp.float32
BF16 = jnp.bfloat16

D_MODEL = 2048
RMS_EPS = 1e-6
LOG2_E = 1.4426950408889634
NEG_INF = -1e30
FORCE_SCORE = 1e9

GLA_HEADS = 4
GLA_DK = D_MODEL // 2
GLA_DV = D_MODEL
GLA_HK = GLA_DK // GLA_HEADS
GLA_HV = GLA_DV // GLA_HEADS
GLA_LOWRANK = 16
GLA_GATE_NORMALIZER = 16.0

NSA_HEAD_DIM = 128
NSA_HEAD_SHIFT = 7
NSA_HEADS = D_MODEL // NSA_HEAD_DIM
NSA_KV_GROUPS = 4
NSA_Q_PER_KV = NSA_HEADS // NSA_KV_GROUPS
NSA_WIDTH = NSA_HEADS * NSA_HEAD_DIM
NSA_KV_WIDTH = NSA_KV_GROUPS * NSA_HEAD_DIM
CMP_LEN = 32
CMP_STRIDE = 16
CMP_HIDDEN = 2 * NSA_HEAD_DIM
SEL_BLOCK = 64
SEL_SHIFT = 6
SEL_TOPK = 16
WINDOW = 512
ROPE_THETA = 500000.0
ROPE_DIM = NSA_HEAD_DIM // 4

SSD_D_INNER = 2 * D_MODEL
SSD_HEAD_DIM = 64
SSD_HEAD_SHIFT = 6
SSD_HEADS = SSD_D_INNER // SSD_HEAD_DIM
SSD_GROUPS = 8
SSD_HEADS_PER_GROUP = SSD_HEADS // SSD_GROUPS
SSD_D_STATE = 128
SSD_CONV = 4
SSD_GROUP_WIDTH = SSD_HEADS_PER_GROUP * SSD_HEAD_DIM
SSD_CONV_DIM = SSD_D_INNER + 2 * SSD_GROUPS * SSD_D_STATE

D_FF = ((8 * D_MODEL + 3 * 256 - 1) // (3 * 256)) * 256

LANES = 128
V7X_VMEM_LIMIT_BYTES = 56 * 1024 * 1024

_REF_SIZES = (3 * D_MODEL, GLA_DK, GLA_DK, GLA_DV, GLA_LOWRANK, GLA_DV,
              NSA_WIDTH, 6 * NSA_KV_WIDTH, 3 * NSA_HEADS,
              SSD_D_INNER, SSD_CONV_DIM, SSD_HEADS)
_REF_OFF = tuple(int(v) for v in np.concatenate([[0], np.cumsum(_REF_SIZES)]))
(_R_GATE, _R_GQ, _R_GK, _R_GV, _R_GLOW, _R_GR, _R_NQ, _R_NKV, _R_NG,
 _R_SZ, _R_SXBC, _R_SDT) = _REF_OFF[:-1]
PA_COLS = _R_GLOW
PB_START, PB_COLS = _R_GR, _R_NG - _R_GR
PC_START, PC_COLS = _R_SZ, _R_SDT - _R_SZ
A_GATE, A_GQ, A_GK, A_GV = _R_GATE, _R_GQ, _R_GK, _R_GV
B_GR, B_NQ, B_NKV = 0, _R_NQ - _R_GR, _R_NKV - _R_GR
C_SZ, C_SXBC = 0, _R_SXBC - _R_SZ
S_BLOCKS = (_R_GLOW // LANES, _R_NG // LANES, _R_SDT // LANES)
S_GLOW = _R_GLOW % LANES
S_NG = LANES + _R_NG % LANES
S_DT = 2 * LANES + _R_SDT % LANES
PROJ_TILE_M = 1024
PROJ_TILE_N = 1024
PROJ_CAST_ROWS = 256
MERGE_TILE_M = 512
MERGE_TILE_N = 512
FFN_TILE_M = 512
FFN_TILE_F = 512


def _params(sem, vmem=V7X_VMEM_LIMIT_BYTES):
    return pltpu.CompilerParams(dimension_semantics=sem, vmem_limit_bytes=vmem)


def _sigmoid(x):
    return 1.0 / (1.0 + jnp.exp(-x))


def _silu(x):
    return x * _sigmoid(x)


def _softplus(x):
    return jnp.maximum(x, 0.0) + jnp.log(1.0 + jnp.exp(-jnp.abs(x)))


def _split3(x):
    hi = x.astype(BF16)
    r1 = x - hi.astype(F32)
    mid = r1.astype(BF16)
    lo = (r1 - mid.astype(F32)).astype(BF16)
    return hi, mid, lo


def _dot(a, b):
    return jnp.dot(a, b, preferred_element_type=F32)


def _dot_nt(a, b):
    return lax.dot_general(a, b, (((1,), (1,)), ((), ())), preferred_element_type=F32)


def _dot_tn(a, b):
    return lax.dot_general(a, b, (((0,), (0,)), ((), ())), preferred_element_type=F32)


def _sel_dot(sel, x):
    hi, mid, lo = _split3(x)
    return _dot(sel, hi) + _dot(sel, mid) + _dot(sel, lo)


def _sel_dot_left(x, sel):
    hi, mid, lo = _split3(x)
    return _dot(hi, sel) + _dot(mid, sel) + _dot(lo, sel)


def _rms_rows(x, gain):
    ms = jnp.mean(x * x, axis=-1, keepdims=True)
    return x * lax.rsqrt(ms + RMS_EPS) * gain


def _rms_cast_kernel(x_ref, g_ref, o_ref):
    o_ref[...] = _rms_rows(x_ref[...], g_ref[...]).astype(o_ref.dtype)


def _rms_cast(x2d, gain, tm):
    t, d = x2d.shape
    return pl.pallas_call(
        _rms_cast_kernel,
        out_shape=jax.ShapeDtypeStruct((t, d), BF16),
        grid=(t // tm,),
        in_specs=[pl.BlockSpec((tm, d), lambda i: (i, 0)),
                  pl.BlockSpec((1, d), lambda i: (0, 0))],
        out_specs=pl.BlockSpec((tm, d), lambda i: (i, 0)),
        compiler_params=_params(("parallel",)),
        name="rms_cast",
    )(x2d, gain.reshape(1, d))


def _proj_kernel(*refs, shift, residual):
    refs = list(refs)
    h_ref, wa_ref = refs[:2]
    wb_ref = refs[2] if shift else None
    res_ref = refs[-3] if residual else None
    o_ref, w_s = refs[-2:]
    tn = w_s.shape[1]

    @pl.when(pl.program_id(1) == 0)
    def _():
        for r0 in range(0, w_s.shape[0], PROJ_CAST_ROWS):
            rows = pl.ds(r0, PROJ_CAST_ROWS)
            w = wa_ref[rows, :]
            if shift:
                wcat = jnp.concatenate([w, wb_ref[rows, :]], axis=1)
                w = pltpu.roll(wcat, tn + LANES - shift, axis=1)[:, :tn]
            w_s[rows, :] = w.astype(BF16)

    y = _dot(h_ref[...], w_s[...])
    o_ref[...] = y + res_ref[...] if residual else y


def _proj(h, w_stack, layer, start, ncols, tm, tn, residual=None, name="in_proj"):
    t, d = h.shape
    shift = start % LANES
    base = (start - shift) // tn
    assert (start - shift) % tn == 0 and ncols % tn == 0
    in_specs = [pl.BlockSpec((tm, d), lambda j, i: (i, 0)),
                pl.BlockSpec((None, d, tn), lambda j, i: (layer, 0, base + j))]
    args = [h, w_stack]
    if shift:
        nxt = tn // LANES
        in_specs.append(pl.BlockSpec((None, d, LANES), lambda j, i: (layer, 0, (base + j + 1) * nxt)))
        args.append(w_stack)
    if residual is not None:
        in_specs.append(pl.BlockSpec((tm, tn), lambda j, i: (i, j)))
        args.append(residual)
    return pl.pallas_call(
        functools.partial(_proj_kernel, shift=shift, residual=residual is not None),
        out_shape=jax.ShapeDtypeStruct((t, ncols), F32),
        grid=(ncols // tn, t // tm),
        in_specs=in_specs,
        out_specs=pl.BlockSpec((tm, tn), lambda j, i: (i, j)),
        scratch_shapes=[pltpu.VMEM((d, tn), BF16)],
        compiler_params=_params(("parallel", "arbitrary")),
        name=name,
    )(*args)


def _proj_small_kernel(h_ref, w0_ref, w1_ref, w2_ref, o_ref, w_s):
    @pl.when(pl.program_id(0) == 0)
    def _():
        for b, w_ref in enumerate((w0_ref, w1_ref, w2_ref)):
            w_s[:, b * LANES:(b + 1) * LANES] = w_ref[...].astype(BF16)

    o_ref[...] = _dot(h_ref[...], w_s[...])


def _proj_small(h, w_stack, layer, tm):
    t, d = h.shape
    nb = len(S_BLOCKS)
    wspecs = [pl.BlockSpec((None, d, LANES), functools.partial(lambda i, blk: (layer, 0, blk), blk=blk))
              for blk in S_BLOCKS]
    return pl.pallas_call(
        _proj_small_kernel,
        out_shape=jax.ShapeDtypeStruct((t, nb * LANES), F32),
        grid=(t // tm,),
        in_specs=[pl.BlockSpec((tm, d), lambda i: (i, 0))] + wspecs,
        out_specs=pl.BlockSpec((tm, nb * LANES), lambda i: (i, 0)),
        scratch_shapes=[pltpu.VMEM((d, nb * LANES), BF16)],
        compiler_params=_params(("arbitrary",)),
        name="in_proj_small",
    )(h, w_stack, w_stack, w_stack)


def _merge_kernel(og_ref, on_ref, os_ref, g0_ref, g1_ref, g2_ref, wb0_ref, wb1_ref, wb2_ref, o_ref):
    m = (_sigmoid(g0_ref[...]) * _dot(og_ref[...], wb0_ref[...])
         + _sigmoid(g1_ref[...]) * _dot(on_ref[...], wb1_ref[...])
         + _sigmoid(g2_ref[...]) * _dot(os_ref[...], wb2_ref[...]))
    o_ref[...] = m.astype(o_ref.dtype)


def _merge(o_gla, o_nsa, o_ssd, pa, wb, tm, tn):
    t = o_gla.shape[0]
    d = D_MODEL
    nj = d // tn
    gate_blk0 = A_GATE // tn
    return pl.pallas_call(
        _merge_kernel,
        out_shape=jax.ShapeDtypeStruct((t, d), BF16),
        grid=(t // tm, nj),
        in_specs=[pl.BlockSpec((tm, GLA_DV), lambda i, j: (i, 0)),
                  pl.BlockSpec((tm, NSA_WIDTH), lambda i, j: (i, 0)),
                  pl.BlockSpec((tm, SSD_D_INNER), lambda i, j: (i, 0)),
                  pl.BlockSpec((tm, tn), lambda i, j: (i, gate_blk0 + j)),
                  pl.BlockSpec((tm, tn), lambda i, j: (i, gate_blk0 + nj + j)),
                  pl.BlockSpec((tm, tn), lambda i, j: (i, gate_blk0 + 2 * nj + j)),
                  pl.BlockSpec((GLA_DV, tn), lambda i, j: (0, j)),
                  pl.BlockSpec((NSA_WIDTH, tn), lambda i, j: (1, j)),
                  pl.BlockSpec((SSD_D_INNER, tn), lambda i, j: (1, j))],
        out_specs=pl.BlockSpec((tm, tn), lambda i, j: (i, j)),
        compiler_params=_params(("parallel", "arbitrary")),
        name="branch_merge",
    )(o_gla, o_nsa, o_ssd, pa, pa, pa, wb, wb, wb)


def _ffn_kernel(x_ref, g_ref, gf_ref, wg_ref, wu_ref, wd_ref, o_ref, h_ref, *, final_norm):
    j = pl.program_id(1)

    @pl.when(j == 0)
    def _():
        x = x_ref[...]
        h_ref[...] = _rms_rows(x, g_ref[...]).astype(BF16)
        o_ref[...] = x

    h = h_ref[...]
    a = _silu(_dot(h, wg_ref[...])) * _dot(h, wu_ref[...])
    o_ref[...] += _dot(a.astype(BF16), wd_ref[...])

    if final_norm:
        @pl.when(j == pl.num_programs(1) - 1)
        def _():
            o_ref[...] = _rms_rows(o_ref[...], gf_ref[...])


def _ffn(x2d, gain, gain_final, wg, wu, wd, tm, tf, final_norm):
    t, d = x2d.shape
    f = wg.shape[1]
    return pl.pallas_call(
        functools.partial(_ffn_kernel, final_norm=final_norm),
        out_shape=jax.ShapeDtypeStruct((t, d), F32),
        grid=(t // tm, f // tf),
        in_specs=[pl.BlockSpec((tm, d), lambda i, j: (i, 0)),
                  pl.BlockSpec((1, d), lambda i, j: (0, 0)),
                  pl.BlockSpec((1, d), lambda i, j: (0, 0)),
                  pl.BlockSpec((d, tf), lambda i, j: (0, j)),
                  pl.BlockSpec((d, tf), lambda i, j: (0, j)),
                  pl.BlockSpec((tf, d), lambda i, j: (j, 0))],
        out_specs=pl.BlockSpec((tm, d), lambda i, j: (i, 0)),
        scratch_shapes=[pltpu.VMEM((tm, d), BF16)],
        compiler_params=_params(("parallel", "arbitrary")),
        name="ffn",
    )(x2d, gain.reshape(1, d), gain_final.reshape(1, d), wg, wu, wd)


GLA_CHUNK = 128
GLA_SUB = 16
GLA_EXP_CAP = 88.0
GLA_HEADS_PER_STEP = 4
GLA_TILE = 256


def _gla_kernel(q_ref, k_ref, v_ref, r_ref, sm_ref, w2_ref, gb_ref, ng_ref, o_ref, st_ref,
                *, chunks_per_tile):
    @pl.when(pl.program_id(2) == 0)
    def _():
        st_ref[...] = jnp.zeros_like(st_ref)

    c_rows = GLA_CHUNK
    row = lax.broadcasted_iota(jnp.int32, (c_rows, c_rows), 0)
    col = lax.broadcasted_iota(jnp.int32, (c_rows, c_rows), 1)
    tri = (col <= row).astype(BF16)
    srow = lax.broadcasted_iota(jnp.int32, (GLA_SUB, c_rows), 0)
    scol = lax.broadcasted_iota(jnp.int32, (GLA_SUB, c_rows), 1)
    scale = GLA_HK ** -0.5

    smb = sm_ref[...].astype(BF16)
    work = []
    for h in range(GLA_HEADS_PER_STEP):
        kcols = slice(h * GLA_HK, (h + 1) * GLA_HK)
        vcols = slice(h * GLA_HV, (h + 1) * GLA_HV)
        ga = _dot(smb, w2_ref[:, kcols]) + gb_ref[:, kcols]
        log_a_all = -_softplus(-ga) * (1.0 / GLA_GATE_NORMALIZER)
        for c in range(chunks_per_tile):
            rows = slice(c * c_rows, (c + 1) * c_rows)
            bc = _sel_dot(tri, log_a_all[rows])
            q = q_ref[rows, kcols] * scale
            k = k_ref[rows, kcols]
            vb = v_ref[rows, vcols].astype(BF16)
            parts = []
            for i in range(c_rows // GLA_SUB):
                lo = i * GLA_SUB
                ref_pt = bc[lo - 1:lo, :] if i > 0 else jnp.zeros((1, GLA_HK), F32)
                hi = lo + GLA_SUB
                kt = (k[:hi] * jnp.exp(jnp.minimum(ref_pt - bc[:hi], GLA_EXP_CAP))).astype(BF16)
                if hi < c_rows:
                    kt = jnp.concatenate([kt, jnp.zeros((c_rows - hi, GLA_HK), BF16)], axis=0)
                qi = (q[lo:lo + GLA_SUB, :] * jnp.exp(bc[lo:lo + GLA_SUB, :] - ref_pt)).astype(BF16)
                sc = _dot_nt(qi, kt)
                parts.append(jnp.where(scol <= srow + lo, sc, 0.0))
            intra = _dot(jnp.concatenate(parts, axis=0).astype(BF16), vb)
            b_last = bc[c_rows - 1:c_rows, :]
            khat = (k * jnp.exp(b_last - bc)).astype(BF16)
            work.append((h, rows, vcols, (q * jnp.exp(bc)).astype(BF16), intra,
                         jnp.exp(b_last), _dot_tn(vb, khat)))

    states = [st_ref[h] for h in range(GLA_HEADS_PER_STEP)]
    for h, rows, vcols, qe, intra, decay, incr in work:
        out = _dot_nt(qe, states[h].astype(BF16)) + intra
        states[h] = states[h] * decay + incr
        y = _rms_rows(out, ng_ref[...]) * _silu(r_ref[rows, vcols])
        o_ref[rows, vcols] = y.astype(o_ref.dtype)
    for h in range(GLA_HEADS_PER_STEP):
        st_ref[h] = states[h]


def _gla(pa, pb, ps, w2pad, gate_b, norm_g, bsz, seq, tile):
    t = bsz * seq
    nt = seq // tile
    hps = GLA_HEADS_PER_STEP
    kw, vw = hps * GLA_HK, hps * GLA_HV
    qb, kb = A_GQ // kw, A_GK // kw
    vb, rb = A_GV // vw, B_GR // vw
    return pl.pallas_call(
        functools.partial(_gla_kernel, chunks_per_tile=tile // GLA_CHUNK),
        out_shape=jax.ShapeDtypeStruct((t, GLA_DV), BF16),
        grid=(bsz, GLA_HEADS // hps, nt),
        in_specs=[pl.BlockSpec((tile, kw), lambda b, h, c: (b * nt + c, qb + h)),
                  pl.BlockSpec((tile, kw), lambda b, h, c: (b * nt + c, kb + h)),
                  pl.BlockSpec((tile, vw), lambda b, h, c: (b * nt + c, vb + h)),
                  pl.BlockSpec((tile, vw), lambda b, h, c: (b * nt + c, rb + h)),
                  pl.BlockSpec((tile, LANES), lambda b, h, c: (b * nt + c, 0)),
                  pl.BlockSpec((LANES, kw), lambda b, h, c: (0, h)),
                  pl.BlockSpec((1, kw), lambda b, h, c: (0, h)),
                  pl.BlockSpec((1, GLA_HV), lambda b, h, c: (0, 0))],
        out_specs=pl.BlockSpec((tile, vw), lambda b, h, c: (b * nt + c, h)),
        scratch_shapes=[pltpu.VMEM((hps, GLA_HV, GLA_HK), F32)],
        compiler_params=_params(("parallel", "parallel", "arbitrary")),
        name="gla_mixer",
    )(pa, pa, pa, pb, ps, w2pad, gate_b.reshape(1, GLA_DK), norm_g.reshape(1, GLA_HV))


SSD_STEP_ROWS = 1024
SSD_SUB = 128
SSD_HIST = 8


def _causal_conv(e_ref, src_ref, w_ref, b_ref):
    n = src_ref.shape[0]
    e_ref[pl.ds(SSD_HIST, n), :] = src_ref[...]
    acc = src_ref[...] * w_ref[SSD_CONV - 1:SSD_CONV, :] + b_ref[...]
    for s in range(1, SSD_CONV):
        acc = acc + e_ref[pl.ds(SSD_HIST - s, n), :] * w_ref[SSD_CONV - 1 - s:SSD_CONV - s, :]
    e_ref[pl.ds(0, SSD_HIST), :] = src_ref[pl.ds(n - SSD_HIST, SSD_HIST), :]
    return _silu(acc)


def _ssd_kernel(xs_ref, bm_ref, cm_ref, z_ref, dtc_ref, dtr_ref,
                wx_ref, wb_ref, wc_ref, bx_ref, bb_ref, bc_ref,
                dbc_ref, dbr_ref, alc_ref, alr_ref, dsk_ref, ng_ref,
                o_ref, st_ref, ex_ref, eb_ref, ec_ref):
    n = xs_ref.shape[0]
    sub = min(SSD_SUB, n)
    hpg, hd = SSD_HEADS_PER_GROUP, SSD_HEAD_DIM

    @pl.when(pl.program_id(2) == 0)
    def _():
        st_ref[...] = jnp.zeros_like(st_ref)
        ex_ref[pl.ds(0, SSD_HIST), :] = jnp.zeros((SSD_HIST, ex_ref.shape[1]), F32)
        eb_ref[pl.ds(0, SSD_HIST), :] = jnp.zeros((SSD_HIST, eb_ref.shape[1]), F32)
        ec_ref[pl.ds(0, SSD_HIST), :] = jnp.zeros((SSD_HIST, ec_ref.shape[1]), F32)

    xs = _causal_conv(ex_ref, xs_ref, wx_ref, bx_ref)
    bmb = _causal_conv(eb_ref, bm_ref, wb_ref, bb_ref).astype(BF16)
    cmb = _causal_conv(ec_ref, cm_ref, wc_ref, bc_ref).astype(BF16)

    w3 = 3 * hpg
    krow = lax.broadcasted_iota(jnp.int32, (w3, hpg * hd), 0)
    kcol = lax.broadcasted_iota(jnp.int32, (w3, hpg * hd), 1) >> SSD_HEAD_SHIFT
    expand3 = ((krow == kcol) | (krow == kcol + hpg) | (krow == kcol + 2 * hpg)).astype(BF16)

    def expand_heads(x):
        hi, mid, lo = _split3(x)
        klane = lax.broadcasted_iota(jnp.int32, x.shape, 1)
        return _dot(jnp.where(klane < hpg, hi, jnp.where(klane < 2 * hpg, mid, lo)), expand3)

    dt_c = _softplus(dtc_ref[0] + dbc_ref[0])
    a_c = -jnp.exp(alc_ref[0])
    a_r = -jnp.exp(alr_ref[0])
    xdt = xs * expand_heads(dt_c)

    row = lax.broadcasted_iota(jnp.int32, (sub, sub), 0)
    col = lax.broadcasted_iota(jnp.int32, (sub, sub), 1)
    tri = (col <= row).astype(BF16)
    triu = (row <= col).astype(BF16)
    causal_bias = jnp.where(col <= row, 0.0, NEG_INF)
    lane = lax.broadcasted_iota(jnp.int32, (sub, 2 * hd), 1)

    for si in range(n // sub):
        rows = slice(si * sub, (si + 1) * sub)
        acum_c = _sel_dot(tri, dt_c[rows] * a_c)
        adt_r = _softplus(dtr_ref[0, si] + dbr_ref[0]) * a_r
        acum_r = _sel_dot_left(adt_r, triu)
        a_last = acum_c[sub - 1:sub, :]
        eac_x = expand_heads(jnp.exp(acum_c))
        dec_x = expand_heads(jnp.exp(a_last - acum_c))
        sdec_x = expand_heads(jnp.broadcast_to(jnp.exp(a_last), (8, w3)))[0:1, :]
        ac2 = acum_c * LOG2_E
        ar2 = acum_r * LOG2_E

        xd, cs, bs = xdt[rows], cmb[rows], bmb[rows]
        cb = _dot_nt(cs, bs)
        pair_out = []
        for pr in range(hpg // 2):
            xp = xd[:, pr * 2 * hd:(pr + 1) * 2 * hd]
            acc = None
            for half in range(2):
                r = 2 * pr + half
                seg = jnp.exp2(ac2[:, r:r + 1] - ar2[r:r + 1, :] + causal_bias)
                rhs = jnp.where((lane >> SSD_HEAD_SHIFT) == half, xp, 0.0).astype(BF16)
                term = _dot((cb * seg).astype(BF16), rhs)
                acc = term if acc is None else acc + term
            pair_out.append(acc)
        y_diag = jnp.concatenate(pair_out, axis=1)

        st = st_ref[...]
        y_off = _dot(cs, st.astype(BF16)) * eac_x
        st_ref[...] = st * sdec_x + _dot_tn(bs, (xd * dec_x).astype(BF16))

        y = (y_diag + y_off + xs[rows] * dsk_ref[0]) * _silu(z_ref[rows, :])
        o_ref[rows, :] = _rms_rows(y, ng_ref[...]).astype(o_ref.dtype)


def _ssd(pc, dt_c, dt_r, conv_w, conv_b, dt_bias, a_log, d_skip, norm_g, bsz, seq, chunk):
    t = bsz * seq
    nc = seq // chunk
    sub = min(SSD_SUB, chunk)
    g, hpg, gw, ns = SSD_GROUPS, SSD_HEADS_PER_GROUP, SSD_GROUP_WIDTH, SSD_D_STATE
    xsb = C_SXBC // gw
    bmb = (C_SXBC + SSD_D_INNER) // ns
    cmb = (C_SXBC + SSD_D_INNER + g * ns) // ns
    zb = C_SZ // gw
    cw_b = SSD_D_INNER // ns
    cw_c = (SSD_D_INNER + g * ns) // ns
    conv_b2 = conv_b.reshape(1, SSD_CONV_DIM)
    dbc = jnp.tile(dt_bias.reshape(g, 1, hpg), (1, 1, 3))
    dbr = dt_bias.reshape(g, hpg, 1)
    alc = jnp.tile(a_log.reshape(g, 1, hpg), (1, 1, 3))
    alr = a_log.reshape(g, hpg, 1)
    dsk = jnp.repeat(d_skip, SSD_HEAD_DIM).reshape(g, 1, gw)
    ng = norm_g.reshape(1, SSD_D_INNER)
    return pl.pallas_call(
        _ssd_kernel,
        out_shape=jax.ShapeDtypeStruct((t, SSD_D_INNER), BF16),
        grid=(bsz, g, nc),
        in_specs=[pl.BlockSpec((chunk, gw), lambda b, gi, c: (b * nc + c, xsb + gi)),
                  pl.BlockSpec((chunk, ns), lambda b, gi, c: (b * nc + c, bmb + gi)),
                  pl.BlockSpec((chunk, ns), lambda b, gi, c: (b * nc + c, cmb + gi)),
                  pl.BlockSpec((chunk, gw), lambda b, gi, c: (b * nc + c, zb + gi)),
                  pl.BlockSpec((1, chunk, 3 * hpg), lambda b, gi, c: (gi, b * nc + c, 0)),
                  pl.BlockSpec((1, chunk // sub, hpg, sub), lambda b, gi, c: (gi, b * nc + c, 0, 0)),
                  pl.BlockSpec((SSD_CONV, gw), lambda b, gi, c: (0, gi)),
                  pl.BlockSpec((SSD_CONV, ns), lambda b, gi, c: (0, cw_b + gi)),
                  pl.BlockSpec((SSD_CONV, ns), lambda b, gi, c: (0, cw_c + gi)),
                  pl.BlockSpec((1, gw), lambda b, gi, c: (0, gi)),
                  pl.BlockSpec((1, ns), lambda b, gi, c: (0, cw_b + gi)),
                  pl.BlockSpec((1, ns), lambda b, gi, c: (0, cw_c + gi)),
                  pl.BlockSpec((1, 1, 3 * hpg), lambda b, gi, c: (gi, 0, 0)),
                  pl.BlockSpec((1, hpg, 1), lambda b, gi, c: (gi, 0, 0)),
                  pl.BlockSpec((1, 1, 3 * hpg), lambda b, gi, c: (gi, 0, 0)),
                  pl.BlockSpec((1, hpg, 1), lambda b, gi, c: (gi, 0, 0)),
                  pl.BlockSpec((1, 1, gw), lambda b, gi, c: (gi, 0, 0)),
                  pl.BlockSpec((1, gw), lambda b, gi, c: (0, gi))],
        out_specs=pl.BlockSpec((chunk, gw), lambda b, gi, c: (b * nc + c, gi)),
        scratch_shapes=[pltpu.VMEM((ns, gw), F32),
                        pltpu.VMEM((SSD_HIST + chunk, gw), F32),
                        pltpu.VMEM((SSD_HIST + chunk, ns), F32),
                        pltpu.VMEM((SSD_HIST + chunk, ns), F32)],
        compiler_params=_params(("parallel", "parallel", "arbitrary")),
        name="ssd_mixer",
    )(pc, pc, pc, pc, dt_c, dt_r, conv_w, conv_w, conv_w, conv_b2, conv_b2, conv_b2,
      dbc, dbr, alc, alr, dsk, ng)


NSA_TQ = 256
NSA_KB = 512


def _rope(x, cos, sin):
    lane = lax.broadcasted_iota(jnp.int32, x.shape, 1)
    half = ROPE_DIM // 2
    swapped = jnp.where(lane < half, pltpu.roll(x, LANES - half, axis=1), pltpu.roll(x, half, axis=1))
    return x * cos + swapped * sin


def _compress(src_ref, pe_ref, w1_ref, w2_ref, nblk):
    hd = NSA_HEAD_DIM
    a0 = jnp.zeros((nblk, CMP_HIDDEN), F32)
    a1 = jnp.zeros((nblk, CMP_HIDDEN), F32)
    for c in range(CMP_STRIDE):
        xc = src_ref[pl.ds(c, nblk, stride=CMP_STRIDE), :]
        a0 = a0 + _dot((xc + pe_ref[c:c + 1, :]).astype(BF16), w1_ref[c * hd:(c + 1) * hd, :])
        c1 = CMP_STRIDE + c
        a1 = a1 + _dot((xc + pe_ref[c1:c1 + 1, :]).astype(BF16), w1_ref[c1 * hd:(c1 + 1) * hd, :])
    hid = a0 + pltpu.roll(a1, nblk - 1, axis=0)
    return _dot(_silu(hid).astype(BF16), w2_ref[...])


def _nsa_kernel(q_ref, kc_ref, vc_ref, ks_ref, vs_ref, kw_ref, vw_ref, g_ref,
                cos_ref, sin_ref, pek_ref, pev_ref, w1k_ref, w2k_ref, w1v_ref, w2v_ref, ovt_ref,
                o_ref, kcmp_s, vcmp_s, ksel_s, vsel_s, kwin_s, vwin_s, krope_s,
                sbuf_s, mx_s, acc_s, *, seq):
    tq, hd, nr = NSA_TQ, NSA_HEAD_DIM, NSA_Q_PER_KV
    i = pl.program_id(2)
    nblk = seq // CMP_STRIDE
    nsel = seq // SEL_BLOCK
    scale = hd ** -0.5

    @pl.when(i == 0)
    def _():
        cos, sin = cos_ref[...], sin_ref[...]
        krope_s[...] = _rope(kc_ref[...], cos, sin)
        ksel_s[...] = _rope(ks_ref[...], cos, sin).astype(BF16)
        kwin_s[...] = _rope(kw_ref[...], cos, sin).astype(BF16)
        vsel_s[:, :hd] = vs_ref[...].astype(BF16)
        vsel_s[:, hd:] = jnp.ones((seq, hd), BF16)
        vwin_s[:, :hd] = vw_ref[...].astype(BF16)
        vwin_s[:, hd:] = jnp.ones((seq, hd), BF16)
        kcmp_s[...] = _compress(krope_s, pek_ref, w1k_ref, w2k_ref, nblk).astype(BF16)
        vcmp_s[:, :hd] = _compress(vc_ref, pev_ref, w1v_ref, w2v_ref, nblk).astype(BF16)
        vcmp_s[:, hd:] = jnp.ones((nblk, hd), BF16)

    t0 = pl.multiple_of(i * tq, tq)
    cos = cos_ref[pl.ds(t0, tq), :]
    sin = sin_ref[pl.ds(t0, tq), :]
    qs = [_rope(q_ref[:, r * hd:(r + 1) * hd], cos, sin) for r in range(nr)]
    qb = jnp.concatenate(qs, axis=0).astype(BF16)
    c2 = scale * LOG2_E
    tpos = t0 + lax.broadcasted_iota(jnp.int32, (tq, 1), 0)

    ncol = lax.broadcasted_iota(jnp.int32, (1, nblk), 1)
    ok = (ncol * CMP_STRIDE + (CMP_LEN - 1)) <= tpos
    bias_c = jnp.where(ok, 0.0, NEG_INF)
    s = _dot_nt(qb, kcmp_s[...]).reshape(nr, tq, nblk) + bias_c[None]
    e = jnp.exp2((s - jnp.max(s, axis=-1, keepdims=True)) * c2)
    oc = _dot(e.reshape(nr * tq, nblk).astype(BF16), vcmp_s[...]).reshape(nr, tq, 2 * hd)
    inv_lc = 1.0 / oc[:, :, hd:]
    has_cmp = (tpos >= CMP_LEN - 1)[None]
    o_cmp = jnp.where(has_cmp, oc[:, :, :hd] * inv_lc, 0.0)
    p_cmp = jnp.where(ok[None], e * inv_lc[:, :, :nblk], 0.0)

    p_sum = p_cmp[0]
    for r in range(1, nr):
        p_sum = p_sum + p_cmp[r]
    ph = p_sum.astype(BF16)
    pm = (p_sum - ph.astype(F32)).astype(BF16)
    ovt = ovt_ref[...]
    p_slc = _dot_nt(ovt, ph) + _dot_nt(ovt, pm)
    jrow = lax.broadcasted_iota(jnp.int32, (nsel, tq), 0)
    blk_t = (t0 + lax.broadcasted_iota(jnp.int32, (nsel, tq), 1)) >> SEL_SHIFT
    forced = (jrow == 0) | (jrow == blk_t) | (jrow == blk_t - 1)
    score = jnp.where(forced, FORCE_SCORE, jnp.where(jrow <= blk_t, p_slc, NEG_INF))
    rank = jnp.zeros((nsel, tq), F32)
    for j in range(nsel):
        sj = score[j:j + 1, :]
        beats = jnp.where(sj > score, 1.0, jnp.where((sj == score) & (jrow > j), 1.0, 0.0))
        rank = rank + beats
    sel_t = jnp.where(rank < float(min(SEL_TOPK, nsel)), 1.0, 0.0)
    sel = jnp.transpose(sel_t).astype(BF16)

    wlen = min(WINDOW + tq, seq)
    w0 = pl.multiple_of(jnp.maximum(t0 + tq - wlen, 0), tq)
    ktw = kwin_s[pl.ds(w0, wlen), :]
    vtw = vwin_s[pl.ds(w0, wlen), :]
    kposw = w0 + lax.broadcasted_iota(jnp.int32, (1, wlen), 1)
    bias_w = jnp.where((kposw <= tpos) & (kposw > tpos - WINDOW), 0.0, NEG_INF)
    sw = _dot_nt(qb, ktw).reshape(nr, tq, wlen) + bias_w[None]
    ew = jnp.exp2((sw - jnp.max(sw, axis=-1, keepdims=True)) * c2)
    ow = _dot(ew.reshape(nr * tq, wlen).astype(BF16), vtw).reshape(nr, tq, 2 * hd)
    o_win = ow[:, :, :hd] * (1.0 / ow[:, :, hd:])

    ng = nr * 3
    gs = _sigmoid(g_ref[0])
    g_hi, g_mid, g_lo = _split3(gs)
    glane = lax.broadcasted_iota(jnp.int32, gs.shape, 1)
    g_terms = jnp.where(glane < ng, g_hi, jnp.where(glane < 2 * ng, g_mid, g_lo))
    grow = lax.broadcasted_iota(jnp.int32, (3 * ng, ng * hd), 0)
    gcol = lax.broadcasted_iota(jnp.int32, (3 * ng, ng * hd), 1) >> NSA_HEAD_SHIFT
    gsel = (grow == gcol) | (grow == gcol + ng) | (grow == gcol + 2 * ng)
    gexp = _dot(g_terms, gsel.astype(BF16))
    gates = [gexp[:, j * hd:(j + 1) * hd] for j in range(ng)]
    o_cw = [gates[3 * r] * o_cmp[r] + gates[3 * r + 2] * o_win[r] for r in range(nr)]

    kb = min(NSA_KB, seq)
    n_kb = (t0 + tq + kb - 1) // kb
    nlb = kb // LANES
    mx_s[...] = jnp.full(mx_s.shape, NEG_INF, F32)
    acc_s[...] = jnp.zeros(acc_s.shape, F32)

    def lane_blocks(x):
        return [x[:, b * LANES:(b + 1) * LANES] for b in range(nlb)]

    def score_step(c, carry):
        k0 = pl.multiple_of(c * kb, kb)
        erow = lax.broadcasted_iota(jnp.int32, (nsel, kb), 0)
        ecol = lax.broadcasted_iota(jnp.int32, (nsel, kb), 1)
        expand = (((k0 + ecol) >> SEL_SHIFT) == erow).astype(BF16)
        chosen = _dot(sel, expand)
        kpos = k0 + lax.broadcasted_iota(jnp.int32, (1, kb), 1)
        bias = jnp.where((chosen > 0.5) & (kpos <= tpos), 0.0, NEG_INF)
        sb = _dot_nt(qb, ksel_s[pl.ds(k0, kb), :]).reshape(nr, tq, kb) + bias[None]
        sb = sb.reshape(nr * tq, kb)
        sbuf_s[c] = sb
        mx = mx_s[...]
        for blk in lane_blocks(sb):
            mx = jnp.maximum(mx, blk)
        mx_s[...] = mx
        return carry

    lax.fori_loop(0, n_kb, score_step, 0)
    m_sel = jnp.max(mx_s[...], axis=-1, keepdims=True)

    def prob_step(c, carry):
        k0 = pl.multiple_of(c * kb, kb)
        pt = jnp.exp2((sbuf_s[c] - m_sel) * c2)
        acc_s[...] += _dot(pt.astype(BF16), vsel_s[pl.ds(k0, kb), :])
        return carry

    lax.fori_loop(0, n_kb, prob_step, 0)
    o_sel = (acc_s[:, :hd] * (1.0 / acc_s[:, hd:])).reshape(nr, tq, hd)

    for r in range(nr):
        o = o_cw[r] + gates[3 * r + 1] * o_sel[r]
        o_ref[:, r * hd:(r + 1) * hd] = o.astype(o_ref.dtype)


def _nsa_tables(seq):
    half = ROPE_DIM // 2
    inv_freq = ROPE_THETA ** (-jnp.arange(half, dtype=F32) / half)
    ang = jnp.arange(seq).astype(F32)[:, None] * inv_freq[None, :]
    cos, sin = jnp.cos(ang), jnp.sin(ang)
    rest = NSA_HEAD_DIM - ROPE_DIM
    cos_t = jnp.concatenate([cos, cos, jnp.ones((seq, rest), F32)], axis=1)
    sin_t = jnp.concatenate([-sin, sin, jnp.zeros((seq, rest), F32)], axis=1)
    n_sel = seq // SEL_BLOCK
    cmp_starts = np.arange(seq // CMP_STRIDE) * CMP_STRIDE
    sel_starts = np.arange(n_sel) * SEL_BLOCK
    overlap = np.clip(np.minimum(cmp_starts[:, None] + CMP_LEN, sel_starts[None, :] + SEL_BLOCK)
                      - np.maximum(cmp_starts[:, None], sel_starts[None, :]), 0, None).astype(np.float32) / CMP_LEN
    return cos_t, sin_t, jnp.asarray(overlap.T, BF16)


def _nsa(pb, gates, pos_k, pos_v, w1k, w2k, w1v, w2v, bsz, seq):
    t = bsz * seq
    tq, hd, g, nr = NSA_TQ, NSA_HEAD_DIM, NSA_KV_GROUPS, NSA_Q_PER_KV
    nq = seq // tq
    nblk = seq // CMP_STRIDE
    nsel = seq // SEL_BLOCK
    cos_t, sin_t, ovt = _nsa_tables(seq)
    kb = min(NSA_KB, seq)
    qblk = B_NQ // (nr * hd)
    kvb = B_NKV // hd

    def kv_spec(split):
        return pl.BlockSpec((seq, hd), lambda b, gi, i: (b, kvb + split * g + gi))

    const2 = lambda b, gi, i: (0, 0)
    return pl.pallas_call(
        functools.partial(_nsa_kernel, seq=seq),
        out_shape=jax.ShapeDtypeStruct((t, NSA_WIDTH), BF16),
        grid=(bsz, g, nq),
        in_specs=[pl.BlockSpec((tq, nr * hd), lambda b, gi, i: (b * nq + i, qblk + gi))]
                 + [kv_spec(sp) for sp in range(6)]
                 + [pl.BlockSpec((1, tq, nr * 9), lambda b, gi, i: (gi, b * nq + i, 0)),
                    pl.BlockSpec((seq, hd), const2),
                    pl.BlockSpec((seq, hd), const2),
                    pl.BlockSpec((CMP_LEN, hd), const2),
                    pl.BlockSpec((CMP_LEN, hd), const2),
                    pl.BlockSpec((CMP_LEN * hd, CMP_HIDDEN), const2),
                    pl.BlockSpec((CMP_HIDDEN, hd), const2),
                    pl.BlockSpec((CMP_LEN * hd, CMP_HIDDEN), const2),
                    pl.BlockSpec((CMP_HIDDEN, hd), const2),
                    pl.BlockSpec((nsel, nblk), const2)],
        out_specs=pl.BlockSpec((tq, nr * hd), lambda b, gi, i: (b * nq + i, gi)),
        scratch_shapes=[pltpu.VMEM((nblk, hd), BF16), pltpu.VMEM((nblk, 2 * hd), BF16),
                        pltpu.VMEM((seq, hd), BF16), pltpu.VMEM((seq, 2 * hd), BF16),
                        pltpu.VMEM((seq, hd), BF16), pltpu.VMEM((seq, 2 * hd), BF16),
                        pltpu.VMEM((seq, hd), F32),
                        pltpu.VMEM((seq // kb, nr * tq, kb), F32),
                        pltpu.VMEM((nr * tq, LANES), F32),
                        pltpu.VMEM((nr * tq, 2 * hd), F32)],
        compiler_params=_params(("parallel", "parallel", "arbitrary")),
        name="nsa_mixer",
    )(pb, pb, pb, pb, pb, pb, pb, gates, cos_t, sin_t, pos_k, pos_v, w1k, w2k, w1v, w2v, ovt)


def _in_proj(x2d, gain, w_in_stack, layer):
    t = x2d.shape[0]
    tm = min(PROJ_TILE_M, t)
    h = _rms_cast(x2d, gain, tm)
    pa = _proj(h, w_in_stack, layer, 0, PA_COLS, tm, PROJ_TILE_N)
    pb = _proj(h, w_in_stack, layer, PB_START, PB_COLS, tm, PROJ_TILE_N)
    pc = _proj(h, w_in_stack, layer, PC_START, PC_COLS, tm, PROJ_TILE_N)
    ps = _proj_small(h, w_in_stack, layer, tm)
    return pa, pb, pc, ps


def _small_views(ps):
    t = ps.shape[0]
    nsa_g = ps[:, S_NG:S_NG + 3 * NSA_HEADS]
    nsa_g = nsa_g.reshape(t, NSA_KV_GROUPS, NSA_Q_PER_KV * 3).transpose(1, 0, 2)
    nsa_g = jnp.tile(nsa_g, (1, 1, 3))
    dt = ps[:, S_DT:S_DT + SSD_HEADS].reshape(t, SSD_GROUPS, SSD_HEADS_PER_GROUP)
    sub = min(SSD_SUB, t)
    dt_r = dt.reshape(t // sub, sub, SSD_GROUPS, SSD_HEADS_PER_GROUP).transpose(2, 0, 3, 1)
    return nsa_g, jnp.tile(dt.transpose(1, 0, 2), (1, 1, 3)), dt_r


def _layer(x2d, bsz, seq, layer, norm_mix, w_in, gla_gate_w2, gla_gate_b, gla_out_norm,
           nsa_cmp_pos_k, nsa_cmp_pos_v, nsa_cmp_k_w1, nsa_cmp_k_w2, nsa_cmp_v_w1, nsa_cmp_v_w2,
           ssd_conv_w, ssd_conv_b, ssd_dt_bias, ssd_a_log, ssd_d, ssd_out_norm,
           w_branch, w_out, norm_ffn, w_ffn_gate, w_ffn_up, w_ffn_down, norm_final, final_norm):
    t = bsz * seq
    pa, pb, pc, ps = _in_proj(x2d, norm_mix, w_in, layer)
    nsa_g, dt_c, dt_r = _small_views(ps)

    w2pad = jnp.zeros((LANES, GLA_DK), BF16).at[S_GLOW:S_GLOW + GLA_LOWRANK].set(
        gla_gate_w2.astype(BF16))
    o_gla = _gla(pa, pb, ps, w2pad, gla_gate_b, gla_out_norm, bsz, seq, min(GLA_TILE, seq))
    o_nsa = _nsa(pb, nsa_g, nsa_cmp_pos_k, nsa_cmp_pos_v,
                 nsa_cmp_k_w1.astype(BF16), nsa_cmp_k_w2.astype(BF16),
                 nsa_cmp_v_w1.astype(BF16), nsa_cmp_v_w2.astype(BF16), bsz, seq)
    o_ssd = _ssd(pc, dt_c, dt_r, ssd_conv_w, ssd_conv_b, ssd_dt_bias, ssd_a_log, ssd_d,
                 ssd_out_norm, bsz, seq, min(SSD_STEP_ROWS, seq))

    merged = _merge(o_gla, o_nsa, o_ssd, pa, w_branch.astype(BF16), min(MERGE_TILE_M, t),
                    MERGE_TILE_N)
    tm = min(PROJ_TILE_M, t)
    x2d = _proj(merged, w_out, layer, 0, D_MODEL, tm, PROJ_TILE_N, residual=x2d, name="out_proj")
    return _ffn(x2d, norm_ffn, norm_final, w_ffn_gate.astype(BF16), w_ffn_up.astype(BF16),
                w_ffn_down.astype(BF16), min(FFN_TILE_M, t), FFN_TILE_F, final_norm)


def kernel(x, norm_mix, w_in, gla_gate_w2, gla_gate_b, gla_out_norm, nsa_cmp_pos_k, nsa_cmp_pos_v,
           nsa_cmp_k_w1, nsa_cmp_k_w2, nsa_cmp_v_w1, nsa_cmp_v_w2, ssd_conv_w, ssd_conv_b,
           ssd_dt_bias, ssd_a_log, ssd_d, ssd_out_norm, w_branch, w_out, norm_ffn, w_ffn_gate,
           w_ffn_up, w_ffn_down, norm_final):
    bsz, seq, d = x.shape
    depth = norm_mix.shape[0]
    x2d = x.reshape(bsz * seq, d)
    for l in range(depth):
        x2d = _layer(x2d, bsz, seq, l, norm_mix[l], w_in, gla_gate_w2[l], gla_gate_b[l],
                     gla_out_norm[l], nsa_cmp_pos_k[l], nsa_cmp_pos_v[l], nsa_cmp_k_w1[l],
                     nsa_cmp_k_w2[l], nsa_cmp_v_w1[l], nsa_cmp_v_w2[l], ssd_conv_w[l],
                     ssd_conv_b[l], ssd_dt_bias[l], ssd_a_log[l], ssd_d[l], ssd_out_norm[l],
                     w_branch[l], w_out, norm_ffn[l], w_ffn_gate[l], w_ffn_up[l],
                     w_ffn_down[l], norm_final, l == depth - 1)
    return x2d.reshape(bsz, seq, d)
```
